```python
import math
import jax, jax.numpy as jnp
from jax import lax
import numpy as np

D_MODEL = 1024
BATCH = 4
SEQ = 4096
DEPTH = 2
DEC_BATCH = 32
DEC_SEQ = 8
PAST_LEN = 8192
PAGE_SIZE = 128

N_EVEN = (DEPTH + 1) // 2
N_ODD = DEPTH // 2
A_WIDTH = D_MODEL // 2
CHUNK = 128
A_GROUPS = 4
A_GROUP_DIM = A_WIDTH // A_GROUPS
B_HEADS = 4
B_QK_DIM = 64
B_V_DIM = 2 * B_QK_DIM
B_WIDTH = B_HEADS * B_V_DIM
IN_EVEN_WIDTH = 2 * A_WIDTH + 2 * B_HEADS * 2 * B_QK_DIM + B_WIDTH
Q_BLOCK = 128
SCALE = B_QK_DIM ** -0.5
SSM_GROUP = 16
SSM_GROUPS = D_MODEL // SSM_GROUP
SSM_STATE = 64
D_FF = ((8 * D_MODEL // 3 + 255) // 256) * 256
EPS = 1e-6

kernel_name = 'hybrid_sgu_diffattn_s5_step'


def rmsnorm(x, g):
    xf = x.astype(jnp.float32)
    y = xf * lax.rsqrt(jnp.mean(xf * xf, axis=-1, keepdims=True) + EPS)
    return (y * g.astype(jnp.float32)).astype(x.dtype)


def swiglu(h, w_in, w_out):
    gate, up = jnp.split(h @ w_in, 2, axis=-1)
    return (jax.nn.silu(gate) * up) @ w_out


def even_project(h, w_in, sgu_norm):
    b, t, _ = h.shape
    p = h @ w_in
    uv = jax.nn.gelu(p[..., :2 * A_WIDTH])
    u = uv[..., :A_WIDTH]
    v = rmsnorm(uv[..., A_WIDTH:], sgu_norm)
    o = 2 * A_WIDTH
    qk = B_HEADS * 2 * B_QK_DIM
    q = p[..., o:o + qk].reshape(b, t, B_HEADS, 2 * B_QK_DIM)
    k = p[..., o + qk:o + 2 * qk].reshape(b, t, B_HEADS, 2 * B_QK_DIM)
    vb = p[..., o + 2 * qk:].reshape(b, t, B_HEADS, B_V_DIM)
    return u, v, q, k, vb


def spatial_gate(v, sgu_w, sgu_b):
    t = v.shape[2]
    w = jnp.where(jnp.tril(jnp.ones((t, t), dtype=bool)), sgu_w[:, :t, :t], 0)
    vg = v.reshape(v.shape[:3] + (A_GROUPS, A_GROUP_DIM))
    out = jnp.einsum('gts,bnsgd->bntgd', w, vg) + sgu_b[:, :t].T[:, :, None]
    return out.reshape(v.shape)


def diff_lambda(lq1, lk1, lq2, lk2, lam_init):
    f32 = jnp.float32
    e1 = jnp.exp(jnp.sum(lq1.astype(f32) * lk1.astype(f32)))
    e2 = jnp.exp(jnp.sum(lq2.astype(f32) * lk2.astype(f32)))
    return e1 - e2 + lam_init


def diff_scores(q, k):
    s1 = jnp.einsum('bqhd,bkhd->bhqk', q[..., :B_QK_DIM], k[..., :B_QK_DIM])
    s2 = jnp.einsum('bqhd,bkhd->bhqk', q[..., B_QK_DIM:], k[..., B_QK_DIM:])
    return s1.astype(jnp.float32), s2.astype(jnp.float32)


def diff_attn_prompt(q, k, v, lam):
    b, s = q.shape[:2]
    q = q * SCALE
    k_pos = jnp.arange(s)

    def block(i):
        start = i * Q_BLOCK
        qb = lax.dynamic_slice_in_dim(q, start, Q_BLOCK, axis=1)
        s1, s2 = diff_scores(qb, k)
        mask = (start + jnp.arange(Q_BLOCK))[:, None] >= k_pos[None, :]
        p1 = jax.nn.softmax(jnp.where(mask, s1, -jnp.inf), axis=-1)
        p2 = jax.nn.softmax(jnp.where(mask, s2, -jnp.inf), axis=-1)
        return jnp.einsum('bhqk,bkhd->bqhd', (p1 - lam * p2).astype(v.dtype), v)

    o = lax.map(block, jnp.arange(s // Q_BLOCK))
    return o.transpose(1, 0, 2, 3, 4).reshape(b, s, B_HEADS, B_V_DIM)


def diff_attn_sample(q, k_new, v_new, k_past, v_past, lam):
    t = q.shape[1]
    n_past = k_past.shape[1]
    q = q * SCALE
    p1s, p2s = diff_scores(q, k_past)
    n1, n2 = diff_scores(q, k_new)
    mask = jnp.tril(jnp.ones((t, t), dtype=bool))
    p1 = jax.nn.softmax(jnp.concatenate([p1s, jnp.where(mask, n1, -jnp.inf)], axis=-1), axis=-1)
    p2 = jax.nn.softmax(jnp.concatenate([p2s, jnp.where(mask, n2, -jnp.inf)], axis=-1), axis=-1)
    a = (p1 - lam * p2).astype(v_new.dtype)
    return (jnp.einsum('bhqk,bkhd->bqhd', a[..., :n_past], v_past)
            + jnp.einsum('bhqk,bkhd->bqhd', a[..., n_past:], v_new))


def diff_head_out(o, subln, lam_init):
    b, t = o.shape[:2]
    return (rmsnorm(o, subln) * (1.0 - lam_init)).reshape(b, t, B_WIDTH)


def s5_scan(u, s0_re, s0_im, a_re, a_im, log_dt, b_re, b_im, c_re, c_im, d_skip):
    bsz, t, _ = u.shape
    f32 = jnp.float32
    a = lax.complex(a_re.astype(f32), a_im.astype(f32))
    dt = jnp.exp(log_dt.astype(f32))[:, None]
    a_bar = jnp.exp(a * dt)
    b_bar = ((a_bar - 1.0) / a)[:, :, None] * lax.complex(b_re.astype(f32), b_im.astype(f32))
    c = lax.complex(c_re.astype(f32), c_im.astype(f32))
    uc = u.astype(f32).reshape(bsz, t, SSM_GROUPS, SSM_GROUP)
    bu = jnp.einsum('gpc,btgc->btgp', b_bar, uc.astype(jnp.complex64))
    a_seq = jnp.broadcast_to(a_bar, bu.shape)

    def combine(left, right):
        a_l, b_l = left
        a_r, b_r = right
        return a_r * a_l, a_r * b_l + b_r

    a_cum, s = lax.associative_scan(combine, (a_seq, bu), axis=1)
    s = s + a_cum * lax.complex(s0_re.astype(f32), s0_im.astype(f32))[:, None]
    y = jnp.real(jnp.einsum('gcp,btgp->btgc', c, s)) + d_skip.astype(f32).reshape(SSM_GROUPS, SSM_GROUP) * uc
    s_last = s[:, -1]
    return y.reshape(bsz, t, D_MODEL).astype(u.dtype), jnp.real(s_last), jnp.imag(s_last)


def s5_glu(y, w_glu, b_glu):
    g = jax.nn.gelu(y)
    z = g @ w_glu + b_glu
    return z[..., :D_MODEL] * jax.nn.sigmoid(z[..., D_MODEL:])


def setup_inputs(seed: int = 0) -> dict:
    key = jax.random.key(seed)
    keys = jax.random.split(key, 48)

    def nrm(i, shape, scale=1.0):
        return jax.random.normal(keys[i], shape, jnp.float32) * scale

    n_pages = PAST_LEN // PAGE_SIZE
    n_used = DEC_BATCH * n_pages
    n_phys = n_used + max(1, n_used // 4)
    page_table = jax.random.permutation(keys[0], n_phys)[:n_used].reshape(DEC_BATCH, n_pages).astype(jnp.int32)
    a_im_init = math.pi * jnp.arange(SSM_STATE, dtype=jnp.float32)
    return {
        'x_prompt': nrm(1, (BATCH, SEQ, D_MODEL)),
        'x_sample': nrm(2, (DEC_BATCH, DEC_SEQ, D_MODEL)),
        'cache_k': nrm(3, (N_EVEN, n_phys, PAGE_SIZE, B_HEADS, 2 * B_QK_DIM)),
        'cache_v': nrm(4, (N_EVEN, n_phys, PAGE_SIZE, B_HEADS, B_V_DIM)),
        'page_table': page_table,
        'state_ssm_re': nrm(5, (N_ODD, DEC_BATCH, SSM_GROUPS, SSM_STATE), 0.1),
        'state_ssm_im': nrm(6, (N_ODD, DEC_BATCH, SSM_GROUPS, SSM_STATE), 0.1),
        'norm_mix': 1.0 + nrm(7, (DEPTH, D_MODEL), 0.02),
        'norm_ffn': 1.0 + nrm(8, (DEPTH, D_MODEL), 0.02),
        'norm_final': 1.0 + nrm(9, (D_MODEL,), 0.02),
        'w_in_even': nrm(10, (N_EVEN, D_MODEL, IN_EVEN_WIDTH), D_MODEL ** -0.5),
        'w_out_even': nrm(11, (N_EVEN, A_WIDTH + B_WIDTH, D_MODEL), (A_WIDTH + B_WIDTH) ** -0.5),
        'sgu_norm': 1.0 + nrm(12, (N_EVEN, A_WIDTH), 0.02),
        'sgu_w': nrm(13, (N_EVEN, A_GROUPS, CHUNK, CHUNK), CHUNK ** -0.5),
        'sgu_b': 1.0 + nrm(14, (N_EVEN, A_GROUPS, CHUNK), 0.02),
        'lambda_q1': nrm(15, (N_EVEN, B_QK_DIM), 0.1),
        'lambda_k1': nrm(16, (N_EVEN, B_QK_DIM), 0.1),
        'lambda_q2': nrm(17, (N_EVEN, B_QK_DIM), 0.1),
        'lambda_k2': nrm(18, (N_EVEN, B_QK_DIM), 0.1),
        'attn_subln': 1.0 + nrm(19, (N_EVEN, B_V_DIM), 0.02),
        'ssm_a_re': -0.5 + nrm(20, (N_ODD, SSM_GROUPS, SSM_STATE), 0.01),
        'ssm_a_im': a_im_init + nrm(21, (N_ODD, SSM_GROUPS, SSM_STATE), 0.01),
        'ssm_log_dt': jax.random.uniform(keys[22], (N_ODD, SSM_GROUPS), jnp.float32, math.log(1e-3), math.log(1e-1)),
        'ssm_b_re': nrm(23, (N_ODD, SSM_GROUPS, SSM_STATE, SSM_GROUP), (2 * SSM_GROUP) ** -0.5),
        'ssm_b_im': nrm(24, (N_ODD, SSM_GROUPS, SSM_STATE, SSM_GROUP), (2 * SSM_GROUP) ** -0.5),
        'ssm_c_re': nrm(25, (N_ODD, SSM_GROUPS, SSM_GROUP, SSM_STATE), SSM_STATE ** -0.5),
        'ssm_c_im': nrm(26, (N_ODD, SSM_GROUPS, SSM_GROUP, SSM_STATE), SSM_STATE ** -0.5),
        'ssm_d': nrm(27, (N_ODD, D_MODEL)),
        'w_glu': nrm(28, (N_ODD, D_MODEL, 2 * D_MODEL), D_MODEL ** -0.5),
        'b_glu': nrm(29, (N_ODD, 2 * D_MODEL), 0.02),
        'w_ffn_in': nrm(30, (DEPTH, D_MODEL, 2 * D_FF), D_MODEL ** -0.5),
        'w_ffn_out': nrm(31, (DEPTH, D_FF, D_MODEL), D_FF ** -0.5),
    }


def reference(x_prompt, x_sample, cache_k, cache_v, page_table, state_ssm_re, state_ssm_im,
              norm_mix, norm_ffn, norm_final, w_in_even, w_out_even, sgu_norm, sgu_w, sgu_b,
              lambda_q1, lambda_k1, lambda_q2, lambda_k2, attn_subln,
              ssm_a_re, ssm_a_im, ssm_log_dt, ssm_b_re, ssm_b_im, ssm_c_re, ssm_c_im, ssm_d,
              w_glu, b_glu, w_ffn_in, w_ffn_out):
    xp, xs = x_prompt, x_sample
    kp_rows, vp_rows, ks_rows, vs_rows, chunk_rows = [], [], [], [], []
    sp_re, sp_im, ss_re, ss_im = [], [], [], []
    for l in range(DEPTH):
        hp = rmsnorm(xp, norm_mix[l])
        hs = rmsnorm(xs, norm_mix[l])
        if l % 2 == 0:
            i = l // 2
            lam_init = 0.8 - 0.6 * math.exp(-0.3 * l)
            lam = diff_lambda(lambda_q1[i], lambda_k1[i], lambda_q2[i], lambda_k2[i], lam_init)
            up, vp, qp, kp, vbp = even_project(hp, w_in_even[i], sgu_norm[i])
            us, vs, qs, k_s, vbs = even_project(hs, w_in_even[i], sgu_norm[i])
            b, s = xp.shape[:2]
            a_p = up * spatial_gate(vp.reshape(b, s // CHUNK, CHUNK, A_WIDTH), sgu_w[i], sgu_b[i]).reshape(b, s, A_WIDTH)
            a_s = us * spatial_gate(vs[:, None], sgu_w[i], sgu_b[i])[:, 0]
            b_p = diff_head_out(diff_attn_prompt(qp, kp, vbp, lam), attn_subln[i], lam_init)
            db = xs.shape[0]
            k_past = cache_k[i][page_table].reshape(db, -1, B_HEADS, 2 * B_QK_DIM)
            v_past = cache_v[i][page_table].reshape(db, -1, B_HEADS, B_V_DIM)
            b_s = diff_head_out(diff_attn_sample(qs, k_s, vbs, k_past, v_past, lam), attn_subln[i], lam_init)
            mp = jnp.concatenate([a_p, b_p], axis=-1) @ w_out_even[i]
            ms = jnp.concatenate([a_s, b_s], axis=-1) @ w_out_even[i]
            kp_rows.append(kp)
            vp_rows.append(vbp)
            ks_rows.append(k_s)
            vs_rows.append(vbs)
            chunk_rows.append(vs)
        else:
            j = l // 2
            zeros = jnp.zeros((xp.shape[0], SSM_GROUPS, SSM_STATE), jnp.float32)
            ssm_params = (ssm_a_re[j], ssm_a_im[j], ssm_log_dt[j], ssm_b_re[j], ssm_b_im[j],
                          ssm_c_re[j], ssm_c_im[j], ssm_d[j])
            y_p, p_re, p_im = s5_scan(hp, zeros, zeros, *ssm_params)
            y_s, s_re, s_im = s5_scan(hs, state_ssm_re[j], state_ssm_im[j], *ssm_params)
            mp = s5_glu(y_p, w_glu[j], b_glu[j])
            ms = s5_glu(y_s, w_glu[j], b_glu[j])
            sp_re.append(p_re)
            sp_im.append(p_im)
            ss_re.append(s_re)
            ss_im.append(s_im)
        xp = xp + mp
        xs = xs + ms
        xp = xp + swiglu(rmsnorm(xp, norm_ffn[l]), w_ffn_in[l], w_ffn_out[l])
        xs = xs + swiglu(rmsnorm(xs, norm_ffn[l]), w_ffn_in[l], w_ffn_out[l])
    y_prompt = rmsnorm(xp, norm_final)
    y_sample = rmsnorm(xs, norm_final)
    return (y_prompt, y_sample, jnp.stack(kp_rows), jnp.stack(vp_rows), jnp.stack(ks_rows),
            jnp.stack(vs_rows), jnp.stack(chunk_rows), jnp.stack(sp_re), jnp.stack(sp_im),
            jnp.stack(ss_re), jnp.stack(ss_im))
```

```python
import functools
import math

import jax
import jax.numpy as jnp
from jax import lax
from jax.experimental import pallas as pl
from jax.experimental.pallas import tpu as pltpu

F32 = jnp.float32
BF16 = jnp.bfloat16

EPS = 1e-6
LANES = 128
SUBLANES = 8
V7X_VMEM_BYTES = 64 * 1024 * 1024
VMEM_LIMIT = V7X_VMEM_BYTES * 7 // 8
MASK_VALUE = -0.7 * float(jnp.finfo(jnp.float32).max)

CHUNK = 128
A_GROUPS = 4
HEAD_DIM = 128
QK_HALF = 64
SSM_GROUP = 16
SLAB_GROUPS = LANES // SSM_GROUP
SCAN_SHIFTS = (1, 2, 4)


def _const_spec(shape):
    zeros = (0,) * len(shape)
    return pl.BlockSpec(shape, lambda *_: zeros, pipeline_mode=pl.Buffered(1))


def _rms(x, g):
    return x * lax.rsqrt(jnp.mean(x * x, axis=-1, keepdims=True) + EPS) * g


def _params(semantics):
    return pltpu.CompilerParams(dimension_semantics=semantics, vmem_limit_bytes=VMEM_LIMIT)


def _even_in_kernel(x_ref, g_ref, w_ref, sn_ref, mix_ref, bias_ref, *outs, prompt, a_width):
    if prompt:
        a_ref, q_ref, k_ref, vb_ref, kb_ref, vbb_ref = outs
    else:
        a_ref, q_ref, k_ref, vb_ref, vn_ref = outs
    tm = x_ref.shape[0]
    rows_per_mix = mix_ref.shape[1]
    gw = a_width // A_GROUPS
    h = _rms(x_ref[...], g_ref[...]).astype(BF16)

    def proj(c0, width):
        return jnp.dot(h, w_ref[:, c0:c0 + width], preferred_element_type=F32)

    u = jax.nn.gelu(proj(0, a_width))
    v = _rms(jax.nn.gelu(proj(a_width, a_width)), sn_ref[...])
    if not prompt:
        vn_ref[...] = v
    v16 = v.astype(BF16)
    o = 2 * a_width
    bw = q_ref.shape[1]
    q_ref[...] = (proj(o, bw) * (QK_HALF ** -0.5)).astype(q_ref.dtype)
    k = proj(o + bw, bw)
    k_ref[...] = k
    vb = proj(o + 2 * bw, bw)
    vb_ref[...] = vb
    if prompt:
        kb_ref[...] = k.astype(BF16)
        vbb_ref[...] = vb.astype(BF16)
    for c in range(tm // rows_per_mix):
        r0 = c * rows_per_mix
        for g in range(A_GROUPS):
            c0 = g * gw
            gate = jnp.dot(mix_ref[g], v16[r0:r0 + rows_per_mix, c0:c0 + gw],
                           preferred_element_type=F32) + bias_ref[:, c0:c0 + gw]
            a_ref[r0:r0 + rows_per_mix, c0:c0 + gw] = (
                u[r0:r0 + rows_per_mix, c0:c0 + gw] * gate).astype(a_ref.dtype)


def _even_in(x, g, w, sn, mix, bias, *, tm, prompt):
    t, d = x.shape
    a_width = sn.shape[1]
    bw = (w.shape[1] - 2 * a_width) // 3
    act = BF16 if prompt else F32
    row = lambda width: pl.BlockSpec((tm, width), lambda i: (i, 0))
    out_shape = [jax.ShapeDtypeStruct((t, a_width), act), jax.ShapeDtypeStruct((t, bw), act),
                 jax.ShapeDtypeStruct((t, bw), F32), jax.ShapeDtypeStruct((t, bw), F32)]
    out_specs = [row(a_width), row(bw), row(bw), row(bw)]
    if prompt:
        out_shape += [jax.ShapeDtypeStruct((t, bw), BF16)] * 2
        out_specs += [row(bw), row(bw)]
    else:
        out_shape += [jax.ShapeDtypeStruct((t, a_width), F32)]
        out_specs += [row(a_width)]
    return pl.pallas_call(
        functools.partial(_even_in_kernel, prompt=prompt, a_width=a_width),
        grid=(t // tm,),
        in_specs=[row(d), _const_spec(g.shape), _const_spec(w.shape), _const_spec(sn.shape),
                  _const_spec(mix.shape), _const_spec(bias.shape)],
        out_specs=out_specs,
        out_shape=out_shape,
        compiler_params=_params(("parallel",)),
        name="even_in_prompt" if prompt else "even_in_sample",
    )(x, g, w, sn, mix, bias)


def _split_halves(q):
    lane = lax.broadcasted_iota(jnp.int32, q.shape, 1)
    zero = jnp.zeros_like(q)
    return jnp.where(lane < QK_HALF, q, zero), jnp.where(lane >= QK_HALF, q, zero)


def _softmax_step(s, vblk, m_ref, l_ref, acc_ref, rows):
    m_prev = m_ref[rows, :]
    m_next = jnp.maximum(m_prev, jnp.max(s, axis=1, keepdims=True))
    p = jnp.exp(s - jnp.concatenate([m_next] * (s.shape[1] // LANES), axis=1))
    alpha = jnp.exp(m_prev - m_next)
    l_ref[rows, :] = alpha * l_ref[rows, :] + jnp.sum(p, axis=1, keepdims=True)
    acc_ref[rows, :] = alpha * acc_ref[rows, :] + jnp.dot(
        p.astype(BF16), vblk, preferred_element_type=F32)
    m_ref[rows, :] = m_next


def _diff_lambda(lq1, lk1, lq2, lk2, lam_init):
    e1 = jnp.exp(jnp.sum(lq1[...] * lk1[...], axis=1, keepdims=True))
    e2 = jnp.exp(jnp.sum(lq2[...] * lk2[...], axis=1, keepdims=True))
    return e1 - e2 + lam_init


def _head_out(o1, o2, lam, sub, lam_init):
    o = o1 - lam * o2
    return _rms(o, sub) * (1.0 - lam_init)


def _attn_prompt_kernel(q_ref, k_ref, v_ref, lq1, lk1, lq2, lk2, sub_ref, o_ref,
                        qs_ref, m_ref, l_ref, acc_ref, *, lam_init):
    tq = q_ref.shape[0]
    i = pl.program_id(2)
    q1, q2 = _split_halves(q_ref[...])
    qs_ref[0:tq, :] = q1
    qs_ref[tq:2 * tq, :] = q2
    m_ref[...] = jnp.full(m_ref.shape, MASK_VALUE, F32)
    l_ref[...] = jnp.zeros(l_ref.shape, F32)
    acc_ref[...] = jnp.zeros(acc_ref.shape, F32)
    all_rows = slice(0, 2 * tq)

    def scores(j):
        kv_rows = pl.ds(pl.multiple_of(j * tq, tq), tq)
        s = lax.dot_general(qs_ref[...], k_ref[kv_rows, :], (((1,), (1,)), ((), ())),
                            preferred_element_type=F32)
        return s, v_ref[kv_rows, :]

    def full_block(j, carry):
        s, vblk = scores(j)
        _softmax_step(s, vblk, m_ref, l_ref, acc_ref, all_rows)
        return carry

    lax.fori_loop(0, i, full_block, 0)
    s, vblk = scores(i)
    row = lax.broadcasted_iota(jnp.int32, s.shape, 0)
    col = lax.broadcasted_iota(jnp.int32, s.shape, 1)
    qpos = jnp.where(row >= tq, row - tq, row)
    s = jnp.where(qpos >= col, s, MASK_VALUE)
    _softmax_step(s, vblk, m_ref, l_ref, acc_ref, all_rows)

    lam = _diff_lambda(lq1, lk1, lq2, lk2, lam_init)
    o1 = acc_ref[0:tq, :] / l_ref[0:tq, :]
    o2 = acc_ref[tq:2 * tq, :] / l_ref[tq:2 * tq, :]
    o_ref[...] = _head_out(o1, o2, lam, sub_ref[...], lam_init).astype(o_ref.dtype)


def _attn_prompt(q, kb, vb, lq1, lk1, lq2, lk2, sub, *, batch, seq, tq, lam_init):
    t, bw = q.shape
    heads = bw // HEAD_DIM
    nq = seq // tq
    vec = lambda a: _const_spec(a.shape)
    return pl.pallas_call(
        functools.partial(_attn_prompt_kernel, lam_init=lam_init),
        grid=(batch, heads, nq),
        in_specs=[pl.BlockSpec((tq, HEAD_DIM), lambda b, h, i: (b * nq + i, h)),
                  pl.BlockSpec((seq, HEAD_DIM), lambda b, h, i: (b, h)),
                  pl.BlockSpec((seq, HEAD_DIM), lambda b, h, i: (b, h)),
                  vec(lq1), vec(lk1), vec(lq2), vec(lk2), vec(sub)],
        out_specs=pl.BlockSpec((tq, HEAD_DIM), lambda b, h, i: (b * nq + i, h)),
        out_shape=jax.ShapeDtypeStruct((t, bw), BF16),
        scratch_shapes=[pltpu.VMEM((2 * tq, HEAD_DIM), BF16),
                        pltpu.VMEM((2 * tq, LANES), F32),
                        pltpu.VMEM((2 * tq, LANES), F32),
                        pltpu.VMEM((2 * tq, HEAD_DIM), F32)],
        compiler_params=_params(("parallel", "parallel", "arbitrary")),
        name="attn_prompt",
    )(q, kb, vb, lq1, lk1, lq2, lk2, sub)


def _attn_sample_kernel(pt_ref, q_ref, kn_ref, vn_ref, *rest, pages_per_step, heads, lam_init):
    del pt_ref
    k_pages = rest[:pages_per_step]
    v_pages = rest[pages_per_step:2 * pages_per_step]
    lq1, lk1, lq2, lk2, sub_ref, o_ref, qs_ref, m_ref, l_ref, acc_ref = rest[2 * pages_per_step:]
    j = pl.program_id(1)
    t_new = q_ref.shape[0]
    rows_per_head = 2 * t_new

    @pl.when(j == 0)
    def _():
        for h in range(heads):
            q1, q2 = _split_halves(q_ref[:, h * HEAD_DIM:(h + 1) * HEAD_DIM])
            qs_ref[h * rows_per_head:(h + 1) * rows_per_head, :] = (
                jnp.concatenate([q1, q2], axis=0).astype(BF16))
        m_ref[...] = jnp.full(m_ref.shape, MASK_VALUE, F32)
        l_ref[...] = jnp.zeros(l_ref.shape, F32)
        acc_ref[...] = jnp.zeros(acc_ref.shape, F32)

    def head_scores(h, kblk):
        rows = slice(h * rows_per_head, (h + 1) * rows_per_head)
        s = lax.dot_general(qs_ref[rows, :], kblk.astype(BF16), (((1,), (1,)), ((), ())),
                            preferred_element_type=F32)
        return rows, s

    for h in range(heads):
        kblk = jnp.concatenate([k_pages[p][:, h, :] for p in range(pages_per_step)], axis=0)
        vblk = jnp.concatenate([v_pages[p][:, h, :] for p in range(pages_per_step)], axis=0)
        rows, s = head_scores(h, kblk)
        _softmax_step(s, vblk.astype(BF16), m_ref, l_ref, acc_ref, rows)

    @pl.when(j == pl.num_programs(1) - 1)
    def _():
        pad = jnp.zeros((LANES - t_new, HEAD_DIM), F32)
        lam = _diff_lambda(lq1, lk1, lq2, lk2, lam_init)
        for h in range(heads):
            cols = slice(h * HEAD_DIM, (h + 1) * HEAD_DIM)
            kblk = jnp.concatenate([kn_ref[:, cols], pad], axis=0)
            vblk = jnp.concatenate([vn_ref[:, cols], pad], axis=0)
            rows, s = head_scores(h, kblk)
            row = lax.broadcasted_iota(jnp.int32, s.shape, 0)
            col = lax.broadcasted_iota(jnp.int32, s.shape, 1)
            qpos = jnp.where(row >= t_new, row - t_new, row)
            s = jnp.where(qpos >= col, s, MASK_VALUE)
            _softmax_step(s, vblk.astype(BF16), m_ref, l_ref, acc_ref, rows)
            r0 = h * rows_per_head
            o1 = acc_ref[r0:r0 + t_new, :] / l_ref[r0:r0 + t_new, :]
            o2 = (acc_ref[r0 + t_new:r0 + rows_per_head, :]
                  / l_ref[r0 + t_new:r0 + rows_per_head, :])
            o_ref[:, cols] = _head_out(o1, o2, lam, sub_ref[...], lam_init)


def _attn_sample(page_table, q, k_new, v_new, cache_k, cache_v, layer, lq1, lk1, lq2, lk2, sub,
                 *, t_new, pages_per_step, lam_init):
    t, bw = q.shape
    heads = bw // HEAD_DIM
    dec_batch, n_pages = page_table.shape
    page_size = cache_k.shape[2]
    steps = n_pages // pages_per_step
    pt_flat = page_table.reshape(-1)

    def page_spec(p):
        return pl.BlockSpec(
            (None, None, page_size, heads, HEAD_DIM),
            lambda b, j, pt: (layer, pt[b * n_pages + j * pages_per_step + p], 0, 0, 0))

    new_spec = pl.BlockSpec((t_new, bw), lambda b, j, pt: (b, 0))
    vec = lambda a: pl.BlockSpec(a.shape, lambda b, j, pt: (0, 0))
    grid_spec = pltpu.PrefetchScalarGridSpec(
        num_scalar_prefetch=1,
        grid=(dec_batch, steps),
        in_specs=([new_spec, new_spec, new_spec]
                  + [page_spec(p) for p in range(pages_per_step)] * 2
                  + [vec(lq1), vec(lk1), vec(lq2), vec(lk2), vec(sub)]),
        out_specs=new_spec,
        scratch_shapes=[pltpu.VMEM((heads * 2 * t_new, HEAD_DIM), BF16),
                        pltpu.VMEM((heads * 2 * t_new, LANES), F32),
                        pltpu.VMEM((heads * 2 * t_new, LANES), F32),
                        pltpu.VMEM((heads * 2 * t_new, HEAD_DIM), F32)])
    return pl.pallas_call(
        functools.partial(_attn_sample_kernel, pages_per_step=pages_per_step, heads=heads,
                          lam_init=lam_init),
        grid_spec=grid_spec,
        out_shape=jax.ShapeDtypeStruct((t, bw), F32),
        compiler_params=_params(("parallel", "arbitrary")),
        name="attn_sample",
    )(pt_flat, q, k_new, v_new, *([cache_k] * pages_per_step), *([cache_v] * pages_per_step),
      lq1, lk1, lq2, lk2, sub)


def _swiglu_residual(x1, g_ref, w_in_ref, w_out_ref, ff_chunk):
    d_ff = w_out_ref.shape[0]
    h = _rms(x1, g_ref[...]).astype(BF16)
    acc = x1
    for c0 in range(0, d_ff, ff_chunk):
        gate = jnp.dot(h, w_in_ref[:, c0:c0 + ff_chunk], preferred_element_type=F32)
        up = jnp.dot(h, w_in_ref[:, d_ff + c0:d_ff + c0 + ff_chunk], preferred_element_type=F32)
        act = (gate * jax.nn.sigmoid(gate) * up).astype(BF16)
        acc = acc + jnp.dot(act, w_out_ref[c0:c0 + ff_chunk, :], preferred_element_type=F32)
    return acc


def _mix_ffn_kernel(x_ref, a_ref, b_ref, wo_ref, g_ref, w_in_ref, w_out_ref, o_ref, *, ff_chunk):
    aw = a_ref.shape[1]
    o_ref[...] = (x_ref[...]
                  + jnp.dot(a_ref[...].astype(BF16), wo_ref[0:aw, :], preferred_element_type=F32)
                  + jnp.dot(b_ref[...].astype(BF16), wo_ref[aw:, :], preferred_element_type=F32))
    o_ref[...] = _swiglu_residual(o_ref[...], g_ref, w_in_ref, w_out_ref, ff_chunk)


def _ffn_final_kernel(x_ref, g_ref, w_in_ref, w_out_ref, gf_ref, o_ref, *, ff_chunk):
    x2 = _swiglu_residual(x_ref[...], g_ref, w_in_ref, w_out_ref, ff_chunk)
    o_ref[...] = _rms(x2, gf_ref[...])


def _mix_ffn(x, a, b, wo, g, w_in, w_out, *, tm, ff_chunk, name):
    t, d = x.shape
    row = lambda arr: pl.BlockSpec((tm, arr.shape[1]), lambda i: (i, 0))
    return pl.pallas_call(
        functools.partial(_mix_ffn_kernel, ff_chunk=ff_chunk),
        grid=(t // tm,),
        in_specs=[row(x), row(a), row(b), _const_spec(wo.shape), _const_spec(g.shape),
                  _const_spec(w_in.shape), _const_spec(w_out.shape)],
        out_specs=row(x),
        out_shape=jax.ShapeDtypeStruct((t, d), F32),
        compiler_params=_params(("parallel",)),
        name=name,
    )(x, a, b, wo, g, w_in, w_out)


def _ffn_final(x, g, w_in, w_out, gf, *, tm, ff_chunk, name):
    t, d = x.shape
    row = pl.BlockSpec((tm, d), lambda i: (i, 0))
    return pl.pallas_call(
        functools.partial(_ffn_final_kernel, ff_chunk=ff_chunk),
        grid=(t // tm,),
        in_specs=[row, _const_spec(g.shape), _const_spec(w_in.shape), _const_spec(w_out.shape),
                  _const_spec(gf.shape)],
        out_specs=row,
        out_shape=jax.ShapeDtypeStruct((t, d), F32),
        compiler_params=_params(("parallel",)),
        name=name,
    )(x, g, w_in, w_out, gf)


def _s5_kernel(x_ref, g_ref, s0re_ref, s0im_ref, coef_ref, bblk_ref, cblk_ref, d_ref,
               wglu_ref, bglu_ref, o_ref, sre_ref, sim_ref,
               re_s, im_s, gel_s, car_re, car_im):
    tl, d = x_ref.shape
    n_slabs = d // LANES
    slab_state = re_s.shape[1] // n_slabs

    @pl.when(pl.program_id(1) == 0)
    def _():
        car_re[...] = s0re_ref[...]
        car_im[...] = s0im_ref[...]

    x = x_ref[...]
    h = _rms(x, g_ref[...])
    hb = h.astype(BF16)
    for j in range(n_slabs):
        bu = jnp.dot(hb[:, j * LANES:(j + 1) * LANES], bblk_ref[j], preferred_element_type=F32)
        re_s[:, j * slab_state:(j + 1) * slab_state] = bu[:, :slab_state]
        im_s[:, j * slab_state:(j + 1) * slab_state] = bu[:, slab_state:]

    def row_tile(r, carry):
        cr, ci = carry
        rows = pl.ds(pl.multiple_of(r * SUBLANES, SUBLANES), SUBLANES)
        wr = re_s[rows, :]
        wi = im_s[rows, :]
        for k, shift in enumerate(SCAN_SHIFTS):
            ar = coef_ref[2 * k]
            ai = coef_ref[2 * k + 1]
            sr = pltpu.roll(wr, shift, 0)
            si = pltpu.roll(wi, shift, 0)
            wr, wi = wr + ar * sr - ai * si, wi + ar * si + ai * sr
        pr = coef_ref[2 * len(SCAN_SHIFTS)]
        pi = coef_ref[2 * len(SCAN_SHIFTS) + 1]
        wr, wi = wr + pr * cr - pi * ci, wi + pr * ci + pi * cr
        re_s[rows, :] = wr
        im_s[rows, :] = wi
        return wr[SUBLANES - 1:SUBLANES, :], wi[SUBLANES - 1:SUBLANES, :]

    cr, ci = lax.fori_loop(0, tl // SUBLANES, row_tile, (car_re[...], car_im[...]))
    car_re[...] = cr
    car_im[...] = ci
    sre_ref[...] = cr
    sim_ref[...] = ci

    for j in range(n_slabs):
        cols = slice(j * LANES, (j + 1) * LANES)
        st = slice(j * slab_state, (j + 1) * slab_state)
        s_cat = jnp.concatenate([re_s[:, st], im_s[:, st]], axis=1).astype(BF16)
        y = jnp.dot(s_cat, cblk_ref[j], preferred_element_type=F32) + d_ref[:, cols] * h[:, cols]
        gel_s[:, cols] = jax.nn.gelu(y).astype(BF16)
    z = jnp.dot(gel_s[...], wglu_ref[...], preferred_element_type=F32) + bglu_ref[...]
    o_ref[...] = x + z[:, :d] * jax.nn.sigmoid(z[:, d:])


def _s5_mix(x, g, s0re, s0im, coef, bblk, cblk, dskip, wglu, bglu, *, batch, seq, tl, name):
    t, d = x.shape
    nt = seq // tl
    n_state = s0re.shape[-1]
    state_spec = pl.BlockSpec((None, 1, n_state), lambda b, i: (b, 0, 0))
    x1, sre, sim = pl.pallas_call(
        _s5_kernel,
        grid=(batch, nt),
        in_specs=[pl.BlockSpec((tl, d), lambda b, i: (b * nt + i, 0)), _const_spec(g.shape),
                  state_spec, state_spec, _const_spec(coef.shape), _const_spec(bblk.shape),
                  _const_spec(cblk.shape), _const_spec(dskip.shape), _const_spec(wglu.shape),
                  _const_spec(bglu.shape)],
        out_specs=[pl.BlockSpec((tl, d), lambda b, i: (b * nt + i, 0)), state_spec, state_spec],
        out_shape=[jax.ShapeDtypeStruct((t, d), F32),
                   jax.ShapeDtypeStruct((batch, 1, n_state), F32),
                   jax.ShapeDtypeStruct((batch, 1, n_state), F32)],
        scratch_shapes=[pltpu.VMEM((tl, n_state), F32), pltpu.VMEM((tl, n_state), F32),
                        pltpu.VMEM((tl, d), BF16),
                        pltpu.VMEM((1, n_state), F32), pltpu.VMEM((1, n_state), F32)],
        compiler_params=_params(("parallel", "arbitrary")),
        name=name,
    )(x, g, s0re, s0im, coef, bblk, cblk, dskip, wglu, bglu)
    return x1, sre, sim


def _s5_tables(a_re, a_im, log_dt, b_re, b_im, c_re, c_im):
    groups, n_p = a_re.shape
    a = lax.complex(a_re, a_im)
    lam_dt = a * jnp.exp(log_dt)[:, None]
    a_bar = jnp.exp(lam_dt)
    b_bar = ((a_bar - 1.0) / a)[:, :, None] * lax.complex(b_re, b_im)
    rows = jnp.arange(SUBLANES)
    coef = []
    for shift in SCAN_SHIFTS:
        p = jnp.exp(lam_dt * shift).reshape(1, -1) * (rows >= shift)[:, None]
        coef += [jnp.real(p), jnp.imag(p)]
    p = jnp.exp(lam_dt.reshape(1, -1) * (rows + 1)[:, None].astype(F32))
    coef += [jnp.real(p), jnp.imag(p)]
    coef = jnp.stack(coef).astype(F32)
    n_slabs = groups // SLAB_GROUPS
    eye = jnp.eye(SLAB_GROUPS, dtype=F32)

    def in_block(m):
        m = m.reshape(n_slabs, SLAB_GROUPS, n_p, SSM_GROUP)
        return jnp.einsum('jgpc,gh->jgchp', m, eye).reshape(
            n_slabs, SLAB_GROUPS * SSM_GROUP, SLAB_GROUPS * n_p)

    def out_block(m):
        m = m.reshape(n_slabs, SLAB_GROUPS, SSM_GROUP, n_p)
        return jnp.einsum('jgcp,gh->jgphc', m, eye).reshape(
            n_slabs, SLAB_GROUPS * n_p, SLAB_GROUPS * SSM_GROUP)

    bblk = jnp.concatenate([in_block(jnp.real(b_bar)), in_block(jnp.imag(b_bar))], axis=2)
    cblk = jnp.concatenate([out_block(c_re), -out_block(c_im)], axis=1)
    return coef, bblk.astype(BF16), cblk.astype(BF16)


def kernel(x_prompt, x_sample, cache_k, cache_v, page_table, state_ssm_re, state_ssm_im,
           norm_mix, norm_ffn, norm_final, w_in_even, w_out_even, sgu_norm, sgu_w, sgu_b,
           lambda_q1, lambda_k1, lambda_q2, lambda_k2, attn_subln,
           ssm_a_re, ssm_a_im, ssm_log_dt, ssm_b_re, ssm_b_im, ssm_c_re, ssm_c_im, ssm_d,
           w_glu, b_glu, w_ffn_in, w_ffn_out):
    batch, seq, d = x_prompt.shape
    dec_batch, t_new, _ = x_sample.shape
    depth = norm_mix.shape[0]
    assert depth == 2 and seq % CHUNK == 0 and t_new == SUBLANES
    tp = batch * seq
    ts = dec_batch * t_new
    xp = x_prompt.reshape(tp, d)
    xs = x_sample.reshape(ts, d)
    tm_prompt = 512
    ff_chunk = 256
    row = lambda v: v.reshape(1, -1)

    lam_init = 0.8 - 0.6 * math.exp(-0.3 * 0)
    w_in0 = w_in_even[0].astype(BF16)
    w_out0 = w_out_even[0].astype(BF16)
    a_width = sgu_norm.shape[1]
    gw = a_width // A_GROUPS
    tril = jnp.tril(jnp.ones((CHUNK, CHUNK), bool))
    mix_p = jnp.where(tril, sgu_w[0], 0).astype(BF16)
    bias_p = jnp.repeat(sgu_b[0].T, gw, axis=1)
    small = jnp.where(tril[:t_new, :t_new], sgu_w[0][:, :t_new, :t_new], 0)
    mix_s = jnp.einsum('gts,bc->gbtcs', small, jnp.eye(dec_batch, dtype=F32)).reshape(
        A_GROUPS, ts, ts).astype(BF16)
    bias_s = jnp.tile(bias_p[:t_new], (dec_batch, 1))
    lam_vecs = [row(lambda_q1[0]), row(lambda_k1[0]), row(lambda_q2[0]), row(lambda_k2[0]),
                row(attn_subln[0])]

    a_p, q_p, k_p, vb_p, kb_p, vbb_p = _even_in(
        xp, row(norm_mix[0]), w_in0, row(sgu_norm[0]), mix_p, bias_p, tm=tm_prompt, prompt=True)
    a_s, q_s, k_s, vb_s, vn_s = _even_in(
        xs, row(norm_mix[0]), w_in0, row(sgu_norm[0]), mix_s, bias_s, tm=ts, prompt=False)
    b_p = _attn_prompt(q_p, kb_p, vbb_p, *lam_vecs, batch=batch, seq=seq, tq=512,
                       lam_init=lam_init)
    b_s = _attn_sample(page_table, q_s, k_s, vb_s, cache_k, cache_v, 0, *lam_vecs,
                       t_new=t_new, pages_per_step=8, lam_init=lam_init)
    w_ffn_in0 = w_ffn_in[0].astype(BF16)
    w_ffn_out0 = w_ffn_out[0].astype(BF16)
    xp = _mix_ffn(xp, a_p, b_p, w_out0, row(norm_ffn[0]), w_ffn_in0, w_ffn_out0,
                  tm=tm_prompt, ff_chunk=ff_chunk, name="mix_ffn_prompt")
    xs = _mix_ffn(xs, a_s, b_s, w_out0, row(norm_ffn[0]), w_ffn_in0, w_ffn_out0,
                  tm=ts, ff_chunk=ff_chunk, name="mix_ffn_sample")

    groups, n_p = ssm_a_re.shape[1:]
    n_state = groups * n_p
    coef, bblk, cblk = _s5_tables(ssm_a_re[0], ssm_a_im[0], ssm_log_dt[0], ssm_b_re[0],
                                  ssm_b_im[0], ssm_c_re[0], ssm_c_im[0])
    w_glu0 = w_glu[0].astype(BF16)
    s5_args = (coef, bblk, cblk, row(ssm_d[0]), w_glu0, row(b_glu[0]))
    zeros = jnp.zeros((batch, 1, n_state), F32)
    xp, p_re, p_im = _s5_mix(xp, row(norm_mix[1]), zeros, zeros, *s5_args,
                             batch=batch, seq=seq, tl=256, name="s5_prompt")
    xs, s_re, s_im = _s5_mix(xs, row(norm_mix[1]), state_ssm_re[0].reshape(dec_batch, 1, n_state),
                             state_ssm_im[0].reshape(dec_batch, 1, n_state), *s5_args,
                             batch=dec_batch, seq=t_new, tl=t_new, name="s5_sample")
    w_ffn_in1 = w_ffn_in[1].astype(BF16)
    w_ffn_out1 = w_ffn_out[1].astype(BF16)
    yp = _ffn_final(xp, row(norm_ffn[1]), w_ffn_in1, w_ffn_out1, row(norm_final),
                    tm=tm_prompt, ff_chunk=ff_chunk, name="ffn_final_prompt")
    ys = _ffn_final(xs, row(norm_ffn[1]), w_ffn_in1, w_ffn_out1, row(norm_final),
                    tm=ts, ff_chunk=ff_chunk, name="ffn_final_sample")

    heads = k_p.shape[1] // HEAD_DIM
    st = lambda s, n: s.reshape(1, n, groups, n_p)
    return (yp.reshape(batch, seq, d), ys.reshape(dec_batch, t_new, d),
            k_p.reshape(1, batch, seq, heads, HEAD_DIM), vb_p.reshape(1, batch, seq, heads, HEAD_DIM),
            k_s.reshape(1, dec_batch, t_new, heads, HEAD_DIM),
            vb_s.reshape(1, dec_batch, t_new, heads, HEAD_DIM),
            vn_s.reshape(1, dec_batch, t_new, a_width),
            st(p_re, batch), st(p_im, batch), st(s_re, dec_batch), st(s_im, dec_batch))
```

```python
import functools
import math

import jax
import jax.numpy as jnp
from jax import lax
from jax.experimental import pallas as pl
from jax.experimental.pallas import tpu as pltpu

F32 = jnp.float32
BF16 = jnp.bfloat16

EPS = 1e-6
LANES = 128
SUBLANES = 8
V7X_VMEM_BYTES = 64 * 1024 * 1024
VMEM_LIMIT = V7X_VMEM_BYTES * 7 // 8
MASK_VALUE = -0.7 * float(jnp.finfo(jnp.float32).max)

CHUNK = 128
A_GROUPS = 4
HEAD_DIM = 128
QK_HALF = 64
LOG2E = math.log2(math.e)
SSM_GROUP = 16
SLAB_GROUPS = LANES // SSM_GROUP
SCAN_SHIFTS = (1, 2, 4)


def _const_spec(shape):
    zeros = (0,) * len(shape)
    return pl.BlockSpec(shape, lambda *_: zeros, pipeline_mode=pl.Buffered(1))


def _rms(x, g):
    return x * lax.rsqrt(jnp.mean(x * x, axis=-1, keepdims=True) + EPS) * g


def _params(semantics):
    return pltpu.CompilerParams(dimension_semantics=semantics, vmem_limit_bytes=VMEM_LIMIT)


def _even_in_kernel(x_ref, g_ref, w_ref, sn_ref, mix_ref, bias_ref, *outs, prompt, a_width):
    if prompt:
        a_ref, q_ref, k_ref, vb_ref, kb_ref, vbb_ref = outs
    else:
        a_ref, q_ref, k_ref, vb_ref, vn_ref = outs
    tm = x_ref.shape[0]
    rows_per_mix = mix_ref.shape[1]
    gw = a_width // A_GROUPS
    h = _rms(x_ref[...], g_ref[...]).astype(BF16)

    def proj(c0, width):
        return jnp.dot(h, w_ref[:, c0:c0 + width], preferred_element_type=F32)

    u = jax.nn.gelu(proj(0, a_width))
    v = _rms(jax.nn.gelu(proj(a_width, a_width)), sn_ref[...])
    if not prompt:
        vn_ref[...] = v
    v16 = v.astype(BF16)
    o = 2 * a_width
    bw = q_ref.shape[1]
    q_ref[...] = (proj(o, bw) * (QK_HALF ** -0.5 * LOG2E)).astype(q_ref.dtype)
    k = proj(o + bw, bw)
    vb = proj(o + 2 * bw, bw)
    heads = bw // HEAD_DIM
    for hd in range(heads):
        k_ref[pl.ds(hd, tm, stride=heads), :] = k[:, hd * HEAD_DIM:(hd + 1) * HEAD_DIM]
        vb_ref[pl.ds(hd, tm, stride=heads), :] = vb[:, hd * HEAD_DIM:(hd + 1) * HEAD_DIM]
    if prompt:
        kb_ref[...] = k.astype(BF16)
        vbb_ref[...] = vb.astype(BF16)
    for c in range(tm // rows_per_mix):
        r0 = c * rows_per_mix
        for g in range(A_GROUPS):
            c0 = g * gw
            gate = jnp.dot(mix_ref[g], v16[r0:r0 + rows_per_mix, c0:c0 + gw],
                           preferred_element_type=F32) + bias_ref[:, c0:c0 + gw]
            a_ref[r0:r0 + rows_per_mix, c0:c0 + gw] = (
                u[r0:r0 + rows_per_mix, c0:c0 + gw] * gate).astype(a_ref.dtype)


def _even_in(x, g, w, sn, mix, bias, *, tm, prompt):
    t, d = x.shape
    a_width = sn.shape[1]
    bw = (w.shape[1] - 2 * a_width) // 3
    act = BF16 if prompt else F32
    row = lambda width: pl.BlockSpec((tm, width), lambda i: (i, 0))
    heads = bw // HEAD_DIM
    head_rows = pl.BlockSpec((tm * heads, HEAD_DIM), lambda i: (i, 0))
    out_shape = [jax.ShapeDtypeStruct((t, a_width), act), jax.ShapeDtypeStruct((t, bw), act),
                 jax.ShapeDtypeStruct((t * heads, HEAD_DIM), F32),
                 jax.ShapeDtypeStruct((t * heads, HEAD_DIM), F32)]
    out_specs = [row(a_width), row(bw), head_rows, head_rows]
    if prompt:
        out_shape += [jax.ShapeDtypeStruct((t, bw), BF16)] * 2
        out_specs += [row(bw), row(bw)]
    else:
        out_shape += [jax.ShapeDtypeStruct((t, a_width), F32)]
        out_specs += [row(a_width)]
    return pl.pallas_call(
        functools.partial(_even_in_kernel, prompt=prompt, a_width=a_width),
        grid=(t // tm,),
        in_specs=[row(d), _const_spec(g.shape), _const_spec(w.shape), _const_spec(sn.shape),
                  _const_spec(mix.shape), _const_spec(bias.shape)],
        out_specs=out_specs,
        out_shape=out_shape,
        compiler_params=_params(("parallel",)),
        name="even_in_prompt" if prompt else "even_in_sample",
    )(x, g, w, sn, mix, bias)


def _split_halves(q):
    lane = lax.broadcasted_iota(jnp.int32, q.shape, 1)
    zero = jnp.zeros_like(q)
    return jnp.where(lane < QK_HALF, q, zero), jnp.where(lane >= QK_HALF, q, zero)


def _softmax_step(s, vblk, m_ref, l_ref, acc_ref, rows):
    tiles = [s[:, t * LANES:(t + 1) * LANES] for t in range(s.shape[1] // LANES)]
    m_prev = m_ref[rows, :]
    m_next = jnp.maximum(
        m_prev, jnp.max(functools.reduce(jnp.maximum, tiles), axis=1, keepdims=True))
    p_tiles = [jnp.exp2(t - m_next) for t in tiles]
    alpha = jnp.exp2(m_prev - m_next)
    l_ref[rows, :] = alpha * l_ref[rows, :] + functools.reduce(jnp.add, p_tiles)
    p = jnp.concatenate([t.astype(BF16) for t in p_tiles], axis=1)
    acc_ref[rows, :] = alpha * acc_ref[rows, :] + jnp.dot(p, vblk, preferred_element_type=F32)
    m_ref[rows, :] = m_next


def _row_sum(l):
    return jnp.sum(l, axis=1, keepdims=True)


def _diff_lambda(lq1, lk1, lq2, lk2, lam_init):
    e1 = jnp.exp(jnp.sum(lq1[...] * lk1[...], axis=1, keepdims=True))
    e2 = jnp.exp(jnp.sum(lq2[...] * lk2[...], axis=1, keepdims=True))
    return e1 - e2 + lam_init


def _head_out(o1, o2, lam, sub, lam_init):
    o = o1 - lam * o2
    return _rms(o, sub) * (1.0 - lam_init)


def _attn_prompt_kernel(q_ref, k_ref, v_ref, lq1, lk1, lq2, lk2, sub_ref, o_ref,
                        qs_ref, m_ref, l_ref, acc_ref, *, lam_init):
    tq = q_ref.shape[0]
    i = pl.program_id(2)
    q1, q2 = _split_halves(q_ref[...])
    qs_ref[0:tq, :] = q1
    qs_ref[tq:2 * tq, :] = q2
    m_ref[...] = jnp.full(m_ref.shape, MASK_VALUE, F32)
    l_ref[...] = jnp.zeros(l_ref.shape, F32)
    acc_ref[...] = jnp.zeros(acc_ref.shape, F32)
    all_rows = slice(0, 2 * tq)

    def scores(j):
        kv_rows = pl.ds(pl.multiple_of(j * tq, tq), tq)
        s = lax.dot_general(qs_ref[...], k_ref[kv_rows, :], (((1,), (1,)), ((), ())),
                            preferred_element_type=F32)
        return s, v_ref[kv_rows, :]

    def full_block(j, carry):
        s, vblk = scores(j)
        _softmax_step(s, vblk, m_ref, l_ref, acc_ref, all_rows)
        return carry

    lax.fori_loop(0, i, full_block, 0)
    s, vblk = scores(i)
    row = lax.broadcasted_iota(jnp.int32, s.shape, 0)
    col = lax.broadcasted_iota(jnp.int32, s.shape, 1)
    qpos = jnp.where(row >= tq, row - tq, row)
    s = jnp.where(qpos >= col, s, MASK_VALUE)
    _softmax_step(s, vblk, m_ref, l_ref, acc_ref, all_rows)

    lam = _diff_lambda(lq1, lk1, lq2, lk2, lam_init)
    o1 = acc_ref[0:tq, :] / _row_sum(l_ref[0:tq, :])
    o2 = acc_ref[tq:2 * tq, :] / _row_sum(l_ref[tq:2 * tq, :])
    o_ref[...] = _head_out(o1, o2, lam, sub_ref[...], lam_init).astype(o_ref.dtype)


def _attn_prompt(q, kb, vb, lq1, lk1, lq2, lk2, sub, *, batch, seq, tq, lam_init):
    t, bw = q.shape
    heads = bw // HEAD_DIM
    nq = seq // tq
    vec = lambda a: _const_spec(a.shape)
    return pl.pallas_call(
        functools.partial(_attn_prompt_kernel, lam_init=lam_init),
        grid=(batch, heads, nq),
        in_specs=[pl.BlockSpec((tq, HEAD_DIM), lambda b, h, i: (b * nq + i, h)),
                  pl.BlockSpec((seq, HEAD_DIM), lambda b, h, i: (b, h)),
                  pl.BlockSpec((seq, HEAD_DIM), lambda b, h, i: (b, h)),
                  vec(lq1), vec(lk1), vec(lq2), vec(lk2), vec(sub)],
        out_specs=pl.BlockSpec((tq, HEAD_DIM), lambda b, h, i: (b * nq + i, h)),
        out_shape=jax.ShapeDtypeStruct((t, bw), BF16),
        scratch_shapes=[pltpu.VMEM((2 * tq, HEAD_DIM), BF16),
                        pltpu.VMEM((2 * tq, LANES), F32),
                        pltpu.VMEM((2 * tq, LANES), F32),
                        pltpu.VMEM((2 * tq, HEAD_DIM), F32)],
        compiler_params=_params(("parallel", "parallel", "arbitrary")),
        name="attn_prompt",
    )(q, kb, vb, lq1, lk1, lq2, lk2, sub)


def _attn_sample_kernel(pt_ref, q_ref, kn_ref, vn_ref, *rest, pages_per_step, heads, lam_init):
    del pt_ref
    k_pages = rest[:pages_per_step]
    v_pages = rest[pages_per_step:2 * pages_per_step]
    (lq1, lk1, lq2, lk2, sub_ref, o_ref,
     qs_ref, bias_ref, m_ref, l_ref, acc_ref) = rest[2 * pages_per_step:]
    j = pl.program_id(1)
    t_new = q_ref.shape[0]
    rows_per_head = 2 * t_new
    all_rows = slice(0, heads * rows_per_head)

    def head_match(shape):
        row = lax.broadcasted_iota(jnp.int32, shape, 0)
        col = lax.broadcasted_iota(jnp.int32, shape, 1)
        return row, col, lax.rem(col, heads) == row // rows_per_head

    @pl.when(j == 0)
    def _():
        pieces = []
        for h in range(heads):
            pieces += _split_halves(q_ref[:, h * HEAD_DIM:(h + 1) * HEAD_DIM])
        qs_ref[...] = jnp.concatenate(pieces, axis=0).astype(BF16)
        _, _, ok = head_match(bias_ref.shape)
        bias_ref[...] = jnp.where(ok, 0.0, MASK_VALUE)
        m_ref[...] = jnp.full(m_ref.shape, MASK_VALUE, F32)
        l_ref[...] = jnp.zeros(l_ref.shape, F32)
        acc_ref[...] = jnp.zeros(acc_ref.shape, F32)

    def scores(kblk):
        return lax.dot_general(qs_ref[...], kblk.astype(BF16), (((1,), (1,)), ((), ())),
                               preferred_element_type=F32)

    kblk = jnp.concatenate([k_pages[p][...] for p in range(pages_per_step)], axis=0)
    vblk = jnp.concatenate([v_pages[p][...] for p in range(pages_per_step)], axis=0)
    _softmax_step(scores(kblk) + bias_ref[...], vblk.astype(BF16), m_ref, l_ref, acc_ref, all_rows)

    @pl.when(j == pl.num_programs(1) - 1)
    def _():
        pad = jnp.zeros((LANES - kn_ref.shape[0], HEAD_DIM), F32)
        s = scores(jnp.concatenate([kn_ref[...], pad], axis=0))
        row, col, ok = head_match(s.shape)
        s = jnp.where(ok, s, MASK_VALUE)
        s = jnp.where(col // heads <= lax.rem(row, t_new), s, MASK_VALUE)
        vblk = jnp.concatenate([vn_ref[...], pad], axis=0)
        _softmax_step(s, vblk.astype(BF16), m_ref, l_ref, acc_ref, all_rows)
        lam = _diff_lambda(lq1, lk1, lq2, lk2, lam_init)
        for h in range(heads):
            r0 = h * rows_per_head
            o1 = acc_ref[r0:r0 + t_new, :] / _row_sum(l_ref[r0:r0 + t_new, :])
            o2 = (acc_ref[r0 + t_new:r0 + rows_per_head, :]
                  / _row_sum(l_ref[r0 + t_new:r0 + rows_per_head, :]))
            o_ref[:, h * HEAD_DIM:(h + 1) * HEAD_DIM] = _head_out(
                o1, o2, lam, sub_ref[...], lam_init)


def _attn_sample(page_table, q, k_new, v_new, cache_k, cache_v, layer, lq1, lk1, lq2, lk2, sub,
                 *, t_new, pages_per_step, lam_init):
    t, bw = q.shape
    heads = bw // HEAD_DIM
    dec_batch, n_pages = page_table.shape
    page_rows = cache_k.shape[2]
    steps = n_pages // pages_per_step
    pt_flat = page_table.reshape(-1)

    def page_spec(p):
        return pl.BlockSpec(
            (None, None, page_rows, HEAD_DIM),
            lambda b, j, pt: (layer, pt[b * n_pages + j * pages_per_step + p], 0, 0))

    q_spec = pl.BlockSpec((t_new, bw), lambda b, j, pt: (b, 0))
    new_spec = pl.BlockSpec((t_new * heads, HEAD_DIM), lambda b, j, pt: (b, 0))
    vec = lambda a: pl.BlockSpec(a.shape, lambda b, j, pt: (0, 0))
    grid_spec = pltpu.PrefetchScalarGridSpec(
        num_scalar_prefetch=1,
        grid=(dec_batch, steps),
        in_specs=([q_spec, new_spec, new_spec]
                  + [page_spec(p) for p in range(pages_per_step)] * 2
                  + [vec(lq1), vec(lk1), vec(lq2), vec(lk2), vec(sub)]),
        out_specs=q_spec,
        scratch_shapes=[pltpu.VMEM((heads * 2 * t_new, HEAD_DIM), BF16),
                        pltpu.VMEM((heads * 2 * t_new, pages_per_step * page_rows), F32),
                        pltpu.VMEM((heads * 2 * t_new, LANES), F32),
                        pltpu.VMEM((heads * 2 * t_new, LANES), F32),
                        pltpu.VMEM((heads * 2 * t_new, HEAD_DIM), F32)])
    return pl.pallas_call(
        functools.partial(_attn_sample_kernel, pages_per_step=pages_per_step, heads=heads,
                          lam_init=lam_init),
        grid_spec=grid_spec,
        out_shape=jax.ShapeDtypeStruct((t, bw), F32),
        compiler_params=_params(("parallel", "arbitrary")),
        name="attn_sample",
    )(pt_flat, q, k_new, v_new, *([cache_k] * pages_per_step), *([cache_v] * pages_per_step),
      lq1, lk1, lq2, lk2, sub)


def _swiglu_residual(x1, g_ref, w_in_ref, w_out_ref, ff_chunk):
    d_ff = w_out_ref.shape[0]
    h = _rms(x1, g_ref[...]).astype(BF16)
    acc = x1
    for c0 in range(0, d_ff, ff_chunk):
        gate = jnp.dot(h, w_in_ref[:, c0:c0 + ff_chunk], preferred_element_type=F32)
        up = jnp.dot(h, w_in_ref[:, d_ff + c0:d_ff + c0 + ff_chunk], preferred_element_type=F32)
        act = (gate * jax.nn.sigmoid(gate) * up).astype(BF16)
        acc = acc + jnp.dot(act, w_out_ref[c0:c0 + ff_chunk, :], preferred_element_type=F32)
    return acc


def _mix_ffn_kernel(x_ref, a_ref, b_ref, wo_ref, g_ref, w_in_ref, w_out_ref, o_ref, *, ff_chunk):
    aw = a_ref.shape[1]
    o_ref[...] = (x_ref[...]
                  + jnp.dot(a_ref[...].astype(BF16), wo_ref[0:aw, :], preferred_element_type=F32)
                  + jnp.dot(b_ref[...].astype(BF16), wo_ref[aw:, :], preferred_element_type=F32))
    o_ref[...] = _swiglu_residual(o_ref[...], g_ref, w_in_ref, w_out_ref, ff_chunk)


def _ffn_final_kernel(x_ref, g_ref, w_in_ref, w_out_ref, gf_ref, o_ref, *, ff_chunk):
    x2 = _swiglu_residual(x_ref[...], g_ref, w_in_ref, w_out_ref, ff_chunk)
    o_ref[...] = _rms(x2, gf_ref[...])


def _mix_ffn(x, a, b, wo, g, w_in, w_out, *, tm, ff_chunk, name):
    t, d = x.shape
    row = lambda arr: pl.BlockSpec((tm, arr.shape[1]), lambda i: (i, 0))
    return pl.pallas_call(
        functools.partial(_mix_ffn_kernel, ff_chunk=ff_chunk),
        grid=(t // tm,),
        in_specs=[row(x), row(a), row(b), _const_spec(wo.shape), _const_spec(g.shape),
                  _const_spec(w_in.shape), _const_spec(w_out.shape)],
        out_specs=row(x),
        out_shape=jax.ShapeDtypeStruct((t, d), F32),
        compiler_params=_params(("parallel",)),
        name=name,
    )(x, a, b, wo, g, w_in, w_out)


def _ffn_final(x, g, w_in, w_out, gf, *, tm, ff_chunk, name):
    t, d = x.shape
    row = pl.BlockSpec((tm, d), lambda i: (i, 0))
    return pl.pallas_call(
        functools.partial(_ffn_final_kernel, ff_chunk=ff_chunk),
        grid=(t // tm,),
        in_specs=[row, _const_spec(g.shape), _const_spec(w_in.shape), _const_spec(w_out.shape),
                  _const_spec(gf.shape)],
        out_specs=row,
        out_shape=jax.ShapeDtypeStruct((t, d), F32),
        compiler_params=_params(("parallel",)),
        name=name,
    )(x, g, w_in, w_out, gf)


def _s5_kernel(x_ref, g_ref, s0re_ref, s0im_ref, coef_ref, bblk_ref, cblk_ref, d_ref,
               wglu_ref, bglu_ref, o_ref, sre_ref, sim_ref,
               re_s, im_s, gel_s, car_re, car_im):
    tl, d = x_ref.shape
    n_slabs = d // LANES
    slab_state = re_s.shape[1] // n_slabs

    @pl.when(pl.program_id(1) == 0)
    def _():
        car_re[...] = s0re_ref[...]
        car_im[...] = s0im_ref[...]

    x = x_ref[...]
    h = _rms(x, g_ref[...])
    hb = h.astype(BF16)
    for j in range(n_slabs):
        bu = jnp.dot(hb[:, j * LANES:(j + 1) * LANES], bblk_ref[j], preferred_element_type=F32)
        re_s[:, j * slab_state:(j + 1) * slab_state] = bu[:, :slab_state]
        im_s[:, j * slab_state:(j + 1) * slab_state] = bu[:, slab_state:]

    def row_tile(r, carry):
        cr, ci = carry
        rows = pl.ds(pl.multiple_of(r * SUBLANES, SUBLANES), SUBLANES)
        wr = re_s[rows, :]
        wi = im_s[rows, :]
        for k, shift in enumerate(SCAN_SHIFTS):
            ar = coef_ref[2 * k]
            ai = coef_ref[2 * k + 1]
            sr = pltpu.roll(wr, shift, 0)
            si = pltpu.roll(wi, shift, 0)
            wr, wi = wr + ar * sr - ai * si, wi + ar * si + ai * sr
        pr = coef_ref[2 * len(SCAN_SHIFTS)]
        pi = coef_ref[2 * len(SCAN_SHIFTS) + 1]
        wr, wi = wr + pr * cr - pi * ci, wi + pr * ci + pi * cr
        re_s[rows, :] = wr
        im_s[rows, :] = wi
        return wr[SUBLANES - 1:SUBLANES, :], wi[SUBLANES - 1:SUBLANES, :]

    cr, ci = lax.fori_loop(0, tl // SUBLANES, row_tile, (car_re[...], car_im[...]))
    car_re[...] = cr
    car_im[...] = ci
    sre_ref[...] = cr
    sim_ref[...] = ci

    for j in range(n_slabs):
        cols = slice(j * LANES, (j + 1) * LANES)
        st = slice(j * slab_state, (j + 1) * slab_state)
        s_cat = jnp.concatenate([re_s[:, st], im_s[:, st]], axis=1).astype(BF16)
        y = jnp.dot(s_cat, cblk_ref[j], preferred_element_type=F32) + d_ref[:, cols] * h[:, cols]
        gel_s[:, cols] = jax.nn.gelu(y).astype(BF16)
    z = jnp.dot(gel_s[...], wglu_ref[...], preferred_element_type=F32) + bglu_ref[...]
    o_ref[...] = x + z[:, :d] * jax.nn.sigmoid(z[:, d:])


def _s5_mix(x, g, s0re, s0im, coef, bblk, cblk, dskip, wglu, bglu, *, batch, seq, tl, name):
    t, d = x.shape
    nt = seq // tl
    n_state = s0re.shape[-1]
    state_spec = pl.BlockSpec((None, 1, n_state), lambda b, i: (b, 0, 0))
    x1, sre, sim = pl.pallas_call(
        _s5_kernel,
        grid=(batch, nt),
        in_specs=[pl.BlockSpec((tl, d), lambda b, i: (b * nt + i, 0)), _const_spec(g.shape),
                  state_spec, state_spec, _const_spec(coef.shape), _const_spec(bblk.shape),
                  _const_spec(cblk.shape), _const_spec(dskip.shape), _const_spec(wglu.shape),
                  _const_spec(bglu.shape)],
        out_specs=[pl.BlockSpec((tl, d), lambda b, i: (b * nt + i, 0)), state_spec, state_spec],
        out_shape=[jax.ShapeDtypeStruct((t, d), F32),
                   jax.ShapeDtypeStruct((batch, 1, n_state), F32),
                   jax.ShapeDtypeStruct((batch, 1, n_state), F32)],
        scratch_shapes=[pltpu.VMEM((tl, n_state), F32), pltpu.VMEM((tl, n_state), F32),
                        pltpu.VMEM((tl, d), BF16),
                        pltpu.VMEM((1, n_state), F32), pltpu.VMEM((1, n_state), F32)],
        compiler_params=_params(("parallel", "arbitrary")),
        name=name,
    )(x, g, s0re, s0im, coef, bblk, cblk, dskip, wglu, bglu)
    return x1, sre, sim


def _s5_tables(a_re, a_im, log_dt, b_re, b_im, c_re, c_im):
    groups, n_p = a_re.shape
    a = lax.complex(a_re, a_im)
    lam_dt = a * jnp.exp(log_dt)[:, None]
    a_bar = jnp.exp(lam_dt)
    b_bar = ((a_bar - 1.0) / a)[:, :, None] * lax.complex(b_re, b_im)
    rows = jnp.arange(SUBLANES)
    coef = []
    for shift in SCAN_SHIFTS:
        p = jnp.exp(lam_dt * shift).reshape(1, -1) * (rows >= shift)[:, None]
        coef += [jnp.real(p), jnp.imag(p)]
    p = jnp.exp(lam_dt.reshape(1, -1) * (rows + 1)[:, None].astype(F32))
    coef += [jnp.real(p), jnp.imag(p)]
    coef = jnp.stack(coef).astype(F32)
    n_slabs = groups // SLAB_GROUPS
    eye = jnp.eye(SLAB_GROUPS, dtype=F32)

    def in_block(m):
        m = m.reshape(n_slabs, SLAB_GROUPS, n_p, SSM_GROUP)
        return jnp.einsum('jgpc,gh->jgchp', m, eye).reshape(
            n_slabs, SLAB_GROUPS * SSM_GROUP, SLAB_GROUPS * n_p)

    def out_block(m):
        m = m.reshape(n_slabs, SLAB_GROUPS, SSM_GROUP, n_p)
        return jnp.einsum('jgcp,gh->jgphc', m, eye).reshape(
            n_slabs, SLAB_GROUPS * n_p, SLAB_GROUPS * SSM_GROUP)

    bblk = jnp.concatenate([in_block(jnp.real(b_bar)), in_block(jnp.imag(b_bar))], axis=2)
    cblk = jnp.concatenate([out_block(c_re), -out_block(c_im)], axis=1)
    return coef, bblk.astype(BF16), cblk.astype(BF16)


def kernel(x_prompt, x_sample, cache_k, cache_v, page_table, state_ssm_re, state_ssm_im,
           norm_mix, norm_ffn, norm_final, w_in_even, w_out_even, sgu_norm, sgu_w, sgu_b,
           lambda_q1, lambda_k1, lambda_q2, lambda_k2, attn_subln,
           ssm_a_re, ssm_a_im, ssm_log_dt, ssm_b_re, ssm_b_im, ssm_c_re, ssm_c_im, ssm_d,
           w_glu, b_glu, w_ffn_in, w_ffn_out):
    batch, seq, d = x_prompt.shape
    dec_batch, t_new, _ = x_sample.shape
    depth = norm_mix.shape[0]
    assert depth == 2 and seq % CHUNK == 0 and t_new == SUBLANES
    tp = batch * seq
    ts = dec_batch * t_new
    xp = x_prompt.reshape(tp, d)
    xs = x_sample.reshape(ts, d)
    tm_prompt = 512
    ff_chunk = 256
    row = lambda v: v.reshape(1, -1)

    lam_init = 0.8 - 0.6 * math.exp(-0.3 * 0)
    w_in0 = w_in_even[0].astype(BF16)
    w_out0 = w_out_even[0].astype(BF16)
    a_width = sgu_norm.shape[1]
    gw = a_width // A_GROUPS
    tril = jnp.tril(jnp.ones((CHUNK, CHUNK), bool))
    mix_p = jnp.where(tril, sgu_w[0], 0).astype(BF16)
    bias_p = jnp.repeat(sgu_b[0].T, gw, axis=1)
    small = jnp.where(tril[:t_new, :t_new], sgu_w[0][:, :t_new, :t_new], 0)
    mix_s = jnp.einsum('gts,bc->gbtcs', small, jnp.eye(dec_batch, dtype=F32)).reshape(
        A_GROUPS, ts, ts).astype(BF16)
    bias_s = jnp.tile(bias_p[:t_new], (dec_batch, 1))
    lam_vecs = [row(lambda_q1[0]), row(lambda_k1[0]), row(lambda_q2[0]), row(lambda_k2[0]),
                row(attn_subln[0])]

    a_p, q_p, k_p, vb_p, kb_p, vbb_p = _even_in(
        xp, row(norm_mix[0]), w_in0, row(sgu_norm[0]), mix_p, bias_p, tm=tm_prompt, prompt=True)
    a_s, q_s, k_s, vb_s, vn_s = _even_in(
        xs, row(norm_mix[0]), w_in0, row(sgu_norm[0]), mix_s, bias_s, tm=ts, prompt=False)
    b_p = _attn_prompt(q_p, kb_p, vbb_p, *lam_vecs, batch=batch, seq=seq, tq=512,
                       lam_init=lam_init)
    n_layers, n_phys, page_size, heads, _ = cache_k.shape
    page_view = (n_layers, n_phys, page_size * heads, HEAD_DIM)
    b_s = _attn_sample(page_table, q_s, k_s, vb_s, cache_k.reshape(page_view),
                       cache_v.reshape(page_view), 0, *lam_vecs,
                       t_new=t_new, pages_per_step=16, lam_init=lam_init)
    w_ffn_in0 = w_ffn_in[0].astype(BF16)
    w_ffn_out0 = w_ffn_out[0].astype(BF16)
    xp = _mix_ffn(xp, a_p, b_p, w_out0, row(norm_ffn[0]), w_ffn_in0, w_ffn_out0,
                  tm=tm_prompt, ff_chunk=ff_chunk, name="mix_ffn_prompt")
    xs = _mix_ffn(xs, a_s, b_s, w_out0, row(norm_ffn[0]), w_ffn_in0, w_ffn_out0,
                  tm=ts, ff_chunk=ff_chunk, name="mix_ffn_sample")

    groups, n_p = ssm_a_re.shape[1:]
    n_state = groups * n_p
    coef, bblk, cblk = _s5_tables(ssm_a_re[0], ssm_a_im[0], ssm_log_dt[0], ssm_b_re[0],
                                  ssm_b_im[0], ssm_c_re[0], ssm_c_im[0])
    w_glu0 = w_glu[0].astype(BF16)
    s5_args = (coef, bblk, cblk, row(ssm_d[0]), w_glu0, row(b_glu[0]))
    zeros = jnp.zeros((batch, 1, n_state), F32)
    xp, p_re, p_im = _s5_mix(xp, row(norm_mix[1]), zeros, zeros, *s5_args,
                             batch=batch, seq=seq, tl=256, name="s5_prompt")
    xs, s_re, s_im = _s5_mix(xs, row(norm_mix[1]), state_ssm_re[0].reshape(dec_batch, 1, n_state),
                             state_ssm_im[0].reshape(dec_batch, 1, n_state), *s5_args,
                             batch=dec_batch, seq=t_new, tl=t_new, name="s5_sample")
    w_ffn_in1 = w_ffn_in[1].astype(BF16)
    w_ffn_out1 = w_ffn_out[1].astype(BF16)
    yp = _ffn_final(xp, row(norm_ffn[1]), w_ffn_in1, w_ffn_out1, row(norm_final),
                    tm=tm_prompt, ff_chunk=ff_chunk, name="ffn_final_prompt")
    ys = _ffn_final(xs, row(norm_ffn[1]), w_ffn_in1, w_ffn_out1, row(norm_final),
                    tm=ts, ff_chunk=ff_chunk, name="ffn_final_sample")

    st = lambda s, n: s.reshape(1, n, groups, n_p)
    return (yp.reshape(batch, seq, d), ys.reshape(dec_batch, t_new, d),
            k_p.reshape(1, batch, seq, heads, HEAD_DIM), vb_p.reshape(1, batch, seq, heads, HEAD_DIM),
            k_s.reshape(1, dec_batch, t_new, heads, HEAD_DIM),
            vb_s.reshape(1, dec_batch, t_new, heads, HEAD_DIM),
            vn_s.reshape(1, dec_batch, t_new, a_width),
            st(p_re, batch), st(p_im, batch), st(s_re, dec_batch), st(s_im, dec_batch))
```

```python
import functools
import math

import jax
import jax.numpy as jnp
from jax import lax
from jax.experimental import pallas as pl
from jax.experimental.pallas import tpu as pltpu

F32 = jnp.float32
BF16 = jnp.bfloat16

EPS = 1e-6
LANES = 128
SUBLANES = 8
V7X_VMEM_BYTES = 64 * 1024 * 1024
VMEM_LIMIT = V7X_VMEM_BYTES * 7 // 8
MASK_VALUE = -0.7 * float(jnp.finfo(jnp.float32).max)

CHUNK = 128
A_GROUPS = 4
HEAD_DIM = 128
QK_HALF = 64
LOG2E = math.log2(math.e)
SSM_GROUP = 16
SLAB_GROUPS = LANES // SSM_GROUP
SCAN_SHIFTS = (1, 2, 4)
ROW_TOKENS = 4
S5_BLOCK_ROWS = 128


def _const_spec(shape):
    zeros = (0,) * len(shape)
    return pl.BlockSpec(shape, lambda *_: zeros, pipeline_mode=pl.Buffered(1))


def _rms(x, g):
    return x * lax.rsqrt(jnp.mean(x * x, axis=-1, keepdims=True) + EPS) * g


def _params(semantics):
    return pltpu.CompilerParams(dimension_semantics=semantics, vmem_limit_bytes=VMEM_LIMIT)


def _even_in_kernel(x_ref, g_ref, w_ref, sn_ref, mix_ref, bias_ref, *outs, prompt, a_width):
    if prompt:
        a_ref, q_ref, k_ref, vb_ref, kb_ref, vbb_ref = outs
    else:
        a_ref, q_ref, k_ref, vb_ref, vn_ref = outs
    tm = x_ref.shape[0]
    rows_per_mix = mix_ref.shape[1]
    gw = a_width // A_GROUPS
    h = _rms(x_ref[...], g_ref[...]).astype(BF16)

    def proj(c0, width):
        return jnp.dot(h, w_ref[:, c0:c0 + width], preferred_element_type=F32)

    u = jax.nn.gelu(proj(0, a_width))
    v = _rms(jax.nn.gelu(proj(a_width, a_width)), sn_ref[...])
    if not prompt:
        vn_ref[...] = v
    v16 = v.astype(BF16)
    o = 2 * a_width
    bw = q_ref.shape[1]
    q_ref[...] = (proj(o, bw) * (QK_HALF ** -0.5 * LOG2E)).astype(q_ref.dtype)
    k = proj(o + bw, bw)
    vb = proj(o + 2 * bw, bw)
    heads = bw // HEAD_DIM
    for hd in range(heads):
        k_ref[pl.ds(hd, tm, stride=heads), :] = k[:, hd * HEAD_DIM:(hd + 1) * HEAD_DIM]
        vb_ref[pl.ds(hd, tm, stride=heads), :] = vb[:, hd * HEAD_DIM:(hd + 1) * HEAD_DIM]
    if prompt:
        kb_ref[...] = k.astype(BF16)
        vbb_ref[...] = vb.astype(BF16)
    for c in range(tm // rows_per_mix):
        r0 = c * rows_per_mix
        for g in range(A_GROUPS):
            c0 = g * gw
            gate = jnp.dot(mix_ref[g], v16[r0:r0 + rows_per_mix, c0:c0 + gw],
                           preferred_element_type=F32) + bias_ref[:, c0:c0 + gw]
            a_ref[r0:r0 + rows_per_mix, c0:c0 + gw] = (
                u[r0:r0 + rows_per_mix, c0:c0 + gw] * gate).astype(a_ref.dtype)


def _even_in(x, g, w, sn, mix, bias, *, tm, prompt):
    t, d = x.shape
    a_width = sn.shape[1]
    bw = (w.shape[1] - 2 * a_width) // 3
    act = BF16 if prompt else F32
    row = lambda width: pl.BlockSpec((tm, width), lambda i: (i, 0))
    heads = bw // HEAD_DIM
    head_rows = pl.BlockSpec((tm * heads, HEAD_DIM), lambda i: (i, 0))
    out_shape = [jax.ShapeDtypeStruct((t, a_width), act), jax.ShapeDtypeStruct((t, bw), act),
                 jax.ShapeDtypeStruct((t * heads, HEAD_DIM), F32),
                 jax.ShapeDtypeStruct((t * heads, HEAD_DIM), F32)]
    out_specs = [row(a_width), row(bw), head_rows, head_rows]
    if prompt:
        out_shape += [jax.ShapeDtypeStruct((t, bw), BF16)] * 2
        out_specs += [row(bw), row(bw)]
    else:
        out_shape += [jax.ShapeDtypeStruct((t, a_width), F32)]
        out_specs += [row(a_width)]
    return pl.pallas_call(
        functools.partial(_even_in_kernel, prompt=prompt, a_width=a_width),
        grid=(t // tm,),
        in_specs=[row(d), _const_spec(g.shape), _const_spec(w.shape), _const_spec(sn.shape),
                  _const_spec(mix.shape), _const_spec(bias.shape)],
        out_specs=out_specs,
        out_shape=out_shape,
        compiler_params=_params(("parallel",)),
        name="even_in_prompt" if prompt else "even_in_sample",
    )(x, g, w, sn, mix, bias)


def _split_halves(q):
    lane = lax.broadcasted_iota(jnp.int32, q.shape, 1)
    zero = jnp.zeros_like(q)
    return jnp.where(lane < QK_HALF, q, zero), jnp.where(lane >= QK_HALF, q, zero)


def _softmax_step(s, vblk, m_ref, l_ref, acc_ref, rows):
    tiles = [s[:, t * LANES:(t + 1) * LANES] for t in range(s.shape[1] // LANES)]
    m_prev = m_ref[rows, :]
    m_next = jnp.maximum(
        m_prev, jnp.max(functools.reduce(jnp.maximum, tiles), axis=1, keepdims=True))
    p_tiles = [jnp.exp2(t - m_next) for t in tiles]
    alpha = jnp.exp2(m_prev - m_next)
    l_ref[rows, :] = alpha * l_ref[rows, :] + functools.reduce(jnp.add, p_tiles)
    p = jnp.concatenate([t.astype(BF16) for t in p_tiles], axis=1)
    acc_ref[rows, :] = alpha * acc_ref[rows, :] + jnp.dot(p, vblk, preferred_element_type=F32)
    m_ref[rows, :] = m_next


def _row_sum(l):
    return jnp.sum(l, axis=1, keepdims=True)


def _diff_lambda(lq1, lk1, lq2, lk2, lam_init):
    e1 = jnp.exp(jnp.sum(lq1[...] * lk1[...], axis=1, keepdims=True))
    e2 = jnp.exp(jnp.sum(lq2[...] * lk2[...], axis=1, keepdims=True))
    return e1 - e2 + lam_init


def _head_out(o1, o2, lam, sub, lam_init):
    o = o1 - lam * o2
    return _rms(o, sub) * (1.0 - lam_init)


def _attn_prompt_kernel(q_ref, k_ref, v_ref, lq1, lk1, lq2, lk2, sub_ref, o_ref,
                        qs_ref, m_ref, l_ref, acc_ref, *, lam_init):
    tq = q_ref.shape[0]
    i = pl.program_id(2)
    q1, q2 = _split_halves(q_ref[...])
    qs_ref[0:tq, :] = q1
    qs_ref[tq:2 * tq, :] = q2
    m_ref[...] = jnp.full(m_ref.shape, MASK_VALUE, F32)
    l_ref[...] = jnp.zeros(l_ref.shape, F32)
    acc_ref[...] = jnp.zeros(acc_ref.shape, F32)
    all_rows = slice(0, 2 * tq)

    def scores(j):
        kv_rows = pl.ds(pl.multiple_of(j * tq, tq), tq)
        s = lax.dot_general(qs_ref[...], k_ref[kv_rows, :], (((1,), (1,)), ((), ())),
                            preferred_element_type=F32)
        return s, v_ref[kv_rows, :]

    def full_block(j, carry):
        s, vblk = scores(j)
        _softmax_step(s, vblk, m_ref, l_ref, acc_ref, all_rows)
        return carry

    lax.fori_loop(0, i, full_block, 0)
    s, vblk = scores(i)
    row = lax.broadcasted_iota(jnp.int32, s.shape, 0)
    col = lax.broadcasted_iota(jnp.int32, s.shape, 1)
    qpos = jnp.where(row >= tq, row - tq, row)
    s = jnp.where(qpos >= col, s, MASK_VALUE)
    _softmax_step(s, vblk, m_ref, l_ref, acc_ref, all_rows)

    lam = _diff_lambda(lq1, lk1, lq2, lk2, lam_init)
    o1 = acc_ref[0:tq, :] / _row_sum(l_ref[0:tq, :])
    o2 = acc_ref[tq:2 * tq, :] / _row_sum(l_ref[tq:2 * tq, :])
    o_ref[...] = _head_out(o1, o2, lam, sub_ref[...], lam_init).astype(o_ref.dtype)


def _attn_prompt(q, kb, vb, lq1, lk1, lq2, lk2, sub, *, batch, seq, tq, lam_init):
    t, bw = q.shape
    heads = bw // HEAD_DIM
    nq = seq // tq
    vec = lambda a: _const_spec(a.shape)
    return pl.pallas_call(
        functools.partial(_attn_prompt_kernel, lam_init=lam_init),
        grid=(batch, heads, nq),
        in_specs=[pl.BlockSpec((tq, HEAD_DIM), lambda b, h, i: (b * nq + i, h)),
                  pl.BlockSpec((seq, HEAD_DIM), lambda b, h, i: (b, h)),
                  pl.BlockSpec((seq, HEAD_DIM), lambda b, h, i: (b, h)),
                  vec(lq1), vec(lk1), vec(lq2), vec(lk2), vec(sub)],
        out_specs=pl.BlockSpec((tq, HEAD_DIM), lambda b, h, i: (b * nq + i, h)),
        out_shape=jax.ShapeDtypeStruct((t, bw), BF16),
        scratch_shapes=[pltpu.VMEM((2 * tq, HEAD_DIM), BF16),
                        pltpu.VMEM((2 * tq, LANES), F32),
                        pltpu.VMEM((2 * tq, LANES), F32),
                        pltpu.VMEM((2 * tq, HEAD_DIM), F32)],
        compiler_params=_params(("parallel", "parallel", "arbitrary")),
        name="attn_prompt",
    )(q, kb, vb, lq1, lk1, lq2, lk2, sub)


def _attn_sample_kernel(pt_ref, q_ref, kn_ref, vn_ref, *rest, pages_per_step, heads, lam_init):
    del pt_ref
    k_pages = rest[:pages_per_step]
    v_pages = rest[pages_per_step:2 * pages_per_step]
    (lq1, lk1, lq2, lk2, sub_ref, o_ref,
     qs_ref, bias_ref, m_ref, l_ref, acc_ref) = rest[2 * pages_per_step:]
    j = pl.program_id(1)
    t_new = q_ref.shape[0]
    rows_per_head = 2 * t_new
    all_rows = slice(0, heads * rows_per_head)

    def head_match(shape):
        row = lax.broadcasted_iota(jnp.int32, shape, 0)
        col = lax.broadcasted_iota(jnp.int32, shape, 1)
        return row, col, lax.rem(col, heads) == row // rows_per_head

    @pl.when(j == 0)
    def _():
        pieces = []
        for h in range(heads):
            pieces += _split_halves(q_ref[:, h * HEAD_DIM:(h + 1) * HEAD_DIM])
        qs_ref[...] = jnp.concatenate(pieces, axis=0).astype(BF16)
        _, _, ok = head_match(bias_ref.shape)
        bias_ref[...] = jnp.where(ok, 0.0, MASK_VALUE)
        m_ref[...] = jnp.full(m_ref.shape, MASK_VALUE, F32)
        l_ref[...] = jnp.zeros(l_ref.shape, F32)
        acc_ref[...] = jnp.zeros(acc_ref.shape, F32)

    def scores(kblk):
        return lax.dot_general(qs_ref[...], kblk.astype(BF16), (((1,), (1,)), ((), ())),
                               preferred_element_type=F32)

    kblk = jnp.concatenate([k_pages[p][...] for p in range(pages_per_step)], axis=0)
    vblk = jnp.concatenate([v_pages[p][...] for p in range(pages_per_step)], axis=0)
    _softmax_step(scores(kblk) + bias_ref[...], vblk.astype(BF16), m_ref, l_ref, acc_ref, all_rows)

    @pl.when(j == pl.num_programs(1) - 1)
    def _():
        pad = jnp.zeros((LANES - kn_ref.shape[0], HEAD_DIM), F32)
        s = scores(jnp.concatenate([kn_ref[...], pad], axis=0))
        row, col, ok = head_match(s.shape)
        s = jnp.where(ok, s, MASK_VALUE)
        s = jnp.where(col // heads <= lax.rem(row, t_new), s, MASK_VALUE)
        vblk = jnp.concatenate([vn_ref[...], pad], axis=0)
        _softmax_step(s, vblk.astype(BF16), m_ref, l_ref, acc_ref, all_rows)
        lam = _diff_lambda(lq1, lk1, lq2, lk2, lam_init)
        for h in range(heads):
            r0 = h * rows_per_head
            o1 = acc_ref[r0:r0 + t_new, :] / _row_sum(l_ref[r0:r0 + t_new, :])
            o2 = (acc_ref[r0 + t_new:r0 + rows_per_head, :]
                  / _row_sum(l_ref[r0 + t_new:r0 + rows_per_head, :]))
            o_ref[:, h * HEAD_DIM:(h + 1) * HEAD_DIM] = _head_out(
                o1, o2, lam, sub_ref[...], lam_init)


def _attn_sample(page_table, q, k_new, v_new, cache_k, cache_v, layer, lq1, lk1, lq2, lk2, sub,
                 *, t_new, pages_per_step, lam_init):
    t, bw = q.shape
    heads = bw // HEAD_DIM
    dec_batch, n_pages = page_table.shape
    page_rows = cache_k.shape[2]
    steps = n_pages // pages_per_step
    pt_flat = page_table.reshape(-1)

    def page_spec(p):
        return pl.BlockSpec(
            (None, None, page_rows, HEAD_DIM),
            lambda b, j, pt: (layer, pt[b * n_pages + j * pages_per_step + p], 0, 0))

    q_spec = pl.BlockSpec((t_new, bw), lambda b, j, pt: (b, 0))
    new_spec = pl.BlockSpec((t_new * heads, HEAD_DIM), lambda b, j, pt: (b, 0))
    vec = lambda a: pl.BlockSpec(a.shape, lambda b, j, pt: (0, 0))
    grid_spec = pltpu.PrefetchScalarGridSpec(
        num_scalar_prefetch=1,
        grid=(dec_batch, steps),
        in_specs=([q_spec, new_spec, new_spec]
                  + [page_spec(p) for p in range(pages_per_step)] * 2
                  + [vec(lq1), vec(lk1), vec(lq2), vec(lk2), vec(sub)]),
        out_specs=q_spec,
        scratch_shapes=[pltpu.VMEM((heads * 2 * t_new, HEAD_DIM), BF16),
                        pltpu.VMEM((heads * 2 * t_new, pages_per_step * page_rows), F32),
                        pltpu.VMEM((heads * 2 * t_new, LANES), F32),
                        pltpu.VMEM((heads * 2 * t_new, LANES), F32),
                        pltpu.VMEM((heads * 2 * t_new, HEAD_DIM), F32)])
    return pl.pallas_call(
        functools.partial(_attn_sample_kernel, pages_per_step=pages_per_step, heads=heads,
                          lam_init=lam_init),
        grid_spec=grid_spec,
        out_shape=jax.ShapeDtypeStruct((t, bw), F32),
        compiler_params=_params(("parallel", "arbitrary")),
        name="attn_sample",
    )(pt_flat, q, k_new, v_new, *([cache_k] * pages_per_step), *([cache_v] * pages_per_step),
      lq1, lk1, lq2, lk2, sub)


def _swiglu_residual(x1, g_ref, w_in_ref, w_out_ref, ff_chunk):
    d_ff = w_out_ref.shape[0]
    h = _rms(x1, g_ref[...]).astype(BF16)
    acc = x1
    for c0 in range(0, d_ff, ff_chunk):
        gate = jnp.dot(h, w_in_ref[:, c0:c0 + ff_chunk], preferred_element_type=F32)
        up = jnp.dot(h, w_in_ref[:, d_ff + c0:d_ff + c0 + ff_chunk], preferred_element_type=F32)
        act = (gate * jax.nn.sigmoid(gate) * up).astype(BF16)
        acc = acc + jnp.dot(act, w_out_ref[c0:c0 + ff_chunk, :], preferred_element_type=F32)
    return acc


def _mix_ffn_kernel(x_ref, a_ref, b_ref, wo_ref, g_ref, w_in_ref, w_out_ref, o_ref, *scratch,
                    ff_chunk, row_tokens):
    tm, d = x_ref.shape
    ab = jnp.concatenate([a_ref[...].astype(BF16), b_ref[...].astype(BF16)], axis=1)
    x1 = x_ref[...] + jnp.dot(ab, wo_ref[...], preferred_element_type=F32)
    y = _swiglu_residual(x1, g_ref, w_in_ref, w_out_ref, ff_chunk)
    if row_tokens == 1:
        o_ref[...] = y
        return
    slabs = scratch[0]
    for j in range(d // LANES):
        slabs[j] = y[:, j * LANES:(j + 1) * LANES]
    for t in range(row_tokens):
        for j in range(d // LANES):
            o_ref[:, t * d + j * LANES:t * d + (j + 1) * LANES] = (
                slabs[j, pl.ds(t, tm // row_tokens, stride=row_tokens), :])


def _ffn_final_kernel(x_ref, g_ref, w_in_ref, w_out_ref, gf_ref, o_ref, *scratch,
                      ff_chunk, row_tokens):
    if row_tokens == 1:
        x = x_ref[...]
    else:
        r = x_ref.shape[0]
        d = x_ref.shape[1] // row_tokens
        slabs = scratch[0]
        for t in range(row_tokens):
            for j in range(d // LANES):
                slabs[j, pl.ds(t, r, stride=row_tokens), :] = (
                    x_ref[:, t * d + j * LANES:t * d + (j + 1) * LANES])
        x = jnp.concatenate([slabs[j] for j in range(d // LANES)], axis=1)
    o_ref[...] = _rms(_swiglu_residual(x, g_ref, w_in_ref, w_out_ref, ff_chunk), gf_ref[...])


def _mix_ffn(x, a, b, wo, g, w_in, w_out, *, tm, ff_chunk, row_tokens, name):
    t, d = x.shape
    row = lambda arr: pl.BlockSpec((tm, arr.shape[1]), lambda i: (i, 0))
    packed = row_tokens > 1
    return pl.pallas_call(
        functools.partial(_mix_ffn_kernel, ff_chunk=ff_chunk, row_tokens=row_tokens),
        grid=(t // tm,),
        in_specs=[row(x), row(a), row(b), _const_spec(wo.shape), _const_spec(g.shape),
                  _const_spec(w_in.shape), _const_spec(w_out.shape)],
        out_specs=pl.BlockSpec((tm // row_tokens, row_tokens * d), lambda i: (i, 0)),
        out_shape=jax.ShapeDtypeStruct((t // row_tokens, row_tokens * d), F32),
        scratch_shapes=[pltpu.VMEM((d // LANES, tm, LANES), F32)] if packed else [],
        compiler_params=_params(("parallel",)),
        name=name,
    )(x, a, b, wo, g, w_in, w_out)


def _ffn_final(x, g, w_in, w_out, gf, *, tm, ff_chunk, row_tokens, name):
    rows, width = x.shape
    d = width // row_tokens
    t = rows * row_tokens
    packed = row_tokens > 1
    return pl.pallas_call(
        functools.partial(_ffn_final_kernel, ff_chunk=ff_chunk, row_tokens=row_tokens),
        grid=(t // tm,),
        in_specs=[pl.BlockSpec((tm // row_tokens, width), lambda i: (i, 0)),
                  _const_spec(g.shape), _const_spec(w_in.shape), _const_spec(w_out.shape),
                  _const_spec(gf.shape)],
        out_specs=pl.BlockSpec((tm, d), lambda i: (i, 0)),
        out_shape=jax.ShapeDtypeStruct((t, d), F32),
        scratch_shapes=[pltpu.VMEM((d // LANES, tm, LANES), F32)] if packed else [],
        compiler_params=_params(("parallel",)),
        name=name,
    )(x, g, w_in, w_out, gf)


def _s5_kernel(x_ref, g_ref, s0re_ref, s0im_ref, coef_ref, bblk_ref, cblk_ref, d_ref,
               wglu_ref, bglu_ref, o_ref, sre_ref, sim_ref,
               re_s, im_s, gel_s, car_re, car_im):
    tl, d = x_ref.shape
    n_slabs = d // LANES
    slab_state = re_s.shape[1] // n_slabs

    @pl.when(pl.program_id(1) == 0)
    def _():
        car_re[...] = s0re_ref[...]
        car_im[...] = s0im_ref[...]

    x = x_ref[...]
    h = _rms(x, g_ref[...])
    hb = h.astype(BF16)
    for j in range(n_slabs):
        bu = jnp.dot(hb[:, j * LANES:(j + 1) * LANES], bblk_ref[j], preferred_element_type=F32)
        re_s[:, j * slab_state:(j + 1) * slab_state] = bu[:, :slab_state]
        im_s[:, j * slab_state:(j + 1) * slab_state] = bu[:, slab_state:]

    def row_tile(r, carry):
        cr, ci = carry
        rows = pl.ds(pl.multiple_of(r * SUBLANES, SUBLANES), SUBLANES)
        wr = re_s[rows, :]
        wi = im_s[rows, :]
        for k, shift in enumerate(SCAN_SHIFTS):
            ar = coef_ref[2 * k]
            ai = coef_ref[2 * k + 1]
            sr = pltpu.roll(wr, shift, 0)
            si = pltpu.roll(wi, shift, 0)
            wr, wi = wr + ar * sr - ai * si, wi + ar * si + ai * sr
        pr = coef_ref[2 * len(SCAN_SHIFTS)]
        pi = coef_ref[2 * len(SCAN_SHIFTS) + 1]
        wr, wi = wr + pr * cr - pi * ci, wi + pr * ci + pi * cr
        re_s[rows, :] = wr
        im_s[rows, :] = wi
        return wr[SUBLANES - 1:SUBLANES, :], wi[SUBLANES - 1:SUBLANES, :]

    cr, ci = lax.fori_loop(0, tl // SUBLANES, row_tile, (car_re[...], car_im[...]))
    car_re[...] = cr
    car_im[...] = ci
    sre_ref[...] = cr
    sim_ref[...] = ci

    for j in range(n_slabs):
        cols = slice(j * LANES, (j + 1) * LANES)
        st = slice(j * slab_state, (j + 1) * slab_state)
        s_cat = jnp.concatenate([re_s[:, st], im_s[:, st]], axis=1).astype(BF16)
        y = jnp.dot(s_cat, cblk_ref[j], preferred_element_type=F32) + d_ref[:, cols] * h[:, cols]
        gel_s[:, cols] = jax.nn.gelu(y).astype(BF16)
    z = jnp.dot(gel_s[...], wglu_ref[...], preferred_element_type=F32) + bglu_ref[...]
    o_ref[...] = x + z[:, :d] * jax.nn.sigmoid(z[:, d:])


def _s5_mix(x, g, s0re, s0im, coef, bblk, cblk, dskip, wglu, bglu, *, batch, seq, tl, name):
    t, d = x.shape
    nt = seq // tl
    n_state = s0re.shape[-1]
    state_spec = pl.BlockSpec((None, 1, n_state), lambda b, i: (b, 0, 0))
    x1, sre, sim = pl.pallas_call(
        _s5_kernel,
        grid=(batch, nt),
        in_specs=[pl.BlockSpec((tl, d), lambda b, i: (b * nt + i, 0)), _const_spec(g.shape),
                  state_spec, state_spec, _const_spec(coef.shape), _const_spec(bblk.shape),
                  _const_spec(cblk.shape), _const_spec(dskip.shape), _const_spec(wglu.shape),
                  _const_spec(bglu.shape)],
        out_specs=[pl.BlockSpec((tl, d), lambda b, i: (b * nt + i, 0)), state_spec, state_spec],
        out_shape=[jax.ShapeDtypeStruct((t, d), F32),
                   jax.ShapeDtypeStruct((batch, 1, n_state), F32),
                   jax.ShapeDtypeStruct((batch, 1, n_state), F32)],
        scratch_shapes=[pltpu.VMEM((tl, n_state), F32), pltpu.VMEM((tl, n_state), F32),
                        pltpu.VMEM((tl, d), BF16),
                        pltpu.VMEM((1, n_state), F32), pltpu.VMEM((1, n_state), F32)],
        compiler_params=_params(("parallel", "arbitrary")),
        name=name,
    )(x, g, s0re, s0im, coef, bblk, cblk, dskip, wglu, bglu)
    return x1, sre, sim


def _s5_tables(a_re, a_im, log_dt, b_re, b_im, c_re, c_im):
    groups, n_p = a_re.shape
    a = lax.complex(a_re, a_im)
    lam_dt = a * jnp.exp(log_dt)[:, None]
    a_bar = jnp.exp(lam_dt)
    b_bar = ((a_bar - 1.0) / a)[:, :, None] * lax.complex(b_re, b_im)
    rows = jnp.arange(SUBLANES)
    coef = []
    for shift in SCAN_SHIFTS:
        p = jnp.exp(lam_dt * shift).reshape(1, -1) * (rows >= shift)[:, None]
        coef += [jnp.real(p), jnp.imag(p)]
    p = jnp.exp(lam_dt.reshape(1, -1) * (rows + 1)[:, None].astype(F32))
    coef += [jnp.real(p), jnp.imag(p)]
    coef = jnp.stack(coef).astype(F32)
    n_slabs = groups // SLAB_GROUPS
    eye = jnp.eye(SLAB_GROUPS, dtype=F32)

    def in_block(m):
        m = m.reshape(n_slabs, SLAB_GROUPS, n_p, SSM_GROUP)
        return jnp.einsum('jgpc,gh->jgchp', m, eye).reshape(
            n_slabs, SLAB_GROUPS * SSM_GROUP, SLAB_GROUPS * n_p)

    def out_block(m):
        m = m.reshape(n_slabs, SLAB_GROUPS, SSM_GROUP, n_p)
        return jnp.einsum('jgcp,gh->jgphc', m, eye).reshape(
            n_slabs, SLAB_GROUPS * n_p, SLAB_GROUPS * SSM_GROUP)

    bblk = jnp.concatenate([in_block(jnp.real(b_bar)), in_block(jnp.imag(b_bar))], axis=2)
    cblk = jnp.concatenate([out_block(c_re), -out_block(c_im)], axis=1)
    return coef, bblk.astype(BF16), cblk.astype(BF16)


def _s5_rows_kernel(x_ref, g_ref, s0re_ref, s0im_ref, coef_ref, wb_ref, wy_ref, wc_ref, d_ref,
                    wglu_ref, bglu_ref, o_ref, sre_ref, sim_ref,
                    re_s, im_s, gel_s, car_re, car_im, *, row_tokens):
    r, width = x_ref.shape
    d = width // row_tokens
    n_slabs = d // LANES
    slab_state = re_s.shape[1] // n_slabs

    @pl.when(pl.program_id(1) == 0)
    def _():
        car_re[...] = s0re_ref[...]
        car_im[...] = s0im_ref[...]

    xs = [x_ref[:, t * d:(t + 1) * d] for t in range(row_tokens)]
    hs = [_rms(x, g_ref[...]) for x in xs]
    hb = [h.astype(BF16) for h in hs]

    def slab_lhs(j):
        return jnp.concatenate([h[:, j * LANES:(j + 1) * LANES] for h in hb], axis=1)

    for j in range(n_slabs):
        w = jnp.dot(slab_lhs(j), wb_ref[j], preferred_element_type=F32)
        re_s[:, j * slab_state:(j + 1) * slab_state] = w[:, :slab_state]
        im_s[:, j * slab_state:(j + 1) * slab_state] = w[:, slab_state:]

    def row_tile(i, carry):
        cr, ci = carry
        rows = pl.ds(pl.multiple_of(i * SUBLANES, SUBLANES), SUBLANES)
        wr = re_s[rows, :]
        wi = im_s[rows, :]
        for k, shift in enumerate(SCAN_SHIFTS):
            ar = coef_ref[2 * k]
            ai = coef_ref[2 * k + 1]
            sr = pltpu.roll(wr, shift, 0)
            si = pltpu.roll(wi, shift, 0)
            wr, wi = wr + ar * sr - ai * si, wi + ar * si + ai * sr
        pr = coef_ref[2 * len(SCAN_SHIFTS)]
        pi = coef_ref[2 * len(SCAN_SHIFTS) + 1]
        wr, wi = wr + pr * cr - pi * ci, wi + pr * ci + pi * cr
        first = lax.broadcasted_iota(jnp.int32, wr.shape, 0) == 0
        re_s[rows, :] = jnp.where(first, cr, pltpu.roll(wr, 1, 0))
        im_s[rows, :] = jnp.where(first, ci, pltpu.roll(wi, 1, 0))
        return wr[SUBLANES - 1:SUBLANES, :], wi[SUBLANES - 1:SUBLANES, :]

    cr, ci = lax.fori_loop(0, r // SUBLANES, row_tile, (car_re[...], car_im[...]))
    car_re[...] = cr
    car_im[...] = ci
    sre_ref[...] = cr
    sim_ref[...] = ci

    for j in range(n_slabs):
        cols = slice(j * LANES, (j + 1) * LANES)
        st = slice(j * slab_state, (j + 1) * slab_state)
        s_cat = jnp.concatenate([re_s[:, st], im_s[:, st]], axis=1).astype(BF16)
        y = (jnp.dot(slab_lhs(j), wy_ref[j], preferred_element_type=F32)
             + jnp.dot(s_cat, wc_ref[j], preferred_element_type=F32))
        for t in range(row_tokens):
            yt = y[:, t * LANES:(t + 1) * LANES] + d_ref[:, cols] * hs[t][:, cols]
            gel_s[t * r:(t + 1) * r, cols] = jax.nn.gelu(yt).astype(BF16)
    z = jnp.dot(gel_s[...], wglu_ref[...], preferred_element_type=F32) + bglu_ref[...]
    for t in range(row_tokens):
        zt = z[t * r:(t + 1) * r, :]
        o_ref[:, t * d:(t + 1) * d] = xs[t] + zt[:, :d] * jax.nn.sigmoid(zt[:, d:])


def _s5_rows_mix(x, g, s0re, s0im, coef, wb, wy, wc, dskip, wglu, bglu, *, batch, rows_per_seq,
                 r, row_tokens, name):
    rows, width = x.shape
    d = width // row_tokens
    nt = rows_per_seq // r
    n_state = s0re.shape[-1]
    state_spec = pl.BlockSpec((None, 1, n_state), lambda b, i: (b, 0, 0))
    x_spec = pl.BlockSpec((r, width), lambda b, i: (b * nt + i, 0))
    consts = (g, coef, wb, wy, wc, dskip, wglu, bglu)
    x1, sre, sim = pl.pallas_call(
        functools.partial(_s5_rows_kernel, row_tokens=row_tokens),
        grid=(batch, nt),
        in_specs=[x_spec, _const_spec(g.shape), state_spec, state_spec]
                 + [_const_spec(c.shape) for c in consts[1:]],
        out_specs=[x_spec, state_spec, state_spec],
        out_shape=[jax.ShapeDtypeStruct((rows, width), F32),
                   jax.ShapeDtypeStruct((batch, 1, n_state), F32),
                   jax.ShapeDtypeStruct((batch, 1, n_state), F32)],
        scratch_shapes=[pltpu.VMEM((r, n_state), F32), pltpu.VMEM((r, n_state), F32),
                        pltpu.VMEM((r * row_tokens, d), BF16),
                        pltpu.VMEM((1, n_state), F32), pltpu.VMEM((1, n_state), F32)],
        compiler_params=_params(("parallel", "arbitrary")),
        name=name,
    )(x, g, s0re, s0im, *consts[1:])
    return x1, sre, sim


def _s5_row_tables(a_re, a_im, log_dt, b_re, b_im, c_re, c_im, row_tokens):
    groups, n_p = a_re.shape
    p_tok = row_tokens
    hi = lax.Precision.HIGHEST
    a = lax.complex(a_re, a_im)
    lam_dt = a * jnp.exp(log_dt)[:, None]
    power = lambda k: jnp.exp(lam_dt * k)
    b_bar = ((power(1) - 1.0) / a)[:, :, None] * lax.complex(b_re, b_im)
    c = lax.complex(c_re, c_im)
    rows = jnp.arange(SUBLANES)
    coef = []
    for shift in SCAN_SHIFTS:
        pw = power(p_tok * shift).reshape(1, -1) * (rows >= shift)[:, None]
        coef += [jnp.real(pw), jnp.imag(pw)]
    pw = jnp.exp(lam_dt.reshape(1, -1) * (p_tok * (rows + 1))[:, None].astype(F32))
    coef += [jnp.real(pw), jnp.imag(pw)]
    coef = jnp.stack(coef).astype(F32)
    n_slabs = groups // SLAB_GROUPS
    eye = jnp.eye(SLAB_GROUPS, dtype=F32)
    slabbed = lambda m, lead: m.reshape(m.shape[:lead] + (n_slabs, SLAB_GROUPS) + m.shape[lead + 1:])
    in_w = p_tok * SLAB_GROUPS * SSM_GROUP
    st_w = SLAB_GROUPS * n_p

    e = jnp.stack([power(p_tok - 1 - s)[:, :, None] * b_bar for s in range(p_tok)])
    in_block = lambda m: jnp.einsum('sjgpd,gh->jsgdhp', slabbed(m, 1), eye,
                                    precision=hi).reshape(n_slabs, in_w, st_w)
    wb = jnp.concatenate([in_block(jnp.real(e)), in_block(jnp.imag(e))], axis=2)

    ct = jnp.stack([c * power(t + 1)[:, None, :] for t in range(p_tok)])
    out_block = lambda m: jnp.einsum('tjgcp,gh->jgpthc', slabbed(m, 1), eye,
                                     precision=hi).reshape(n_slabs, st_w, in_w)
    wc = jnp.concatenate([out_block(jnp.real(ct)), -out_block(jnp.imag(ct))], axis=1)

    lag = [jnp.real(jnp.einsum('gcp,gpd->gcd', c * power(l)[:, None, :], b_bar, precision=hi))
           for l in range(p_tok)]
    zero = jnp.zeros_like(lag[0])
    toep = jnp.stack([jnp.stack([lag[t - s] if t >= s else zero for t in range(p_tok)])
                      for s in range(p_tok)])
    wy = jnp.einsum('stjgcd,gh->jsgdthc', slabbed(toep, 2), eye,
                    precision=hi).reshape(n_slabs, in_w, in_w)
    return coef, wb.astype(BF16), wy.astype(BF16), wc.astype(BF16)


def kernel(x_prompt, x_sample, cache_k, cache_v, page_table, state_ssm_re, state_ssm_im,
           norm_mix, norm_ffn, norm_final, w_in_even, w_out_even, sgu_norm, sgu_w, sgu_b,
           lambda_q1, lambda_k1, lambda_q2, lambda_k2, attn_subln,
           ssm_a_re, ssm_a_im, ssm_log_dt, ssm_b_re, ssm_b_im, ssm_c_re, ssm_c_im, ssm_d,
           w_glu, b_glu, w_ffn_in, w_ffn_out):
    batch, seq, d = x_prompt.shape
    dec_batch, t_new, _ = x_sample.shape
    depth = norm_mix.shape[0]
    assert depth == 2 and seq % CHUNK == 0 and t_new == SUBLANES
    tp = batch * seq
    ts = dec_batch * t_new
    xp = x_prompt.reshape(tp, d)
    xs = x_sample.reshape(ts, d)
    tm_prompt = 512
    ff_chunk = 256
    row = lambda v: v.reshape(1, -1)

    lam_init = 0.8 - 0.6 * math.exp(-0.3 * 0)
    w_in0 = w_in_even[0].astype(BF16)
    w_out0 = w_out_even[0].astype(BF16)
    a_width = sgu_norm.shape[1]
    gw = a_width // A_GROUPS
    tril = jnp.tril(jnp.ones((CHUNK, CHUNK), bool))
    mix_p = jnp.where(tril, sgu_w[0], 0).astype(BF16)
    bias_p = jnp.repeat(sgu_b[0].T, gw, axis=1)
    small = jnp.where(tril[:t_new, :t_new], sgu_w[0][:, :t_new, :t_new], 0)
    mix_s = jnp.einsum('gts,bc->gbtcs', small, jnp.eye(dec_batch, dtype=F32)).reshape(
        A_GROUPS, ts, ts).astype(BF16)
    bias_s = jnp.tile(bias_p[:t_new], (dec_batch, 1))
    lam_vecs = [row(lambda_q1[0]), row(lambda_k1[0]), row(lambda_q2[0]), row(lambda_k2[0]),
                row(attn_subln[0])]

    a_p, q_p, k_p, vb_p, kb_p, vbb_p = _even_in(
        xp, row(norm_mix[0]), w_in0, row(sgu_norm[0]), mix_p, bias_p, tm=tm_prompt, prompt=True)
    a_s, q_s, k_s, vb_s, vn_s = _even_in(
        xs, row(norm_mix[0]), w_in0, row(sgu_norm[0]), mix_s, bias_s, tm=ts, prompt=False)
    b_p = _attn_prompt(q_p, kb_p, vbb_p, *lam_vecs, batch=batch, seq=seq, tq=512,
                       lam_init=lam_init)
    n_layers, n_phys, page_size, heads, _ = cache_k.shape
    page_view = (n_layers, n_phys, page_size * heads, HEAD_DIM)
    b_s = _attn_sample(page_table, q_s, k_s, vb_s, cache_k.reshape(page_view),
                       cache_v.reshape(page_view), 0, *lam_vecs,
                       t_new=t_new, pages_per_step=16, lam_init=lam_init)
    w_ffn_in0 = w_ffn_in[0].astype(BF16)
    w_ffn_out0 = w_ffn_out[0].astype(BF16)
    xp = _mix_ffn(xp, a_p, b_p, w_out0, row(norm_ffn[0]), w_ffn_in0, w_ffn_out0,
                  tm=tm_prompt, ff_chunk=ff_chunk, row_tokens=ROW_TOKENS, name="mix_ffn_prompt")
    xs = _mix_ffn(xs, a_s, b_s, w_out0, row(norm_ffn[0]), w_ffn_in0, w_ffn_out0,
                  tm=ts, ff_chunk=ff_chunk, row_tokens=1, name="mix_ffn_sample")

    groups, n_p = ssm_a_re.shape[1:]
    n_state = groups * n_p
    ssm = (ssm_a_re[0], ssm_a_im[0], ssm_log_dt[0], ssm_b_re[0], ssm_b_im[0], ssm_c_re[0],
           ssm_c_im[0])
    coef, bblk, cblk = _s5_tables(*ssm)
    w_glu0 = w_glu[0].astype(BF16)
    s5_args = (coef, bblk, cblk, row(ssm_d[0]), w_glu0, row(b_glu[0]))
    zeros = jnp.zeros((batch, 1, n_state), F32)
    coef_r, wb_r, wy_r, wc_r = _s5_row_tables(*ssm, ROW_TOKENS)
    xp, p_re, p_im = _s5_rows_mix(xp, row(norm_mix[1]), zeros, zeros, coef_r, wb_r, wy_r, wc_r,
                                  row(ssm_d[0]), w_glu0, row(b_glu[0]), batch=batch,
                                  rows_per_seq=seq // ROW_TOKENS, r=S5_BLOCK_ROWS,
                                  row_tokens=ROW_TOKENS, name="s5_prompt")
    xs, s_re, s_im = _s5_mix(xs, row(norm_mix[1]), state_ssm_re[0].reshape(dec_batch, 1, n_state),
                             state_ssm_im[0].reshape(dec_batch, 1, n_state), *s5_args,
                             batch=dec_batch, seq=t_new, tl=t_new, name="s5_sample")
    w_ffn_in1 = w_ffn_in[1].astype(BF16)
    w_ffn_out1 = w_ffn_out[1].astype(BF16)
    yp = _ffn_final(xp, row(norm_ffn[1]), w_ffn_in1, w_ffn_out1, row(norm_final), tm=tm_prompt,
                    ff_chunk=ff_chunk, row_tokens=ROW_TOKENS, name="ffn_final_prompt")
    ys = _ffn_final(xs, row(norm_ffn[1]), w_ffn_in1, w_ffn_out1, row(norm_final), tm=ts,
                    ff_chunk=ff_chunk, row_tokens=1, name="ffn_final_sample")

    st = lambda s, n: s.reshape(1, n, groups, n_p)
    return (yp.reshape(batch, seq, d), ys.reshape(dec_batch, t_new, d),
            k_p.reshape(1, batch, seq, heads, HEAD_DIM), vb_p.reshape(1, batch, seq, heads, HEAD_DIM),
            k_s.reshape(1, dec_batch, t_new, heads, HEAD_DIM),
            vb_s.reshape(1, dec_batch, t_new, heads, HEAD_DIM),
            vn_s.reshape(1, dec_batch, t_new, a_width),
            st(p_re, batch), st(p_im, batch), st(s_re, dec_batch), st(s_im, dec_batch))
```

```python
import functools
import math

import jax
import jax.numpy as jnp
from jax import lax
from jax.experimental import pallas as pl
from jax.experimental.pallas import tpu as pltpu

F32 = jnp.float32
BF16 = jnp.bfloat16

EPS = 1e-6
LANES = 128
SUBLANES = 8
V7X_VMEM_BYTES = 64 * 1024 * 1024
VMEM_LIMIT = V7X_VMEM_BYTES * 7 // 8
MASK_VALUE = -0.7 * float(jnp.finfo(jnp.float32).max)

CHUNK = 128
A_GROUPS = 4
HEAD_DIM = 128
QK_HALF = 64
LOG2E = math.log2(math.e)
SSM_GROUP = 16
SLAB_GROUPS = LANES // SSM_GROUP
SCAN_SHIFTS = (1, 2, 4)
ROW_TOKENS = 4
S5_BLOCK_ROWS = 128


def _const_spec(shape):
    zeros = (0,) * len(shape)
    return pl.BlockSpec(shape, lambda *_: zeros, pipeline_mode=pl.Buffered(1))


def _layer_spec(shape, layer):
    index = (layer,) + (0,) * (len(shape) - 1)
    return pl.BlockSpec((None,) + tuple(shape[1:]), lambda *_: index, pipeline_mode=pl.Buffered(1))


def _rms(x, g):
    return x * lax.rsqrt(jnp.mean(x * x, axis=-1, keepdims=True) + EPS) * g


def _params(semantics):
    return pltpu.CompilerParams(dimension_semantics=semantics, vmem_limit_bytes=VMEM_LIMIT)


def _even_in_kernel(x_ref, g_ref, w_ref, sn_ref, mix_ref, bias_ref, *outs, prompt, a_width):
    if prompt:
        a_ref, q_ref, k_ref, vb_ref, kb_ref, vbb_ref = outs
    else:
        a_ref, q_ref, k_ref, vb_ref, vn_ref = outs
    tm = x_ref.shape[0]
    rows_per_mix = mix_ref.shape[1]
    gw = a_width // A_GROUPS
    h = _rms(x_ref[...], g_ref[...]).astype(BF16)

    def proj(c0, width):
        return jnp.dot(h, w_ref[:, c0:c0 + width], preferred_element_type=F32)

    u = jax.nn.gelu(proj(0, a_width))
    v = _rms(jax.nn.gelu(proj(a_width, a_width)), sn_ref[...])
    if not prompt:
        vn_ref[...] = v
    v16 = v.astype(BF16)
    o = 2 * a_width
    bw = q_ref.shape[1]
    q_ref[...] = (proj(o, bw) * (QK_HALF ** -0.5 * LOG2E)).astype(q_ref.dtype)
    k = proj(o + bw, bw)
    vb = proj(o + 2 * bw, bw)
    heads = bw // HEAD_DIM
    for hd in range(heads):
        k_ref[pl.ds(hd, tm, stride=heads), :] = k[:, hd * HEAD_DIM:(hd + 1) * HEAD_DIM]
        vb_ref[pl.ds(hd, tm, stride=heads), :] = vb[:, hd * HEAD_DIM:(hd + 1) * HEAD_DIM]
    if prompt:
        kb_ref[...] = k.astype(BF16)
        vbb_ref[...] = vb.astype(BF16)
    for c in range(tm // rows_per_mix):
        r0 = c * rows_per_mix
        for g in range(A_GROUPS):
            c0 = g * gw
            gate = jnp.dot(mix_ref[g], v16[r0:r0 + rows_per_mix, c0:c0 + gw],
                           preferred_element_type=F32) + bias_ref[:, c0:c0 + gw]
            a_ref[r0:r0 + rows_per_mix, c0:c0 + gw] = (
                u[r0:r0 + rows_per_mix, c0:c0 + gw] * gate).astype(a_ref.dtype)


def _even_in(x, g, w, sn, mix, bias, *, tm, prompt):
    t, d = x.shape
    a_width = sn.shape[1]
    bw = (w.shape[1] - 2 * a_width) // 3
    act = BF16 if prompt else F32
    row = lambda width: pl.BlockSpec((tm, width), lambda i: (i, 0))
    heads = bw // HEAD_DIM
    head_rows = pl.BlockSpec((tm * heads, HEAD_DIM), lambda i: (i, 0))
    out_shape = [jax.ShapeDtypeStruct((t, a_width), act), jax.ShapeDtypeStruct((t, bw), act),
                 jax.ShapeDtypeStruct((t * heads, HEAD_DIM), F32),
                 jax.ShapeDtypeStruct((t * heads, HEAD_DIM), F32)]
    out_specs = [row(a_width), row(bw), head_rows, head_rows]
    if prompt:
        out_shape += [jax.ShapeDtypeStruct((t, bw), BF16)] * 2
        out_specs += [row(bw), row(bw)]
    else:
        out_shape += [jax.ShapeDtypeStruct((t, a_width), F32)]
        out_specs += [row(a_width)]
    return pl.pallas_call(
        functools.partial(_even_in_kernel, prompt=prompt, a_width=a_width),
        grid=(t // tm,),
        in_specs=[row(d), _const_spec(g.shape), _const_spec(w.shape), _const_spec(sn.shape),
                  _const_spec(mix.shape), _const_spec(bias.shape)],
        out_specs=out_specs,
        out_shape=out_shape,
        compiler_params=_params(("parallel",)),
        name="even_in_prompt" if prompt else "even_in_sample",
    )(x, g, w, sn, mix, bias)


def _split_halves(q):
    lane = lax.broadcasted_iota(jnp.int32, q.shape, 1)
    zero = jnp.zeros_like(q)
    return jnp.where(lane < QK_HALF, q, zero), jnp.where(lane >= QK_HALF, q, zero)


def _softmax_step(s, vblk, m_ref, l_ref, acc_ref, rows):
    tiles = [s[:, t * LANES:(t + 1) * LANES] for t in range(s.shape[1] // LANES)]
    m_prev = m_ref[rows, :]
    m_next = jnp.maximum(
        m_prev, jnp.max(functools.reduce(jnp.maximum, tiles), axis=1, keepdims=True))
    p_tiles = [jnp.exp2(t - m_next) for t in tiles]
    alpha = jnp.exp2(m_prev - m_next)
    l_ref[rows, :] = alpha * l_ref[rows, :] + functools.reduce(jnp.add, p_tiles)
    p = jnp.concatenate([t.astype(BF16) for t in p_tiles], axis=1)
    acc_ref[rows, :] = alpha * acc_ref[rows, :] + jnp.dot(p, vblk, preferred_element_type=F32)
    m_ref[rows, :] = m_next


def _row_sum(l):
    return jnp.sum(l, axis=1, keepdims=True)


def _diff_lambda(lq1, lk1, lq2, lk2, lam_init):
    e1 = jnp.exp(jnp.sum(lq1[...] * lk1[...], axis=1, keepdims=True))
    e2 = jnp.exp(jnp.sum(lq2[...] * lk2[...], axis=1, keepdims=True))
    return e1 - e2 + lam_init


def _head_out(o1, o2, lam, sub, lam_init):
    o = o1 - lam * o2
    return _rms(o, sub) * (1.0 - lam_init)


def _attn_prompt_kernel(q_ref, k_ref, v_ref, lq1, lk1, lq2, lk2, sub_ref, o_ref,
                        qs_ref, m_ref, l_ref, acc_ref, *, lam_init):
    tq = q_ref.shape[0]
    i = pl.program_id(2)
    q1, q2 = _split_halves(q_ref[...])
    qs_ref[0:tq, :] = q1
    qs_ref[tq:2 * tq, :] = q2
    m_ref[...] = jnp.full(m_ref.shape, MASK_VALUE, F32)
    l_ref[...] = jnp.zeros(l_ref.shape, F32)
    acc_ref[...] = jnp.zeros(acc_ref.shape, F32)
    all_rows = slice(0, 2 * tq)

    def scores(j):
        kv_rows = pl.ds(pl.multiple_of(j * tq, tq), tq)
        s = lax.dot_general(qs_ref[...], k_ref[kv_rows, :], (((1,), (1,)), ((), ())),
                            preferred_element_type=F32)
        return s, v_ref[kv_rows, :]

    def full_block(j, carry):
        s, vblk = scores(j)
        _softmax_step(s, vblk, m_ref, l_ref, acc_ref, all_rows)
        return carry

    lax.fori_loop(0, i, full_block, 0)
    s, vblk = scores(i)
    row = lax.broadcasted_iota(jnp.int32, s.shape, 0)
    col = lax.broadcasted_iota(jnp.int32, s.shape, 1)
    qpos = jnp.where(row >= tq, row - tq, row)
    s = jnp.where(qpos >= col, s, MASK_VALUE)
    _softmax_step(s, vblk, m_ref, l_ref, acc_ref, all_rows)

    lam = _diff_lambda(lq1, lk1, lq2, lk2, lam_init)
    o1 = acc_ref[0:tq, :] / _row_sum(l_ref[0:tq, :])
    o2 = acc_ref[tq:2 * tq, :] / _row_sum(l_ref[tq:2 * tq, :])
    o_ref[...] = _head_out(o1, o2, lam, sub_ref[...], lam_init).astype(o_ref.dtype)


def _attn_prompt(q, kb, vb, lq1, lk1, lq2, lk2, sub, *, batch, seq, tq, lam_init):
    t, bw = q.shape
    heads = bw // HEAD_DIM
    nq = seq // tq
    vec = lambda a: _const_spec(a.shape)
    return pl.pallas_call(
        functools.partial(_attn_prompt_kernel, lam_init=lam_init),
        grid=(batch, heads, nq),
        in_specs=[pl.BlockSpec((tq, HEAD_DIM), lambda b, h, i: (b * nq + i, h)),
                  pl.BlockSpec((seq, HEAD_DIM), lambda b, h, i: (b, h)),
                  pl.BlockSpec((seq, HEAD_DIM), lambda b, h, i: (b, h)),
                  vec(lq1), vec(lk1), vec(lq2), vec(lk2), vec(sub)],
        out_specs=pl.BlockSpec((tq, HEAD_DIM), lambda b, h, i: (b * nq + i, h)),
        out_shape=jax.ShapeDtypeStruct((t, bw), BF16),
        scratch_shapes=[pltpu.VMEM((2 * tq, HEAD_DIM), BF16),
                        pltpu.VMEM((2 * tq, LANES), F32),
                        pltpu.VMEM((2 * tq, LANES), F32),
                        pltpu.VMEM((2 * tq, HEAD_DIM), F32)],
        compiler_params=_params(("parallel", "parallel", "arbitrary")),
        name="attn_prompt",
    )(q, kb, vb, lq1, lk1, lq2, lk2, sub)


def _attn_sample_kernel(pt_ref, q_ref, kn_ref, vn_ref, *rest, pages_per_step, heads, lam_init):
    del pt_ref
    k_pages = rest[:pages_per_step]
    v_pages = rest[pages_per_step:2 * pages_per_step]
    (lq1, lk1, lq2, lk2, sub_ref, o_ref,
     qs_ref, bias_ref, m_ref, l_ref, acc_ref) = rest[2 * pages_per_step:]
    j = pl.program_id(1)
    t_new = q_ref.shape[0]
    rows_per_head = 2 * t_new
    all_rows = slice(0, heads * rows_per_head)

    def head_match(shape):
        row = lax.broadcasted_iota(jnp.int32, shape, 0)
        col = lax.broadcasted_iota(jnp.int32, shape, 1)
        return row, col, lax.rem(col, heads) == row // rows_per_head

    @pl.when(j == 0)
    def _():
        pieces = []
        for h in range(heads):
            pieces += _split_halves(q_ref[:, h * HEAD_DIM:(h + 1) * HEAD_DIM])
        qs_ref[...] = jnp.concatenate(pieces, axis=0).astype(BF16)
        _, _, ok = head_match(bias_ref.shape)
        bias_ref[...] = jnp.where(ok, 0.0, MASK_VALUE)
        m_ref[...] = jnp.full(m_ref.shape, MASK_VALUE, F32)
        l_ref[...] = jnp.zeros(l_ref.shape, F32)
        acc_ref[...] = jnp.zeros(acc_ref.shape, F32)

    def scores(kblk):
        return lax.dot_general(qs_ref[...], kblk.astype(BF16), (((1,), (1,)), ((), ())),
                               preferred_element_type=F32)

    kblk = jnp.concatenate([k_pages[p][...] for p in range(pages_per_step)], axis=0)
    vblk = jnp.concatenate([v_pages[p][...] for p in range(pages_per_step)], axis=0)
    _softmax_step(scores(kblk) + bias_ref[...], vblk.astype(BF16), m_ref, l_ref, acc_ref, all_rows)

    @pl.when(j == pl.num_programs(1) - 1)
    def _():
        pad = jnp.zeros((LANES - kn_ref.shape[0], HEAD_DIM), F32)
        s = scores(jnp.concatenate([kn_ref[...], pad], axis=0))
        row, col, ok = head_match(s.shape)
        s = jnp.where(ok, s, MASK_VALUE)
        s = jnp.where(col // heads <= lax.rem(row, t_new), s, MASK_VALUE)
        vblk = jnp.concatenate([vn_ref[...], pad], axis=0)
        _softmax_step(s, vblk.astype(BF16), m_ref, l_ref, acc_ref, all_rows)
        lam = _diff_lambda(lq1, lk1, lq2, lk2, lam_init)
        for h in range(heads):
            r0 = h * rows_per_head
            o1 = acc_ref[r0:r0 + t_new, :] / _row_sum(l_ref[r0:r0 + t_new, :])
            o2 = (acc_ref[r0 + t_new:r0 + rows_per_head, :]
                  / _row_sum(l_ref[r0 + t_new:r0 + rows_per_head, :]))
            o_ref[:, h * HEAD_DIM:(h + 1) * HEAD_DIM] = _head_out(
                o1, o2, lam, sub_ref[...], lam_init)


def _attn_sample(page_table, q, k_new, v_new, cache_k, cache_v, layer, lq1, lk1, lq2, lk2, sub,
                 *, t_new, pages_per_step, lam_init):
    t, bw = q.shape
    heads = bw // HEAD_DIM
    dec_batch, n_pages = page_table.shape
    page_rows = cache_k.shape[2]
    steps = n_pages // pages_per_step
    pt_flat = page_table.reshape(-1)

    def page_spec(p):
        return pl.BlockSpec(
            (None, None, page_rows, HEAD_DIM),
            lambda b, j, pt: (layer, pt[b * n_pages + j * pages_per_step + p], 0, 0))

    q_spec = pl.BlockSpec((t_new, bw), lambda b, j, pt: (b, 0))
    new_spec = pl.BlockSpec((t_new * heads, HEAD_DIM), lambda b, j, pt: (b, 0))
    vec = lambda a: pl.BlockSpec(a.shape, lambda b, j, pt: (0, 0))
    grid_spec = pltpu.PrefetchScalarGridSpec(
        num_scalar_prefetch=1,
        grid=(dec_batch, steps),
        in_specs=([q_spec, new_spec, new_spec]
                  + [page_spec(p) for p in range(pages_per_step)] * 2
                  + [vec(lq1), vec(lk1), vec(lq2), vec(lk2), vec(sub)]),
        out_specs=q_spec,
        scratch_shapes=[pltpu.VMEM((heads * 2 * t_new, HEAD_DIM), BF16),
                        pltpu.VMEM((heads * 2 * t_new, pages_per_step * page_rows), F32),
                        pltpu.VMEM((heads * 2 * t_new, LANES), F32),
                        pltpu.VMEM((heads * 2 * t_new, LANES), F32),
                        pltpu.VMEM((heads * 2 * t_new, HEAD_DIM), F32)])
    return pl.pallas_call(
        functools.partial(_attn_sample_kernel, pages_per_step=pages_per_step, heads=heads,
                          lam_init=lam_init),
        grid_spec=grid_spec,
        out_shape=jax.ShapeDtypeStruct((t, bw), F32),
        compiler_params=_params(("parallel", "arbitrary")),
        name="attn_sample",
    )(pt_flat, q, k_new, v_new, *([cache_k] * pages_per_step), *([cache_v] * pages_per_step),
      lq1, lk1, lq2, lk2, sub)


def _swiglu_residual(x1, g_ref, w_in_ref, w_out_ref, ff_chunk):
    d_ff = w_out_ref.shape[0]
    h = _rms(x1, g_ref[...]).astype(BF16)
    acc = x1
    for c0 in range(0, d_ff, ff_chunk):
        gate = jnp.dot(h, w_in_ref[:, c0:c0 + ff_chunk], preferred_element_type=F32)
        up = jnp.dot(h, w_in_ref[:, d_ff + c0:d_ff + c0 + ff_chunk], preferred_element_type=F32)
        act = (gate * jax.nn.sigmoid(gate) * up).astype(BF16)
        acc = acc + jnp.dot(act, w_out_ref[c0:c0 + ff_chunk, :], preferred_element_type=F32)
    return acc


def _mix_ffn_kernel(x_ref, a_ref, b_ref, wo_ref, g_ref, w_in_ref, w_out_ref, o_ref, *scratch,
                    ff_chunk, row_tokens):
    tm, d = x_ref.shape
    ab = jnp.concatenate([a_ref[...].astype(BF16), b_ref[...].astype(BF16)], axis=1)
    x1 = x_ref[...] + jnp.dot(ab, wo_ref[...], preferred_element_type=F32)
    y = _swiglu_residual(x1, g_ref, w_in_ref, w_out_ref, ff_chunk)
    if row_tokens == 1:
        o_ref[...] = y
        return
    slabs = scratch[0]
    for j in range(d // LANES):
        slabs[j] = y[:, j * LANES:(j + 1) * LANES]
    for t in range(row_tokens):
        for j in range(d // LANES):
            o_ref[:, t * d + j * LANES:t * d + (j + 1) * LANES] = (
                slabs[j, pl.ds(t, tm // row_tokens, stride=row_tokens), :])


def _ffn_final_kernel(x_ref, g_ref, w_in_ref, w_out_ref, gf_ref, o_ref, *scratch,
                      ff_chunk, row_tokens):
    if row_tokens == 1:
        x = x_ref[...]
    else:
        r = x_ref.shape[0]
        d = x_ref.shape[1] // row_tokens
        slabs = scratch[0]
        for t in range(row_tokens):
            for j in range(d // LANES):
                slabs[j, pl.ds(t, r, stride=row_tokens), :] = (
                    x_ref[:, t * d + j * LANES:t * d + (j + 1) * LANES])
        x = jnp.concatenate([slabs[j] for j in range(d // LANES)], axis=1)
    o_ref[...] = _rms(_swiglu_residual(x, g_ref, w_in_ref, w_out_ref, ff_chunk), gf_ref[...])


def _mix_ffn(x, a, b, wo, g, w_in, w_out, *, layer, tm, ff_chunk, row_tokens, name):
    t, d = x.shape
    row = lambda arr: pl.BlockSpec((tm, arr.shape[1]), lambda i: (i, 0))
    packed = row_tokens > 1
    return pl.pallas_call(
        functools.partial(_mix_ffn_kernel, ff_chunk=ff_chunk, row_tokens=row_tokens),
        grid=(t // tm,),
        in_specs=[row(x), row(a), row(b), _const_spec(wo.shape), _const_spec(g.shape),
                  _layer_spec(w_in.shape, layer), _layer_spec(w_out.shape, layer)],
        out_specs=pl.BlockSpec((tm // row_tokens, row_tokens * d), lambda i: (i, 0)),
        out_shape=jax.ShapeDtypeStruct((t // row_tokens, row_tokens * d), F32),
        scratch_shapes=[pltpu.VMEM((d // LANES, tm, LANES), F32)] if packed else [],
        compiler_params=_params(("parallel",)),
        name=name,
    )(x, a, b, wo, g, w_in, w_out)


def _ffn_final(x, g, w_in, w_out, gf, *, layer, tm, ff_chunk, row_tokens, name):
    rows, width = x.shape
    d = width // row_tokens
    t = rows * row_tokens
    packed = row_tokens > 1
    return pl.pallas_call(
        functools.partial(_ffn_final_kernel, ff_chunk=ff_chunk, row_tokens=row_tokens),
        grid=(t // tm,),
        in_specs=[pl.BlockSpec((tm // row_tokens, width), lambda i: (i, 0)),
                  _const_spec(g.shape), _layer_spec(w_in.shape, layer),
                  _layer_spec(w_out.shape, layer), _const_spec(gf.shape)],
        out_specs=pl.BlockSpec((tm, d), lambda i: (i, 0)),
        out_shape=jax.ShapeDtypeStruct((t, d), F32),
        scratch_shapes=[pltpu.VMEM((d // LANES, tm, LANES), F32)] if packed else [],
        compiler_params=_params(("parallel",)),
        name=name,
    )(x, g, w_in, w_out, gf)


def _s5_kernel(x_ref, g_ref, s0re_ref, s0im_ref, coef_ref, bblk_ref, cblk_ref, d_ref,
               wglu_ref, bglu_ref, o_ref, sre_ref, sim_ref,
               re_s, im_s, gel_s, car_re, car_im):
    tl, d = x_ref.shape
    n_slabs = d // LANES
    slab_state = re_s.shape[1] // n_slabs

    @pl.when(pl.program_id(1) == 0)
    def _():
        car_re[...] = s0re_ref[...]
        car_im[...] = s0im_ref[...]

    x = x_ref[...]
    h = _rms(x, g_ref[...])
    hb = h.astype(BF16)
    for j in range(n_slabs):
        bu = jnp.dot(hb[:, j * LANES:(j + 1) * LANES], bblk_ref[j], preferred_element_type=F32)
        re_s[:, j * slab_state:(j + 1) * slab_state] = bu[:, :slab_state]
        im_s[:, j * slab_state:(j + 1) * slab_state] = bu[:, slab_state:]

    def row_tile(r, carry):
        cr, ci = carry
        rows = pl.ds(pl.multiple_of(r * SUBLANES, SUBLANES), SUBLANES)
        wr = re_s[rows, :]
        wi = im_s[rows, :]
        for k, shift in enumerate(SCAN_SHIFTS):
            ar = coef_ref[2 * k]
            ai = coef_ref[2 * k + 1]
            sr = pltpu.roll(wr, shift, 0)
            si = pltpu.roll(wi, shift, 0)
            wr, wi = wr + ar * sr - ai * si, wi + ar * si + ai * sr
        pr = coef_ref[2 * len(SCAN_SHIFTS)]
        pi = coef_ref[2 * len(SCAN_SHIFTS) + 1]
        wr, wi = wr + pr * cr - pi * ci, wi + pr * ci + pi * cr
        re_s[rows, :] = wr
        im_s[rows, :] = wi
        return wr[SUBLANES - 1:SUBLANES, :], wi[SUBLANES - 1:SUBLANES, :]

    cr, ci = lax.fori_loop(0, tl // SUBLANES, row_tile, (car_re[...], car_im[...]))
    car_re[...] = cr
    car_im[...] = ci
    sre_ref[...] = cr
    sim_ref[...] = ci

    for j in range(n_slabs):
        cols = slice(j * LANES, (j + 1) * LANES)
        st = slice(j * slab_state, (j + 1) * slab_state)
        s_cat = jnp.concatenate([re_s[:, st], im_s[:, st]], axis=1).astype(BF16)
        y = jnp.dot(s_cat, cblk_ref[j], preferred_element_type=F32) + d_ref[:, cols] * h[:, cols]
        gel_s[:, cols] = jax.nn.gelu(y).astype(BF16)
    z = jnp.dot(gel_s[...], wglu_ref[...], preferred_element_type=F32) + bglu_ref[...]
    o_ref[...] = x + z[:, :d] * jax.nn.sigmoid(z[:, d:])


def _s5_mix(x, g, s0re, s0im, coef, bblk, cblk, dskip, wglu, bglu, *, batch, seq, tl, name):
    t, d = x.shape
    nt = seq // tl
    n_state = s0re.shape[-1]
    state_spec = pl.BlockSpec((None, 1, n_state), lambda b, i: (b, 0, 0))
    x1, sre, sim = pl.pallas_call(
        _s5_kernel,
        grid=(batch, nt),
        in_specs=[pl.BlockSpec((tl, d), lambda b, i: (b * nt + i, 0)), _const_spec(g.shape),
                  state_spec, state_spec, _const_spec(coef.shape), _const_spec(bblk.shape),
                  _const_spec(cblk.shape), _const_spec(dskip.shape), _const_spec(wglu.shape),
                  _const_spec(bglu.shape)],
        out_specs=[pl.BlockSpec((tl, d), lambda b, i: (b * nt + i, 0)), state_spec, state_spec],
        out_shape=[jax.ShapeDtypeStruct((t, d), F32),
                   jax.ShapeDtypeStruct((batch, 1, n_state), F32),
                   jax.ShapeDtypeStruct((batch, 1, n_state), F32)],
        scratch_shapes=[pltpu.VMEM((tl, n_state), F32), pltpu.VMEM((tl, n_state), F32),
                        pltpu.VMEM((tl, d), BF16),
                        pltpu.VMEM((1, n_state), F32), pltpu.VMEM((1, n_state), F32)],
        compiler_params=_params(("parallel", "arbitrary")),
        name=name,
    )(x, g, s0re, s0im, coef, bblk, cblk, dskip, wglu, bglu)
    return x1, sre, sim


def _s5_tables(a_re, a_im, log_dt, b_re, b_im, c_re, c_im):
    groups, n_p = a_re.shape
    a = lax.complex(a_re, a_im)
    lam_dt = a * jnp.exp(log_dt)[:, None]
    a_bar = jnp.exp(lam_dt)
    b_bar = ((a_bar - 1.0) / a)[:, :, None] * lax.complex(b_re, b_im)
    rows = jnp.arange(SUBLANES)
    coef = []
    for shift in SCAN_SHIFTS:
        p = jnp.exp(lam_dt * shift).reshape(1, -1) * (rows >= shift)[:, None]
        coef += [jnp.real(p), jnp.imag(p)]
    p = jnp.exp(lam_dt.reshape(1, -1) * (rows + 1)[:, None].astype(F32))
    coef += [jnp.real(p), jnp.imag(p)]
    coef = jnp.stack(coef).astype(F32)
    n_slabs = groups // SLAB_GROUPS
    bb = b_bar.reshape(n_slabs, SLAB_GROUPS, n_p, SSM_GROUP).transpose(0, 1, 3, 2)
    bblk = _expand_groups(jnp.stack([jnp.real(bb), jnp.imag(bb)], axis=3)[:, None])
    cc = lax.complex(c_re, c_im).reshape(n_slabs, SLAB_GROUPS, SSM_GROUP, n_p).transpose(0, 1, 3, 2)
    cblk = _expand_groups(jnp.stack([jnp.real(cc), -jnp.imag(cc)], axis=1)[:, :, :, :, None])
    return coef, bblk, cblk


def _expand_groups(m):
    n_slabs, r0, g8, r1, c0, c1 = m.shape
    full = jnp.broadcast_to(m[:, :, :, :, :, None, :], (n_slabs, r0, g8, r1, c0, g8, c1))
    full = full.reshape(n_slabs, r0 * g8 * r1, c0 * g8 * c1)
    row_g = (lax.broadcasted_iota(jnp.int32, full.shape, 1) // r1) % g8
    col_h = (lax.broadcasted_iota(jnp.int32, full.shape, 2) // c1) % g8
    return jnp.where(row_g == col_h, full, 0.0).astype(BF16)


def _s5_rows_kernel(x_ref, g_ref, s0re_ref, s0im_ref, coef_ref, wb_ref, wy_ref, wc_ref, d_ref,
                    wglu_ref, bglu_ref, o_ref, sre_ref, sim_ref,
                    re_s, im_s, gel_s, car_re, car_im, *, row_tokens, n_seq):
    r, width = x_ref.shape
    d = width // row_tokens
    n_slabs = d // LANES
    slab_state = re_s.shape[1] // n_slabs

    xs = [x_ref[:, t * d:(t + 1) * d] for t in range(row_tokens)]
    hs = [_rms(x, g_ref[...]) for x in xs]
    hb = [h.astype(BF16) for h in hs]

    def slab_lhs(j):
        return jnp.concatenate([h[:, j * LANES:(j + 1) * LANES] for h in hb], axis=1)

    for j in range(n_slabs):
        w = jnp.dot(slab_lhs(j), wb_ref[j], preferred_element_type=F32)
        re_s[:, j * slab_state:(j + 1) * slab_state] = w[:, :slab_state]
        im_s[:, j * slab_state:(j + 1) * slab_state] = w[:, slab_state:]

    def row_tile(i, carry):
        cr, ci = carry
        rows = pl.ds(pl.multiple_of(i * SUBLANES, SUBLANES), SUBLANES)
        wr = re_s[rows, :]
        wi = im_s[rows, :]
        for k, shift in enumerate(SCAN_SHIFTS):
            ar = coef_ref[2 * k]
            ai = coef_ref[2 * k + 1]
            sr = pltpu.roll(wr, shift, 0)
            si = pltpu.roll(wi, shift, 0)
            wr, wi = wr + ar * sr - ai * si, wi + ar * si + ai * sr
        pr = coef_ref[2 * len(SCAN_SHIFTS)]
        pi = coef_ref[2 * len(SCAN_SHIFTS) + 1]
        wr, wi = wr + pr * cr - pi * ci, wi + pr * ci + pi * cr
        first = lax.broadcasted_iota(jnp.int32, wr.shape, 0) == 0
        re_s[rows, :] = jnp.where(first, cr, pltpu.roll(wr, 1, 0))
        im_s[rows, :] = jnp.where(first, ci, pltpu.roll(wi, 1, 0))
        return wr[SUBLANES - 1:SUBLANES, :], wi[SUBLANES - 1:SUBLANES, :]

    if n_seq == 1:
        @pl.when(pl.program_id(1) == 0)
        def _():
            car_re[...] = s0re_ref[...]
            car_im[...] = s0im_ref[...]

        cr, ci = lax.fori_loop(0, r // SUBLANES, row_tile, (car_re[...], car_im[...]))
        car_re[...] = cr
        car_im[...] = ci
    else:
        a_re = coef_ref[0][1:2, :]
        a_im = coef_ref[1][1:2, :]
        cr, ci = s0re_ref[...], s0im_ref[...]
        for n in range(r // n_seq):
            rows = slice(n * n_seq, (n + 1) * n_seq)
            wr, wi = re_s[rows, :], im_s[rows, :]
            re_s[rows, :] = cr
            im_s[rows, :] = ci
            cr, ci = a_re * cr - a_im * ci + wr, a_re * ci + a_im * cr + wi
    sre_ref[...] = cr
    sim_ref[...] = ci

    for j in range(n_slabs):
        cols = slice(j * LANES, (j + 1) * LANES)
        st = slice(j * slab_state, (j + 1) * slab_state)
        s_cat = jnp.concatenate([re_s[:, st], im_s[:, st]], axis=1).astype(BF16)
        y = (jnp.dot(slab_lhs(j), wy_ref[j], preferred_element_type=F32)
             + jnp.dot(s_cat, wc_ref[j], preferred_element_type=F32))
        for t in range(row_tokens):
            yt = y[:, t * LANES:(t + 1) * LANES] + d_ref[:, cols] * hs[t][:, cols]
            gel_s[t * r:(t + 1) * r, cols] = jax.nn.gelu(yt).astype(BF16)
    z = jnp.dot(gel_s[...], wglu_ref[...], preferred_element_type=F32) + bglu_ref[...]
    for t in range(row_tokens):
        zt = z[t * r:(t + 1) * r, :]
        o_ref[:, t * d:(t + 1) * d] = xs[t] + zt[:, :d] * jax.nn.sigmoid(zt[:, d:])


def _s5_rows_mix(x, g, s0re, s0im, coef, wb, wy, wc, dskip, wglu, bglu, *, batch, rows_per_seq,
                 r, row_tokens, step_major, name):
    rows, width = x.shape
    d = width // row_tokens
    n_state = s0re.shape[-1]
    if step_major:
        grid = (1, 1)
        r = rows
        state_spec = pl.BlockSpec((batch, n_state), lambda b, i: (0, 0))
        x_spec = pl.BlockSpec((rows, width), lambda b, i: (0, 0))
    else:
        nt = rows_per_seq // r
        grid = (batch, nt)
        state_spec = pl.BlockSpec((None, 1, n_state), lambda b, i: (b, 0, 0))
        x_spec = pl.BlockSpec((r, width), lambda b, i: (b * nt + i, 0))
    consts = (g, coef, wb, wy, wc, dskip, wglu, bglu)
    x1, sre, sim = pl.pallas_call(
        functools.partial(_s5_rows_kernel, row_tokens=row_tokens,
                          n_seq=batch if step_major else 1),
        grid=grid,
        in_specs=[x_spec, _const_spec(g.shape), state_spec, state_spec]
                 + [_const_spec(c.shape) for c in consts[1:]],
        out_specs=[x_spec, state_spec, state_spec],
        out_shape=[jax.ShapeDtypeStruct((rows, width), F32),
                   jax.ShapeDtypeStruct(s0re.shape, F32),
                   jax.ShapeDtypeStruct(s0im.shape, F32)],
        scratch_shapes=[pltpu.VMEM((r, n_state), F32), pltpu.VMEM((r, n_state), F32),
                        pltpu.VMEM((r * row_tokens, d), BF16),
                        pltpu.VMEM((1, n_state), F32), pltpu.VMEM((1, n_state), F32)],
        compiler_params=_params(("parallel", "arbitrary")),
        name=name,
    )(x, g, s0re, s0im, *consts[1:])
    return x1, sre, sim


def _s5_row_tables(a_re, a_im, log_dt, b_re, b_im, c_re, c_im, row_tokens):
    groups, n_p = a_re.shape
    p_tok = row_tokens
    hi = lax.Precision.HIGHEST
    a = lax.complex(a_re, a_im)
    lam_dt = a * jnp.exp(log_dt)[:, None]
    power = lambda k: jnp.exp(lam_dt * k)
    b_bar = ((power(1) - 1.0) / a)[:, :, None] * lax.complex(b_re, b_im)
    c = lax.complex(c_re, c_im)
    rows = jnp.arange(SUBLANES)
    coef = []
    for shift in SCAN_SHIFTS:
        pw = power(p_tok * shift).reshape(1, -1) * (rows >= shift)[:, None]
        coef += [jnp.real(pw), jnp.imag(pw)]
    pw = jnp.exp(lam_dt.reshape(1, -1) * (p_tok * (rows + 1))[:, None].astype(F32))
    coef += [jnp.real(pw), jnp.imag(pw)]
    coef = jnp.stack(coef).astype(F32)
    n_slabs = groups // SLAB_GROUPS
    slabbed = lambda m, lead: m.reshape(m.shape[:lead] + (n_slabs, SLAB_GROUPS) + m.shape[lead + 1:])

    e = jnp.stack([power(p_tok - 1 - s)[:, :, None] * b_bar for s in range(p_tok)])
    e = slabbed(e, 1).transpose(1, 0, 2, 4, 3)
    wb = _expand_groups(jnp.stack([jnp.real(e), jnp.imag(e)], axis=4))

    ct = jnp.stack([c * power(t + 1)[:, None, :] for t in range(p_tok)])
    ct = slabbed(ct, 1).transpose(1, 2, 4, 0, 3)
    wc = _expand_groups(jnp.stack([jnp.real(ct), -jnp.imag(ct)], axis=1))

    lag = [jnp.real(jnp.einsum('gcp,gpd->gcd', c * power(l)[:, None, :], b_bar, precision=hi))
           for l in range(p_tok)]
    zero = jnp.zeros_like(lag[0])
    toep = jnp.stack([jnp.stack([lag[t - s] if t >= s else zero for t in range(p_tok)])
                      for s in range(p_tok)])
    wy = _expand_groups(slabbed(toep, 2).transpose(2, 0, 3, 5, 1, 4))
    return coef, wb, wy, wc


def kernel(x_prompt, x_sample, cache_k, cache_v, page_table, state_ssm_re, state_ssm_im,
           norm_mix, norm_ffn, norm_final, w_in_even, w_out_even, sgu_norm, sgu_w, sgu_b,
           lambda_q1, lambda_k1, lambda_q2, lambda_k2, attn_subln,
           ssm_a_re, ssm_a_im, ssm_log_dt, ssm_b_re, ssm_b_im, ssm_c_re, ssm_c_im, ssm_d,
           w_glu, b_glu, w_ffn_in, w_ffn_out):
    batch, seq, d = x_prompt.shape
    dec_batch, t_new, _ = x_sample.shape
    depth = norm_mix.shape[0]
    assert depth == 2 and seq % CHUNK == 0 and t_new == SUBLANES
    tp = batch * seq
    ts = dec_batch * t_new
    xp = x_prompt.reshape(tp, d)
    xs = x_sample.reshape(ts, d)
    tm_prompt = 512
    ff_chunk = 256
    row = lambda v: v.reshape(1, -1)

    lam_init = 0.8 - 0.6 * math.exp(-0.3 * 0)
    w_in0 = w_in_even[0].astype(BF16)
    w_out0 = w_out_even[0].astype(BF16)
    a_width = sgu_norm.shape[1]
    gw = a_width // A_GROUPS
    tril = jnp.tril(jnp.ones((CHUNK, CHUNK), bool))
    mix_p = jnp.where(tril, sgu_w[0], 0).astype(BF16)
    bias_p = jnp.repeat(sgu_b[0].T, gw, axis=1)
    small = jnp.where(tril[:t_new, :t_new], sgu_w[0][:, :t_new, :t_new], 0)
    mix_s = jnp.einsum('gts,bc->gbtcs', small, jnp.eye(dec_batch, dtype=F32)).reshape(
        A_GROUPS, ts, ts).astype(BF16)
    bias_s = jnp.tile(bias_p[:t_new], (dec_batch, 1))
    lam_vecs = [row(lambda_q1[0]), row(lambda_k1[0]), row(lambda_q2[0]), row(lambda_k2[0]),
                row(attn_subln[0])]

    a_p, q_p, k_p, vb_p, kb_p, vbb_p = _even_in(
        xp, row(norm_mix[0]), w_in0, row(sgu_norm[0]), mix_p, bias_p, tm=tm_prompt, prompt=True)
    a_s, q_s, k_s, vb_s, vn_s = _even_in(
        xs, row(norm_mix[0]), w_in0, row(sgu_norm[0]), mix_s, bias_s, tm=ts, prompt=False)
    b_p = _attn_prompt(q_p, kb_p, vbb_p, *lam_vecs, batch=batch, seq=seq, tq=512,
                       lam_init=lam_init)
    n_layers, n_phys, page_size, heads, _ = cache_k.shape
    page_view = (n_layers, n_phys, page_size * heads, HEAD_DIM)
    b_s = _attn_sample(page_table, q_s, k_s, vb_s, cache_k.reshape(page_view),
                       cache_v.reshape(page_view), 0, *lam_vecs,
                       t_new=t_new, pages_per_step=16, lam_init=lam_init)
    w_ffn_in_b = w_ffn_in.astype(BF16)
    w_ffn_out_b = w_ffn_out.astype(BF16)
    xp = _mix_ffn(xp, a_p, b_p, w_out0, row(norm_ffn[0]), w_ffn_in_b, w_ffn_out_b, layer=0,
                  tm=tm_prompt, ff_chunk=ff_chunk, row_tokens=ROW_TOKENS, name="mix_ffn_prompt")
    xs = _mix_ffn(xs, a_s, b_s, w_out0, row(norm_ffn[0]), w_ffn_in_b, w_ffn_out_b, layer=0,
                  tm=ts, ff_chunk=ff_chunk, row_tokens=1, name="mix_ffn_sample")

    groups, n_p = ssm_a_re.shape[1:]
    n_state = groups * n_p
    ssm = (ssm_a_re[0], ssm_a_im[0], ssm_log_dt[0], ssm_b_re[0], ssm_b_im[0], ssm_c_re[0],
           ssm_c_im[0])
    w_glu0 = w_glu[0].astype(BF16)
    s5_args = (*_s5_row_tables(*ssm, ROW_TOKENS), row(ssm_d[0]), w_glu0, row(b_glu[0]))
    zeros = jnp.zeros((batch, 1, n_state), F32)
    xp, p_re, p_im = _s5_rows_mix(xp, row(norm_mix[1]), zeros, zeros, *s5_args, batch=batch,
                                  rows_per_seq=seq // ROW_TOKENS, r=S5_BLOCK_ROWS,
                                  row_tokens=ROW_TOKENS, step_major=False, name="s5_prompt")
    steps = t_new // ROW_TOKENS
    pack = lambda v: v.reshape(dec_batch, steps, -1).transpose(1, 0, 2).reshape(
        steps * dec_batch, -1)
    unpack = lambda v: v.reshape(steps, dec_batch, -1).transpose(1, 0, 2).reshape(ts, d)
    xs, s_re, s_im = _s5_rows_mix(pack(xs), row(norm_mix[1]),
                                  state_ssm_re[0].reshape(dec_batch, n_state),
                                  state_ssm_im[0].reshape(dec_batch, n_state), *s5_args,
                                  batch=dec_batch, rows_per_seq=steps, r=None,
                                  row_tokens=ROW_TOKENS, step_major=True, name="s5_sample")
    xs = unpack(xs)
    yp = _ffn_final(xp, row(norm_ffn[1]), w_ffn_in_b, w_ffn_out_b, row(norm_final), layer=1,
                    tm=tm_prompt, ff_chunk=ff_chunk, row_tokens=ROW_TOKENS,
                    name="ffn_final_prompt")
    ys = _ffn_final(xs, row(norm_ffn[1]), w_ffn_in_b, w_ffn_out_b, row(norm_final), layer=1,
                    tm=ts, ff_chunk=ff_chunk, row_tokens=1, name="ffn_final_sample")

    st = lambda s, n: s.reshape(1, n, groups, n_p)
    return (yp.reshape(batch, seq, d), ys.reshape(dec_batch, t_new, d),
            k_p.reshape(1, batch, seq, heads, HEAD_DIM), vb_p.reshape(1, batch, seq, heads, HEAD_DIM),
            k_s.reshape(1, dec_batch, t_new, heads, HEAD_DIM),
            vb_s.reshape(1, dec_batch, t_new, heads, HEAD_DIM),
            vn_s.reshape(1, dec_batch, t_new, a_width),
            st(p_re, batch), st(p_im, batch), st(s_re, dec_batch), st(s_im, dec_batch))
```

```python
import functools
import math

import jax
import jax.numpy as jnp
from jax import lax
from jax.experimental import pallas as pl
from jax.experimental.pallas import tpu as pltpu

F32 = jnp.float32
BF16 = jnp.bfloat16

EPS = 1e-6
LANES = 128
SUBLANES = 8
V7X_VMEM_BYTES = 64 * 1024 * 1024
VMEM_LIMIT = V7X_VMEM_BYTES * 7 // 8
MASK_VALUE = -0.7 * float(jnp.finfo(jnp.float32).max)

CHUNK = 128
A_GROUPS = 4
HEAD_DIM = 128
QK_HALF = 64
LOG2E = math.log2(math.e)
SSM_GROUP = 16
SLAB_GROUPS = LANES // SSM_GROUP
SCAN_SHIFTS = (1, 2, 4)
ROW_TOKENS = 4
S5_BLOCK_ROWS = 128


def _const_spec(shape):
    zeros = (0,) * len(shape)
    return pl.BlockSpec(shape, lambda *_: zeros, pipeline_mode=pl.Buffered(1))


def _layer_spec(shape, layer):
    index = (layer,) + (0,) * (len(shape) - 1)
    return pl.BlockSpec((None,) + tuple(shape[1:]), lambda *_: index, pipeline_mode=pl.Buffered(1))


def _rms(x, g):
    return x * lax.rsqrt(jnp.mean(x * x, axis=-1, keepdims=True) + EPS) * g


def _params(semantics):
    return pltpu.CompilerParams(dimension_semantics=semantics, vmem_limit_bytes=VMEM_LIMIT)


def _even_in_kernel(x_ref, g_ref, w_ref, sn_ref, mix_ref, bias_ref, *outs, prompt, a_width):
    if prompt:
        a_ref, q_ref, k_ref, vb_ref, kb_ref, vbb_ref = outs
    else:
        a_ref, q_ref, k_ref, vb_ref, vn_ref = outs
    tm = x_ref.shape[0]
    rows_per_mix = mix_ref.shape[1]
    gw = a_width // A_GROUPS
    h = _rms(x_ref[...], g_ref[...]).astype(BF16)

    def proj(c0, width):
        return jnp.dot(h, w_ref[:, c0:c0 + width], preferred_element_type=F32)

    u = jax.nn.gelu(proj(0, a_width))
    v = _rms(jax.nn.gelu(proj(a_width, a_width)), sn_ref[...])
    if not prompt:
        vn_ref[...] = v
    v16 = v.astype(BF16)
    o = 2 * a_width
    bw = q_ref.shape[1]
    q_ref[...] = (proj(o, bw) * (QK_HALF ** -0.5 * LOG2E)).astype(q_ref.dtype)
    k = proj(o + bw, bw)
    vb = proj(o + 2 * bw, bw)
    heads = bw // HEAD_DIM
    for hd in range(heads):
        k_ref[pl.ds(hd, tm, stride=heads), :] = k[:, hd * HEAD_DIM:(hd + 1) * HEAD_DIM]
        vb_ref[pl.ds(hd, tm, stride=heads), :] = vb[:, hd * HEAD_DIM:(hd + 1) * HEAD_DIM]
    if prompt:
        kb_ref[...] = k.astype(BF16)
        vbb_ref[...] = vb.astype(BF16)
    for c in range(tm // rows_per_mix):
        r0 = c * rows_per_mix
        for g in range(A_GROUPS):
            c0 = g * gw
            gate = jnp.dot(mix_ref[g], v16[r0:r0 + rows_per_mix, c0:c0 + gw],
                           preferred_element_type=F32) + bias_ref[:, c0:c0 + gw]
            a_ref[r0:r0 + rows_per_mix, c0:c0 + gw] = (
                u[r0:r0 + rows_per_mix, c0:c0 + gw] * gate).astype(a_ref.dtype)


def _even_in(x, g, w, sn, mix, bias, *, tm, prompt):
    t, d = x.shape
    a_width = sn.shape[1]
    bw = (w.shape[1] - 2 * a_width) // 3
    act = BF16 if prompt else F32
    row = lambda width: pl.BlockSpec((tm, width), lambda i: (i, 0))
    heads = bw // HEAD_DIM
    head_rows = pl.BlockSpec((tm * heads, HEAD_DIM), lambda i: (i, 0))
    out_shape = [jax.ShapeDtypeStruct((t, a_width), act), jax.ShapeDtypeStruct((t, bw), act),
                 jax.ShapeDtypeStruct((t * heads, HEAD_DIM), F32),
                 jax.ShapeDtypeStruct((t * heads, HEAD_DIM), F32)]
    out_specs = [row(a_width), row(bw), head_rows, head_rows]
    if prompt:
        out_shape += [jax.ShapeDtypeStruct((t, bw), BF16)] * 2
        out_specs += [row(bw), row(bw)]
    else:
        out_shape += [jax.ShapeDtypeStruct((t, a_width), F32)]
        out_specs += [row(a_width)]
    return pl.pallas_call(
        functools.partial(_even_in_kernel, prompt=prompt, a_width=a_width),
        grid=(t // tm,),
        in_specs=[row(d), _const_spec(g.shape), _const_spec(w.shape), _const_spec(sn.shape),
                  _const_spec(mix.shape), _const_spec(bias.shape)],
        out_specs=out_specs,
        out_shape=out_shape,
        compiler_params=_params(("parallel",)),
        name="even_in_prompt" if prompt else "even_in_sample",
    )(x, g, w, sn, mix, bias)


def _split_halves(q):
    lane = lax.broadcasted_iota(jnp.int32, q.shape, 1)
    zero = jnp.zeros_like(q)
    return jnp.where(lane < QK_HALF, q, zero), jnp.where(lane >= QK_HALF, q, zero)


def _softmax_step(s, vblk, m_ref, l_ref, acc_ref, rows):
    tiles = [s[:, t * LANES:(t + 1) * LANES] for t in range(s.shape[1] // LANES)]
    m_prev = m_ref[rows, :]
    m_next = jnp.maximum(
        m_prev, jnp.max(functools.reduce(jnp.maximum, tiles), axis=1, keepdims=True))
    p_tiles = [jnp.exp2(t - m_next) for t in tiles]
    alpha = jnp.exp2(m_prev - m_next)
    l_ref[rows, :] = alpha * l_ref[rows, :] + functools.reduce(jnp.add, p_tiles)
    p = jnp.concatenate([t.astype(BF16) for t in p_tiles], axis=1)
    acc_ref[rows, :] = alpha * acc_ref[rows, :] + jnp.dot(p, vblk, preferred_element_type=F32)
    m_ref[rows, :] = m_next


def _row_sum(l):
    return jnp.sum(l, axis=1, keepdims=True)


def _diff_lambda(lq1, lk1, lq2, lk2, lam_init):
    e1 = jnp.exp(jnp.sum(lq1[...] * lk1[...], axis=1, keepdims=True))
    e2 = jnp.exp(jnp.sum(lq2[...] * lk2[...], axis=1, keepdims=True))
    return e1 - e2 + lam_init


def _head_out(o1, o2, lam, sub, lam_init):
    o = o1 - lam * o2
    return _rms(o, sub) * (1.0 - lam_init)


def _attn_prompt_kernel(q_ref, k_ref, v_ref, lq1, lk1, lq2, lk2, sub_ref, o_ref,
                        qs_ref, m_ref, l_ref, acc_ref, *, lam_init):
    tq = q_ref.shape[0]
    i = pl.program_id(2)
    q1, q2 = _split_halves(q_ref[...])
    qs_ref[0:tq, :] = q1
    qs_ref[tq:2 * tq, :] = q2
    m_ref[...] = jnp.full(m_ref.shape, MASK_VALUE, F32)
    l_ref[...] = jnp.zeros(l_ref.shape, F32)
    acc_ref[...] = jnp.zeros(acc_ref.shape, F32)
    all_rows = slice(0, 2 * tq)

    def scores(j):
        kv_rows = pl.ds(pl.multiple_of(j * tq, tq), tq)
        s = lax.dot_general(qs_ref[...], k_ref[kv_rows, :], (((1,), (1,)), ((), ())),
                            preferred_element_type=F32)
        return s, v_ref[kv_rows, :]

    def full_block(j, carry):
        s, vblk = scores(j)
        _softmax_step(s, vblk, m_ref, l_ref, acc_ref, all_rows)
        return carry

    lax.fori_loop(0, i, full_block, 0)
    s, vblk = scores(i)
    row = lax.broadcasted_iota(jnp.int32, s.shape, 0)
    col = lax.broadcasted_iota(jnp.int32, s.shape, 1)
    qpos = jnp.where(row >= tq, row - tq, row)
    s = jnp.where(qpos >= col, s, MASK_VALUE)
    _softmax_step(s, vblk, m_ref, l_ref, acc_ref, all_rows)

    lam = _diff_lambda(lq1, lk1, lq2, lk2, lam_init)
    o1 = acc_ref[0:tq, :] / _row_sum(l_ref[0:tq, :])
    o2 = acc_ref[tq:2 * tq, :] / _row_sum(l_ref[tq:2 * tq, :])
    o_ref[...] = _head_out(o1, o2, lam, sub_ref[...], lam_init).astype(o_ref.dtype)


def _attn_prompt(q, kb, vb, lq1, lk1, lq2, lk2, sub, *, batch, seq, tq, lam_init):
    t, bw = q.shape
    heads = bw // HEAD_DIM
    nq = seq // tq
    vec = lambda a: _const_spec(a.shape)
    return pl.pallas_call(
        functools.partial(_attn_prompt_kernel, lam_init=lam_init),
        grid=(batch, heads, nq),
        in_specs=[pl.BlockSpec((tq, HEAD_DIM), lambda b, h, i: (b * nq + i, h)),
                  pl.BlockSpec((seq, HEAD_DIM), lambda b, h, i: (b, h)),
                  pl.BlockSpec((seq, HEAD_DIM), lambda b, h, i: (b, h)),
                  vec(lq1), vec(lk1), vec(lq2), vec(lk2), vec(sub)],
        out_specs=pl.BlockSpec((tq, HEAD_DIM), lambda b, h, i: (b * nq + i, h)),
        out_shape=jax.ShapeDtypeStruct((t, bw), BF16),
        scratch_shapes=[pltpu.VMEM((2 * tq, HEAD_DIM), BF16),
                        pltpu.VMEM((2 * tq, LANES), F32),
                        pltpu.VMEM((2 * tq, LANES), F32),
                        pltpu.VMEM((2 * tq, HEAD_DIM), F32)],
        compiler_params=_params(("parallel", "parallel", "arbitrary")),
        name="attn_prompt",
    )(q, kb, vb, lq1, lk1, lq2, lk2, sub)


def _attn_sample_kernel(pt_ref, q_ref, kn_ref, vn_ref, *rest, pages_per_step, heads, lam_init):
    del pt_ref
    k_pages = rest[:pages_per_step]
    v_pages = rest[pages_per_step:2 * pages_per_step]
    (lq1, lk1, lq2, lk2, sub_ref, o_ref,
     qs_ref, bias_ref, m_ref, l_ref, acc_ref) = rest[2 * pages_per_step:]
    j = pl.program_id(1)
    t_new = q_ref.shape[0]
    rows_per_head = 2 * t_new
    all_rows = slice(0, heads * rows_per_head)

    def head_match(shape):
        row = lax.broadcasted_iota(jnp.int32, shape, 0)
        col = lax.broadcasted_iota(jnp.int32, shape, 1)
        return row, col, lax.rem(col, heads) == row // rows_per_head

    @pl.when(j == 0)
    def _():
        pieces = []
        for h in range(heads):
            pieces += _split_halves(q_ref[:, h * HEAD_DIM:(h + 1) * HEAD_DIM])
        qs_ref[...] = jnp.concatenate(pieces, axis=0).astype(BF16)
        _, _, ok = head_match(bias_ref.shape)
        bias_ref[...] = jnp.where(ok, 0.0, MASK_VALUE)
        m_ref[...] = jnp.full(m_ref.shape, MASK_VALUE, F32)
        l_ref[...] = jnp.zeros(l_ref.shape, F32)
        acc_ref[...] = jnp.zeros(acc_ref.shape, F32)

    def scores(kblk):
        return lax.dot_general(qs_ref[...], kblk.astype(BF16), (((1,), (1,)), ((), ())),
                               preferred_element_type=F32)

    kblk = jnp.concatenate([k_pages[p][...] for p in range(pages_per_step)], axis=0)
    vblk = jnp.concatenate([v_pages[p][...] for p in range(pages_per_step)], axis=0)
    _softmax_step(scores(kblk) + bias_ref[...], vblk.astype(BF16), m_ref, l_ref, acc_ref, all_rows)

    @pl.when(j == pl.num_programs(1) - 1)
    def _():
        pad = jnp.zeros((LANES - kn_ref.shape[0], HEAD_DIM), F32)
        s = scores(jnp.concatenate([kn_ref[...], pad], axis=0))
        row, col, ok = head_match(s.shape)
        s = jnp.where(ok, s, MASK_VALUE)
        s = jnp.where(col // heads <= lax.rem(row, t_new), s, MASK_VALUE)
        vblk = jnp.concatenate([vn_ref[...], pad], axis=0)
        _softmax_step(s, vblk.astype(BF16), m_ref, l_ref, acc_ref, all_rows)
        lam = _diff_lambda(lq1, lk1, lq2, lk2, lam_init)
        for h in range(heads):
            r0 = h * rows_per_head
            o1 = acc_ref[r0:r0 + t_new, :] / _row_sum(l_ref[r0:r0 + t_new, :])
            o2 = (acc_ref[r0 + t_new:r0 + rows_per_head, :]
                  / _row_sum(l_ref[r0 + t_new:r0 + rows_per_head, :]))
            o_ref[:, h * HEAD_DIM:(h + 1) * HEAD_DIM] = _head_out(
                o1, o2, lam, sub_ref[...], lam_init)


def _attn_sample(page_table, q, k_new, v_new, cache_k, cache_v, layer, lq1, lk1, lq2, lk2, sub,
                 *, t_new, pages_per_step, lam_init):
    t, bw = q.shape
    heads = bw // HEAD_DIM
    dec_batch, n_pages = page_table.shape
    page_rows = cache_k.shape[2]
    steps = n_pages // pages_per_step
    pt_flat = page_table.reshape(-1)

    def page_spec(p):
        return pl.BlockSpec(
            (None, None, page_rows, HEAD_DIM),
            lambda b, j, pt: (layer, pt[b * n_pages + j * pages_per_step + p], 0, 0))

    q_spec = pl.BlockSpec((t_new, bw), lambda b, j, pt: (b, 0))
    new_spec = pl.BlockSpec((t_new * heads, HEAD_DIM), lambda b, j, pt: (b, 0))
    vec = lambda a: pl.BlockSpec(a.shape, lambda b, j, pt: (0, 0))
    grid_spec = pltpu.PrefetchScalarGridSpec(
        num_scalar_prefetch=1,
        grid=(dec_batch, steps),
        in_specs=([q_spec, new_spec, new_spec]
                  + [page_spec(p) for p in range(pages_per_step)] * 2
                  + [vec(lq1), vec(lk1), vec(lq2), vec(lk2), vec(sub)]),
        out_specs=q_spec,
        scratch_shapes=[pltpu.VMEM((heads * 2 * t_new, HEAD_DIM), BF16),
                        pltpu.VMEM((heads * 2 * t_new, pages_per_step * page_rows), F32),
                        pltpu.VMEM((heads * 2 * t_new, LANES), F32),
                        pltpu.VMEM((heads * 2 * t_new, LANES), F32),
                        pltpu.VMEM((heads * 2 * t_new, HEAD_DIM), F32)])
    return pl.pallas_call(
        functools.partial(_attn_sample_kernel, pages_per_step=pages_per_step, heads=heads,
                          lam_init=lam_init),
        grid_spec=grid_spec,
        out_shape=jax.ShapeDtypeStruct((t, bw), F32),
        compiler_params=_params(("parallel", "arbitrary")),
        name="attn_sample",
    )(pt_flat, q, k_new, v_new, *([cache_k] * pages_per_step), *([cache_v] * pages_per_step),
      lq1, lk1, lq2, lk2, sub)


def _swiglu_residual(x1, g_ref, w_in_ref, w_out_ref, ff_chunk):
    d_ff = w_out_ref.shape[0]
    h = _rms(x1, g_ref[...]).astype(BF16)
    acc = x1
    for c0 in range(0, d_ff, ff_chunk):
        gate = jnp.dot(h, w_in_ref[:, c0:c0 + ff_chunk], preferred_element_type=F32)
        up = jnp.dot(h, w_in_ref[:, d_ff + c0:d_ff + c0 + ff_chunk], preferred_element_type=F32)
        act = (gate * jax.nn.sigmoid(gate) * up).astype(BF16)
        acc = acc + jnp.dot(act, w_out_ref[c0:c0 + ff_chunk, :], preferred_element_type=F32)
    return acc


def _mix_ffn_kernel(x_ref, a_ref, b_ref, wo_ref, g_ref, w_in_ref, w_out_ref, o_ref, *scratch,
                    ff_chunk, row_tokens):
    tm, d = x_ref.shape
    ab = jnp.concatenate([a_ref[...].astype(BF16), b_ref[...].astype(BF16)], axis=1)
    x1 = x_ref[...] + jnp.dot(ab, wo_ref[...], preferred_element_type=F32)
    y = _swiglu_residual(x1, g_ref, w_in_ref, w_out_ref, ff_chunk)
    if row_tokens == 1:
        o_ref[...] = y
        return
    slabs = scratch[0]
    for j in range(d // LANES):
        slabs[j] = y[:, j * LANES:(j + 1) * LANES]
    for t in range(row_tokens):
        for j in range(d // LANES):
            o_ref[:, t * d + j * LANES:t * d + (j + 1) * LANES] = (
                slabs[j, pl.ds(t, tm // row_tokens, stride=row_tokens), :])


def _ffn_final_kernel(x_ref, g_ref, w_in_ref, w_out_ref, gf_ref, o_ref, *scratch,
                      ff_chunk, row_tokens):
    if row_tokens == 1:
        x = x_ref[...]
    else:
        r = x_ref.shape[0]
        d = x_ref.shape[1] // row_tokens
        slabs = scratch[0]
        for t in range(row_tokens):
            for j in range(d // LANES):
                slabs[j, pl.ds(t, r, stride=row_tokens), :] = (
                    x_ref[:, t * d + j * LANES:t * d + (j + 1) * LANES])
        x = jnp.concatenate([slabs[j] for j in range(d // LANES)], axis=1)
    o_ref[...] = _rms(_swiglu_residual(x, g_ref, w_in_ref, w_out_ref, ff_chunk), gf_ref[...])


def _mix_ffn(x, a, b, wo, g, w_in, w_out, *, layer, tm, ff_chunk, row_tokens, name):
    t, d = x.shape
    row = lambda arr: pl.BlockSpec((tm, arr.shape[1]), lambda i: (i, 0))
    packed = row_tokens > 1
    return pl.pallas_call(
        functools.partial(_mix_ffn_kernel, ff_chunk=ff_chunk, row_tokens=row_tokens),
        grid=(t // tm,),
        in_specs=[row(x), row(a), row(b), _const_spec(wo.shape), _const_spec(g.shape),
                  _layer_spec(w_in.shape, layer), _layer_spec(w_out.shape, layer)],
        out_specs=pl.BlockSpec((tm // row_tokens, row_tokens * d), lambda i: (i, 0)),
        out_shape=jax.ShapeDtypeStruct((t // row_tokens, row_tokens * d), F32),
        scratch_shapes=[pltpu.VMEM((d // LANES, tm, LANES), F32)] if packed else [],
        compiler_params=_params(("parallel",)),
        name=name,
    )(x, a, b, wo, g, w_in, w_out)


def _ffn_final(x, g, w_in, w_out, gf, *, layer, tm, ff_chunk, row_tokens, name):
    rows, width = x.shape
    d = width // row_tokens
    t = rows * row_tokens
    packed = row_tokens > 1
    return pl.pallas_call(
        functools.partial(_ffn_final_kernel, ff_chunk=ff_chunk, row_tokens=row_tokens),
        grid=(t // tm,),
        in_specs=[pl.BlockSpec((tm // row_tokens, width), lambda i: (i, 0)),
                  _const_spec(g.shape), _layer_spec(w_in.shape, layer),
                  _layer_spec(w_out.shape, layer), _const_spec(gf.shape)],
        out_specs=pl.BlockSpec((tm, d), lambda i: (i, 0)),
        out_shape=jax.ShapeDtypeStruct((t, d), F32),
        scratch_shapes=[pltpu.VMEM((d // LANES, tm, LANES), F32)] if packed else [],
        compiler_params=_params(("parallel",)),
        name=name,
    )(x, g, w_in, w_out, gf)


def _expand_groups(compact, r1, c1):
    n_slabs, rows, k = compact.shape
    width = (k // c1) * SLAB_GROUPS * c1
    src = lax.broadcasted_iota(jnp.int32, (k, width), 0)
    dst = lax.broadcasted_iota(jnp.int32, (k, width), 1)
    select = (src == (dst // (SLAB_GROUPS * c1)) * c1 + dst % c1).astype(F32)
    full = jnp.dot(compact.reshape(n_slabs * rows, k), select,
                   precision=lax.Precision.HIGHEST).reshape(n_slabs, rows, width)
    row_g = (lax.broadcasted_iota(jnp.int32, full.shape, 1) // r1) % SLAB_GROUPS
    col_h = (lax.broadcasted_iota(jnp.int32, full.shape, 2) // c1) % SLAB_GROUPS
    return jnp.where(row_g == col_h, full, 0.0).astype(BF16)


def _s5_rows_kernel(x_ref, g_ref, s0re_ref, s0im_ref, coef_ref, wb_ref, wy_ref, wc_ref, d_ref,
                    wglu_ref, bglu_ref, o_ref, sre_ref, sim_ref,
                    re_s, im_s, gel_s, car_re, car_im, *, row_tokens, n_seq):
    r, width = x_ref.shape
    d = width // row_tokens
    n_slabs = d // LANES
    slab_state = re_s.shape[1] // n_slabs

    xs = [x_ref[:, t * d:(t + 1) * d] for t in range(row_tokens)]
    hs = [_rms(x, g_ref[...]) for x in xs]
    hb = [h.astype(BF16) for h in hs]

    def slab_lhs(j):
        return jnp.concatenate([h[:, j * LANES:(j + 1) * LANES] for h in hb], axis=1)

    for j in range(n_slabs):
        w = jnp.dot(slab_lhs(j), wb_ref[j], preferred_element_type=F32)
        re_s[:, j * slab_state:(j + 1) * slab_state] = w[:, :slab_state]
        im_s[:, j * slab_state:(j + 1) * slab_state] = w[:, slab_state:]

    def row_tile(i, carry):
        cr, ci = carry
        rows = pl.ds(pl.multiple_of(i * SUBLANES, SUBLANES), SUBLANES)
        wr = re_s[rows, :]
        wi = im_s[rows, :]
        for k, shift in enumerate(SCAN_SHIFTS):
            ar = coef_ref[2 * k]
            ai = coef_ref[2 * k + 1]
            sr = pltpu.roll(wr, shift, 0)
            si = pltpu.roll(wi, shift, 0)
            wr, wi = wr + ar * sr - ai * si, wi + ar * si + ai * sr
        pr = coef_ref[2 * len(SCAN_SHIFTS)]
        pi = coef_ref[2 * len(SCAN_SHIFTS) + 1]
        wr, wi = wr + pr * cr - pi * ci, wi + pr * ci + pi * cr
        first = lax.broadcasted_iota(jnp.int32, wr.shape, 0) == 0
        re_s[rows, :] = jnp.where(first, cr, pltpu.roll(wr, 1, 0))
        im_s[rows, :] = jnp.where(first, ci, pltpu.roll(wi, 1, 0))
        return wr[SUBLANES - 1:SUBLANES, :], wi[SUBLANES - 1:SUBLANES, :]

    if n_seq == 1:
        @pl.when(pl.program_id(1) == 0)
        def _():
            car_re[...] = s0re_ref[...]
            car_im[...] = s0im_ref[...]

        cr, ci = lax.fori_loop(0, r // SUBLANES, row_tile, (car_re[...], car_im[...]))
        car_re[...] = cr
        car_im[...] = ci
    else:
        a_re = coef_ref[0][1:2, :]
        a_im = coef_ref[1][1:2, :]
        cr, ci = s0re_ref[...], s0im_ref[...]
        for n in range(r // n_seq):
            rows = slice(n * n_seq, (n + 1) * n_seq)
            wr, wi = re_s[rows, :], im_s[rows, :]
            re_s[rows, :] = cr
            im_s[rows, :] = ci
            cr, ci = a_re * cr - a_im * ci + wr, a_re * ci + a_im * cr + wi
    sre_ref[...] = cr
    sim_ref[...] = ci

    for j in range(n_slabs):
        cols = slice(j * LANES, (j + 1) * LANES)
        st = slice(j * slab_state, (j + 1) * slab_state)
        s_cat = jnp.concatenate([re_s[:, st], im_s[:, st]], axis=1).astype(BF16)
        y = (jnp.dot(slab_lhs(j), wy_ref[j], preferred_element_type=F32)
             + jnp.dot(s_cat, wc_ref[j], preferred_element_type=F32))
        for t in range(row_tokens):
            yt = y[:, t * LANES:(t + 1) * LANES] + d_ref[:, cols] * hs[t][:, cols]
            gel_s[t * r:(t + 1) * r, cols] = jax.nn.gelu(yt).astype(BF16)
    z = jnp.dot(gel_s[...], wglu_ref[...], preferred_element_type=F32) + bglu_ref[...]
    for t in range(row_tokens):
        zt = z[t * r:(t + 1) * r, :]
        o_ref[:, t * d:(t + 1) * d] = xs[t] + zt[:, :d] * jax.nn.sigmoid(zt[:, d:])


def _s5_rows_mix(x, g, s0re, s0im, coef, wb, wy, wc, dskip, wglu, bglu, *, batch, rows_per_seq,
                 r, row_tokens, step_major, name):
    rows, width = x.shape
    d = width // row_tokens
    n_state = s0re.shape[-1]
    if step_major:
        grid = (1, 1)
        r = rows
        state_spec = pl.BlockSpec((batch, n_state), lambda b, i: (0, 0))
        x_spec = pl.BlockSpec((rows, width), lambda b, i: (0, 0))
    else:
        nt = rows_per_seq // r
        grid = (batch, nt)
        state_spec = pl.BlockSpec((None, 1, n_state), lambda b, i: (b, 0, 0))
        x_spec = pl.BlockSpec((r, width), lambda b, i: (b * nt + i, 0))
    consts = (g, coef, wb, wy, wc, dskip, wglu, bglu)
    x1, sre, sim = pl.pallas_call(
        functools.partial(_s5_rows_kernel, row_tokens=row_tokens,
                          n_seq=batch if step_major else 1),
        grid=grid,
        in_specs=[x_spec, _const_spec(g.shape), state_spec, state_spec]
                 + [_const_spec(c.shape) for c in consts[1:]],
        out_specs=[x_spec, state_spec, state_spec],
        out_shape=[jax.ShapeDtypeStruct((rows, width), F32),
                   jax.ShapeDtypeStruct(s0re.shape, F32),
                   jax.ShapeDtypeStruct(s0im.shape, F32)],
        scratch_shapes=[pltpu.VMEM((r, n_state), F32), pltpu.VMEM((r, n_state), F32),
                        pltpu.VMEM((r * row_tokens, d), BF16),
                        pltpu.VMEM((1, n_state), F32), pltpu.VMEM((1, n_state), F32)],
        compiler_params=_params(("parallel", "arbitrary")),
        name=name,
    )(x, g, s0re, s0im, *consts[1:])
    return x1, sre, sim


def _s5_row_tables(a_re, a_im, log_dt, b_re, b_im, c_re, c_im, row_tokens):
    groups, n_p = a_re.shape
    p_tok = row_tokens
    hi = lax.Precision.HIGHEST
    a = lax.complex(a_re, a_im)
    lam_dt = a * jnp.exp(log_dt)[:, None]
    power = lambda k: jnp.exp(lam_dt * k)
    b_bar = ((power(1) - 1.0) / a)[:, :, None] * lax.complex(b_re, b_im)
    c = lax.complex(c_re, c_im)
    rows = jnp.arange(SUBLANES)
    coef = []
    for shift in SCAN_SHIFTS:
        pw = power(p_tok * shift).reshape(1, -1) * (rows >= shift)[:, None]
        coef += [jnp.real(pw), jnp.imag(pw)]
    pw = jnp.exp(lam_dt.reshape(1, -1) * (p_tok * (rows + 1))[:, None].astype(F32))
    coef += [jnp.real(pw), jnp.imag(pw)]
    coef = jnp.stack(coef).astype(F32)
    n_slabs = groups // SLAB_GROUPS
    slabbed = lambda m, lead: m.reshape(m.shape[:lead] + (n_slabs, SLAB_GROUPS) + m.shape[lead + 1:])

    e = jnp.stack([power(p_tok - 1 - s)[:, :, None] * b_bar for s in range(p_tok)])
    e = slabbed(e, 1).transpose(1, 0, 2, 4, 3).reshape(n_slabs, -1, n_p)
    wb = _expand_groups(jnp.concatenate([jnp.real(e), jnp.imag(e)], axis=2), SSM_GROUP, n_p)

    ct = [slabbed(c * power(t + 1)[:, None, :], 0).transpose(0, 1, 3, 2).reshape(
        n_slabs, -1, SSM_GROUP) for t in range(p_tok)]
    ct = jnp.concatenate(ct, axis=2)
    wc = _expand_groups(jnp.concatenate([jnp.real(ct), -jnp.imag(ct)], axis=1), n_p, SSM_GROUP)

    lag = [jnp.real(jnp.einsum('gcp,gpd->gdc', c * power(l)[:, None, :], b_bar, precision=hi)
                    ).reshape(n_slabs, -1, SSM_GROUP) for l in range(p_tok)]
    zero = jnp.zeros_like(lag[0])
    toep = jnp.concatenate(
        [jnp.concatenate([lag[t - s] if t >= s else zero for t in range(p_tok)], axis=2)
         for s in range(p_tok)], axis=1)
    wy = _expand_groups(toep, SSM_GROUP, SSM_GROUP)
    return coef, wb, wy, wc


def kernel(x_prompt, x_sample, cache_k, cache_v, page_table, state_ssm_re, state_ssm_im,
           norm_mix, norm_ffn, norm_final, w_in_even, w_out_even, sgu_norm, sgu_w, sgu_b,
           lambda_q1, lambda_k1, lambda_q2, lambda_k2, attn_subln,
           ssm_a_re, ssm_a_im, ssm_log_dt, ssm_b_re, ssm_b_im, ssm_c_re, ssm_c_im, ssm_d,
           w_glu, b_glu, w_ffn_in, w_ffn_out):
    batch, seq, d = x_prompt.shape
    dec_batch, t_new, _ = x_sample.shape
    depth = norm_mix.shape[0]
    assert depth == 2 and seq % CHUNK == 0 and t_new == SUBLANES
    tp = batch * seq
    ts = dec_batch * t_new
    xp = x_prompt.reshape(tp, d)
    xs = x_sample.reshape(ts, d)
    tm_prompt = 512
    ff_chunk = 256
    row = lambda v: v.reshape(1, -1)

    lam_init = 0.8 - 0.6 * math.exp(-0.3 * 0)
    w_in0 = w_in_even[0].astype(BF16)
    w_out0 = w_out_even[0].astype(BF16)
    a_width = sgu_norm.shape[1]
    gw = a_width // A_GROUPS
    tril = jnp.tril(jnp.ones((CHUNK, CHUNK), bool))
    mix_p = jnp.where(tril, sgu_w[0], 0).astype(BF16)
    bias_p = jnp.repeat(sgu_b[0].T, gw, axis=1)
    small = jnp.where(tril[:t_new, :t_new], sgu_w[0][:, :t_new, :t_new], 0)
    row_i = lax.broadcasted_iota(jnp.int32, (ts, ts), 0)
    col_i = lax.broadcasted_iota(jnp.int32, (ts, ts), 1)
    rep = (row_i[:, :t_new] % t_new == col_i[:, :t_new]).astype(F32)
    tiled = jnp.einsum('at,gts,bs->gab', rep, small, rep, precision=lax.Precision.HIGHEST)
    mix_s = jnp.where(row_i // t_new == col_i // t_new, tiled, 0.0).astype(BF16)
    bias_s = jnp.tile(bias_p[:t_new], (dec_batch, 1))
    lam_vecs = [row(lambda_q1[0]), row(lambda_k1[0]), row(lambda_q2[0]), row(lambda_k2[0]),
                row(attn_subln[0])]

    a_p, q_p, k_p, vb_p, kb_p, vbb_p = _even_in(
        xp, row(norm_mix[0]), w_in0, row(sgu_norm[0]), mix_p, bias_p, tm=tm_prompt, prompt=True)
    a_s, q_s, k_s, vb_s, vn_s = _even_in(
        xs, row(norm_mix[0]), w_in0, row(sgu_norm[0]), mix_s, bias_s, tm=ts, prompt=False)
    b_p = _attn_prompt(q_p, kb_p, vbb_p, *lam_vecs, batch=batch, seq=seq, tq=512,
                       lam_init=lam_init)
    n_layers, n_phys, page_size, heads, _ = cache_k.shape
    page_view = (n_layers, n_phys, page_size * heads, HEAD_DIM)
    b_s = _attn_sample(page_table, q_s, k_s, vb_s, cache_k.reshape(page_view),
                       cache_v.reshape(page_view), 0, *lam_vecs,
                       t_new=t_new, pages_per_step=16, lam_init=lam_init)
    w_ffn_in_b = w_ffn_in.astype(BF16)
    w_ffn_out_b = w_ffn_out.astype(BF16)
    xp = _mix_ffn(xp, a_p, b_p, w_out0, row(norm_ffn[0]), w_ffn_in_b, w_ffn_out_b, layer=0,
                  tm=tm_prompt, ff_chunk=ff_chunk, row_tokens=ROW_TOKENS, name="mix_ffn_prompt")
    xs = _mix_ffn(xs, a_s, b_s, w_out0, row(norm_ffn[0]), w_ffn_in_b, w_ffn_out_b, layer=0,
                  tm=ts, ff_chunk=ff_chunk, row_tokens=1, name="mix_ffn_sample")

    groups, n_p = ssm_a_re.shape[1:]
    n_state = groups * n_p
    ssm = (ssm_a_re[0], ssm_a_im[0], ssm_log_dt[0], ssm_b_re[0], ssm_b_im[0], ssm_c_re[0],
           ssm_c_im[0])
    w_glu0 = w_glu[0].astype(BF16)
    s5_args = (*_s5_row_tables(*ssm, ROW_TOKENS), row(ssm_d[0]), w_glu0, row(b_glu[0]))
    zeros = jnp.zeros((batch, 1, n_state), F32)
    xp, p_re, p_im = _s5_rows_mix(xp, row(norm_mix[1]), zeros, zeros, *s5_args, batch=batch,
                                  rows_per_seq=seq // ROW_TOKENS, r=S5_BLOCK_ROWS,
                                  row_tokens=ROW_TOKENS, step_major=False, name="s5_prompt")
    steps = t_new // ROW_TOKENS
    pack = lambda v: v.reshape(dec_batch, steps, -1).transpose(1, 0, 2).reshape(
        steps * dec_batch, -1)
    unpack = lambda v: v.reshape(steps, dec_batch, -1).transpose(1, 0, 2).reshape(ts, d)
    xs, s_re, s_im = _s5_rows_mix(pack(xs), row(norm_mix[1]),
                                  state_ssm_re[0].reshape(dec_batch, n_state),
                                  state_ssm_im[0].reshape(dec_batch, n_state), *s5_args,
                                  batch=dec_batch, rows_per_seq=steps, r=None,
                                  row_tokens=ROW_TOKENS, step_major=True, name="s5_sample")
    xs = unpack(xs)
    yp = _ffn_final(xp, row(norm_ffn[1]), w_ffn_in_b, w_ffn_out_b, row(norm_final), layer=1,
                    tm=tm_prompt, ff_chunk=ff_chunk, row_tokens=ROW_TOKENS,
                    name="ffn_final_prompt")
    ys = _ffn_final(xs, row(norm_ffn[1]), w_ffn_in_b, w_ffn_out_b, row(norm_final), layer=1,
                    tm=ts, ff_chunk=ff_chunk, row_tokens=1, name="ffn_final_sample")

    st = lambda s, n: s.reshape(1, n, groups, n_p)
    return (yp.reshape(batch, seq, d), ys.reshape(dec_batch, t_new, d),
            k_p.reshape(1, batch, seq, heads, HEAD_DIM), vb_p.reshape(1, batch, seq, heads, HEAD_DIM),
            k_s.reshape(1, dec_batch, t_new, heads, HEAD_DIM),
            vb_s.reshape(1, dec_batch, t_new, heads, HEAD_DIM),
            vn_s.reshape(1, dec_batch, t_new, a_width),
            st(p_re, batch), st(p_im, batch), st(s_re, dec_batch), st(s_im, dec_batch))
```

```python
import functools
import math

import jax
import jax.numpy as jnp
from jax import lax
from jax.experimental import pallas as pl
from jax.experimental.pallas import tpu as pltpu

F32 = jnp.float32
BF16 = jnp.bfloat16

EPS = 1e-6
LANES = 128
SUBLANES = 8
V7X_VMEM_BYTES = 64 * 1024 * 1024
VMEM_LIMIT = V7X_VMEM_BYTES * 7 // 8
MASK_VALUE = -0.7 * float(jnp.finfo(jnp.float32).max)

CHUNK = 128
A_GROUPS = 4
HEAD_DIM = 128
QK_HALF = 64
LOG2E = math.log2(math.e)
SSM_GROUP = 16
SLAB_GROUPS = LANES // SSM_GROUP
SCAN_SHIFTS = (1, 2, 4)
ROW_TOKENS = 4
S5_BLOCK_ROWS = 128

def _const_spec(shape):
    zeros = (0,) * len(shape)
    return pl.BlockSpec(shape, lambda *_: zeros, pipeline_mode=pl.Buffered(1))


def _layer_spec(shape, layer):
    index = (layer,) + (0,) * (len(shape) - 1)
    return pl.BlockSpec((None,) + tuple(shape[1:]), lambda *_: index, pipeline_mode=pl.Buffered(1))


def _rms(x, g):
    return x * lax.rsqrt(jnp.mean(x * x, axis=-1, keepdims=True) + EPS) * g


def _params(semantics):
    return pltpu.CompilerParams(dimension_semantics=semantics, vmem_limit_bytes=VMEM_LIMIT)


def _even_in_kernel(x_ref, g_ref, w_ref, sn_ref, mix_ref, bias_ref, *outs, prompt, a_width):
    if prompt:
        a_ref, q_ref, k_ref, vb_ref, kb_ref, vbb_ref = outs
    else:
        a_ref, q_ref, k_ref, vb_ref, vn_ref = outs
    tm = x_ref.shape[0]
    rows_per_mix = mix_ref.shape[1]
    gw = a_width // A_GROUPS
    h = _rms(x_ref[...], g_ref[...]).astype(BF16)

    def proj(c0, width):
        return jnp.dot(h, w_ref[:, c0:c0 + width], preferred_element_type=F32)

    u = jax.nn.gelu(proj(0, a_width))
    v = _rms(jax.nn.gelu(proj(a_width, a_width)), sn_ref[...])
    if not prompt:
        vn_ref[...] = v
    v16 = v.astype(BF16)
    o = 2 * a_width
    bw = q_ref.shape[1]
    q_ref[...] = (proj(o, bw) * (QK_HALF ** -0.5 * LOG2E)).astype(q_ref.dtype)
    k = proj(o + bw, bw)
    vb = proj(o + 2 * bw, bw)
    heads = bw // HEAD_DIM
    for hd in range(heads):
        k_ref[pl.ds(hd, tm, stride=heads), :] = k[:, hd * HEAD_DIM:(hd + 1) * HEAD_DIM]
        vb_ref[pl.ds(hd, tm, stride=heads), :] = vb[:, hd * HEAD_DIM:(hd + 1) * HEAD_DIM]
    if prompt:
        kb_ref[...] = k.astype(BF16)
        vbb_ref[...] = vb.astype(BF16)
    for c in range(tm // rows_per_mix):
        r0 = c * rows_per_mix
        for g in range(A_GROUPS):
            c0 = g * gw
            gate = jnp.dot(mix_ref[g], v16[r0:r0 + rows_per_mix, c0:c0 + gw],
                           preferred_element_type=F32) + bias_ref[:, c0:c0 + gw]
            a_ref[r0:r0 + rows_per_mix, c0:c0 + gw] = (
                u[r0:r0 + rows_per_mix, c0:c0 + gw] * gate).astype(a_ref.dtype)


def _even_in(x, g, w, sn, mix, bias, *, tm, prompt):
    t, d = x.shape
    a_width = sn.shape[1]
    bw = (w.shape[1] - 2 * a_width) // 3
    act = BF16 if prompt else F32
    row = lambda width: pl.BlockSpec((tm, width), lambda i: (i, 0))
    heads = bw // HEAD_DIM
    head_rows = pl.BlockSpec((tm * heads, HEAD_DIM), lambda i: (i, 0))
    out_shape = [jax.ShapeDtypeStruct((t, a_width), act), jax.ShapeDtypeStruct((t, bw), act),
                 jax.ShapeDtypeStruct((t * heads, HEAD_DIM), F32),
                 jax.ShapeDtypeStruct((t * heads, HEAD_DIM), F32)]
    out_specs = [row(a_width), row(bw), head_rows, head_rows]
    if prompt:
        out_shape += [jax.ShapeDtypeStruct((t, bw), BF16)] * 2
        out_specs += [row(bw), row(bw)]
    else:
        out_shape += [jax.ShapeDtypeStruct((t, a_width), F32)]
        out_specs += [row(a_width)]
    return pl.pallas_call(
        functools.partial(_even_in_kernel, prompt=prompt, a_width=a_width),
        grid=(t // tm,),
        in_specs=[row(d), _const_spec(g.shape), _const_spec(w.shape), _const_spec(sn.shape),
                  _const_spec(mix.shape), _const_spec(bias.shape)],
        out_specs=out_specs,
        out_shape=out_shape,
        compiler_params=_params(("parallel",)),
        name="even_in_prompt" if prompt else "even_in_sample",
    )(x, g, w, sn, mix, bias)


def _split_halves(q):
    lane = lax.broadcasted_iota(jnp.int32, q.shape, 1)
    zero = jnp.zeros_like(q)
    return jnp.where(lane < QK_HALF, q, zero), jnp.where(lane >= QK_HALF, q, zero)


def _softmax_step(s, vblk, m_ref, l_ref, acc_ref, rows):
    tiles = [s[:, t * LANES:(t + 1) * LANES] for t in range(s.shape[1] // LANES)]
    m_prev = m_ref[rows, :]
    m_next = jnp.maximum(
        m_prev, jnp.max(functools.reduce(jnp.maximum, tiles), axis=1, keepdims=True))
    p_tiles = [jnp.exp2(t - m_next) for t in tiles]
    alpha = jnp.exp2(m_prev - m_next)
    l_ref[rows, :] = alpha * l_ref[rows, :] + functools.reduce(jnp.add, p_tiles)
    p = jnp.concatenate([t.astype(BF16) for t in p_tiles], axis=1)
    acc_ref[rows, :] = alpha * acc_ref[rows, :] + jnp.dot(p, vblk, preferred_element_type=F32)
    m_ref[rows, :] = m_next


def _row_sum(l):
    return jnp.sum(l, axis=1, keepdims=True)


def _diff_lambda(lq1, lk1, lq2, lk2, lam_init):
    e1 = jnp.exp(jnp.sum(lq1[...] * lk1[...], axis=1, keepdims=True))
    e2 = jnp.exp(jnp.sum(lq2[...] * lk2[...], axis=1, keepdims=True))
    return e1 - e2 + lam_init


def _head_out(o1, o2, lam, sub, lam_init):
    o = o1 - lam * o2
    return _rms(o, sub) * (1.0 - lam_init)


def _attn_prompt_kernel(q_ref, k_ref, v_ref, lq1, lk1, lq2, lk2, sub_ref, o_ref,
                        qs_ref, m_ref, l_ref, acc_ref, *, lam_init):
    tq = q_ref.shape[0]
    i = pl.program_id(2)
    q1, q2 = _split_halves(q_ref[...])
    qs_ref[0:tq, :] = q1
    qs_ref[tq:2 * tq, :] = q2
    m_ref[...] = jnp.full(m_ref.shape, MASK_VALUE, F32)
    l_ref[...] = jnp.zeros(l_ref.shape, F32)
    acc_ref[...] = jnp.zeros(acc_ref.shape, F32)
    all_rows = slice(0, 2 * tq)

    def scores(j):
        kv_rows = pl.ds(pl.multiple_of(j * tq, tq), tq)
        s = lax.dot_general(qs_ref[...], k_ref[kv_rows, :], (((1,), (1,)), ((), ())),
                            preferred_element_type=F32)
        return s, v_ref[kv_rows, :]

    def full_block(j):
        s, vblk = scores(j)
        _softmax_step(s, vblk, m_ref, l_ref, acc_ref, all_rows)

    def full_pair(jj, carry):
        full_block(2 * jj)
        full_block(2 * jj + 1)
        return carry

    lax.fori_loop(0, i // 2, full_pair, 0)

    @pl.when(i % 2 == 1)
    def _():
        full_block(i - 1)
    s, vblk = scores(i)
    row = lax.broadcasted_iota(jnp.int32, s.shape, 0)
    col = lax.broadcasted_iota(jnp.int32, s.shape, 1)
    qpos = jnp.where(row >= tq, row - tq, row)
    s = jnp.where(qpos >= col, s, MASK_VALUE)
    _softmax_step(s, vblk, m_ref, l_ref, acc_ref, all_rows)

    lam = _diff_lambda(lq1, lk1, lq2, lk2, lam_init)
    o1 = acc_ref[0:tq, :] / _row_sum(l_ref[0:tq, :])
    o2 = acc_ref[tq:2 * tq, :] / _row_sum(l_ref[tq:2 * tq, :])
    o_ref[...] = _head_out(o1, o2, lam, sub_ref[...], lam_init).astype(o_ref.dtype)


def _attn_prompt(q, kb, vb, lq1, lk1, lq2, lk2, sub, *, batch, seq, tq, lam_init):
    t, bw = q.shape
    heads = bw // HEAD_DIM
    nq = seq // tq
    vec = lambda a: _const_spec(a.shape)
    return pl.pallas_call(
        functools.partial(_attn_prompt_kernel, lam_init=lam_init),
        grid=(batch, heads, nq),
        in_specs=[pl.BlockSpec((tq, HEAD_DIM), lambda b, h, i: (b * nq + i, h)),
                  pl.BlockSpec((seq, HEAD_DIM), lambda b, h, i: (b, h)),
                  pl.BlockSpec((seq, HEAD_DIM), lambda b, h, i: (b, h)),
                  vec(lq1), vec(lk1), vec(lq2), vec(lk2), vec(sub)],
        out_specs=pl.BlockSpec((tq, HEAD_DIM), lambda b, h, i: (b * nq + i, h)),
        out_shape=jax.ShapeDtypeStruct((t, bw), BF16),
        scratch_shapes=[pltpu.VMEM((2 * tq, HEAD_DIM), BF16),
                        pltpu.VMEM((2 * tq, LANES), F32),
                        pltpu.VMEM((2 * tq, LANES), F32),
                        pltpu.VMEM((2 * tq, HEAD_DIM), F32)],
        compiler_params=_params(("parallel", "parallel", "arbitrary")),
        name="attn_prompt",
    )(q, kb, vb, lq1, lk1, lq2, lk2, sub)


def _attn_sample_kernel(pt_ref, q_ref, kn_ref, vn_ref, *rest, pages_per_step, heads, lam_init):
    del pt_ref
    k_pages = rest[:pages_per_step]
    v_pages = rest[pages_per_step:2 * pages_per_step]
    (lq1, lk1, lq2, lk2, sub_ref, o_ref,
     qs_ref, bias_ref, m_ref, l_ref, acc_ref) = rest[2 * pages_per_step:]
    j = pl.program_id(1)
    t_new = q_ref.shape[0]
    rows_per_head = 2 * t_new
    all_rows = slice(0, heads * rows_per_head)

    def head_match(shape):
        row = lax.broadcasted_iota(jnp.int32, shape, 0)
        col = lax.broadcasted_iota(jnp.int32, shape, 1)
        return row, col, lax.rem(col, heads) == row // rows_per_head

    @pl.when(j == 0)
    def _():
        pieces = []
        for h in range(heads):
            pieces += _split_halves(q_ref[:, h * HEAD_DIM:(h + 1) * HEAD_DIM])
        qs_ref[...] = jnp.concatenate(pieces, axis=0).astype(BF16)
        _, _, ok = head_match(bias_ref.shape)
        bias_ref[...] = jnp.where(ok, 0.0, MASK_VALUE)
        m_ref[...] = jnp.full(m_ref.shape, MASK_VALUE, F32)
        l_ref[...] = jnp.zeros(l_ref.shape, F32)
        acc_ref[...] = jnp.zeros(acc_ref.shape, F32)

    def scores(kblk):
        return lax.dot_general(qs_ref[...], kblk.astype(BF16), (((1,), (1,)), ((), ())),
                               preferred_element_type=F32)

    kblk = jnp.concatenate([k_pages[p][...] for p in range(pages_per_step)], axis=0)
    vblk = jnp.concatenate([v_pages[p][...] for p in range(pages_per_step)], axis=0)
    _softmax_step(scores(kblk) + bias_ref[...], vblk.astype(BF16), m_ref, l_ref, acc_ref, all_rows)

    @pl.when(j == pl.num_programs(1) - 1)
    def _():
        pad = jnp.zeros((LANES - kn_ref.shape[0], HEAD_DIM), F32)
        s = scores(jnp.concatenate([kn_ref[...], pad], axis=0))
        row, col, ok = head_match(s.shape)
        s = jnp.where(ok, s, MASK_VALUE)
        s = jnp.where(col // heads <= lax.rem(row, t_new), s, MASK_VALUE)
        vblk = jnp.concatenate([vn_ref[...], pad], axis=0)
        _softmax_step(s, vblk.astype(BF16), m_ref, l_ref, acc_ref, all_rows)
        lam = _diff_lambda(lq1, lk1, lq2, lk2, lam_init)
        for h in range(heads):
            r0 = h * rows_per_head
            o1 = acc_ref[r0:r0 + t_new, :] / _row_sum(l_ref[r0:r0 + t_new, :])
            o2 = (acc_ref[r0 + t_new:r0 + rows_per_head, :]
                  / _row_sum(l_ref[r0 + t_new:r0 + rows_per_head, :]))
            o_ref[:, h * HEAD_DIM:(h + 1) * HEAD_DIM] = _head_out(
                o1, o2, lam, sub_ref[...], lam_init)


def _attn_sample(page_table, q, k_new, v_new, cache_k, cache_v, layer, lq1, lk1, lq2, lk2, sub,
                 *, t_new, pages_per_step, lam_init):
    t, bw = q.shape
    heads = bw // HEAD_DIM
    dec_batch, n_pages = page_table.shape
    page_rows = cache_k.shape[2]
    steps = n_pages // pages_per_step
    pt_flat = page_table.reshape(-1)

    def page_spec(p):
        return pl.BlockSpec(
            (None, None, page_rows, HEAD_DIM),
            lambda b, j, pt: (layer, pt[b * n_pages + j * pages_per_step + p], 0, 0))

    q_spec = pl.BlockSpec((t_new, bw), lambda b, j, pt: (b, 0))
    new_spec = pl.BlockSpec((t_new * heads, HEAD_DIM), lambda b, j, pt: (b, 0))
    vec = lambda a: pl.BlockSpec(a.shape, lambda b, j, pt: (0, 0))
    grid_spec = pltpu.PrefetchScalarGridSpec(
        num_scalar_prefetch=1,
        grid=(dec_batch, steps),
        in_specs=([q_spec, new_spec, new_spec]
                  + [page_spec(p) for p in range(pages_per_step)] * 2
                  + [vec(lq1), vec(lk1), vec(lq2), vec(lk2), vec(sub)]),
        out_specs=q_spec,
        scratch_shapes=[pltpu.VMEM((heads * 2 * t_new, HEAD_DIM), BF16),
                        pltpu.VMEM((heads * 2 * t_new, pages_per_step * page_rows), F32),
                        pltpu.VMEM((heads * 2 * t_new, LANES), F32),
                        pltpu.VMEM((heads * 2 * t_new, LANES), F32),
                        pltpu.VMEM((heads * 2 * t_new, HEAD_DIM), F32)])
    return pl.pallas_call(
        functools.partial(_attn_sample_kernel, pages_per_step=pages_per_step, heads=heads,
                          lam_init=lam_init),
        grid_spec=grid_spec,
        out_shape=jax.ShapeDtypeStruct((t, bw), F32),
        compiler_params=_params(("parallel", "arbitrary")),
        name="attn_sample",
    )(pt_flat, q, k_new, v_new, *([cache_k] * pages_per_step), *([cache_v] * pages_per_step),
      lq1, lk1, lq2, lk2, sub)


def _swiglu_residual(x1, g_ref, w_in_ref, w_out_ref, ff_chunk):
    d_ff = w_out_ref.shape[0]
    h = _rms(x1, g_ref[...]).astype(BF16)
    acc = x1
    for c0 in range(0, d_ff, ff_chunk):
        gate = jnp.dot(h, w_in_ref[:, c0:c0 + ff_chunk], preferred_element_type=F32)
        up = jnp.dot(h, w_in_ref[:, d_ff + c0:d_ff + c0 + ff_chunk], preferred_element_type=F32)
        act = (gate * jax.nn.sigmoid(gate) * up).astype(BF16)
        acc = acc + jnp.dot(act, w_out_ref[c0:c0 + ff_chunk, :], preferred_element_type=F32)
    return acc


def _mix_ffn_kernel(x_ref, a_ref, b_ref, wo_ref, g_ref, w_in_ref, w_out_ref, o_ref, *scratch,
                    ff_chunk, row_tokens):
    tm, d = x_ref.shape
    ab = jnp.concatenate([a_ref[...].astype(BF16), b_ref[...].astype(BF16)], axis=1)
    x1 = x_ref[...] + jnp.dot(ab, wo_ref[...], preferred_element_type=F32)
    y = _swiglu_residual(x1, g_ref, w_in_ref, w_out_ref, ff_chunk)
    if row_tokens == 1:
        o_ref[...] = y
        return
    slabs = scratch[0]
    for j in range(d // LANES):
        slabs[j] = y[:, j * LANES:(j + 1) * LANES]
    for t in range(row_tokens):
        for j in range(d // LANES):
            o_ref[:, t * d + j * LANES:t * d + (j + 1) * LANES] = (
                slabs[j, pl.ds(t, tm // row_tokens, stride=row_tokens), :])


def _ffn_final_kernel(x_ref, g_ref, w_in_ref, w_out_ref, gf_ref, o_ref, *scratch,
                      ff_chunk, row_tokens):
    if row_tokens == 1:
        x = x_ref[...]
    else:
        r = x_ref.shape[0]
        d = x_ref.shape[1] // row_tokens
        slabs = scratch[0]
        for t in range(row_tokens):
            for j in range(d // LANES):
                slabs[j, pl.ds(t, r, stride=row_tokens), :] = (
                    x_ref[:, t * d + j * LANES:t * d + (j + 1) * LANES])
        x = jnp.concatenate([slabs[j] for j in range(d // LANES)], axis=1)
    o_ref[...] = _rms(_swiglu_residual(x, g_ref, w_in_ref, w_out_ref, ff_chunk), gf_ref[...])


def _mix_ffn(x, a, b, wo, g, w_in, w_out, *, layer, tm, ff_chunk, row_tokens, name):
    t, d = x.shape
    row = lambda arr: pl.BlockSpec((tm, arr.shape[1]), lambda i: (i, 0))
    packed = row_tokens > 1
    return pl.pallas_call(
        functools.partial(_mix_ffn_kernel, ff_chunk=ff_chunk, row_tokens=row_tokens),
        grid=(t // tm,),
        in_specs=[row(x), row(a), row(b), _const_spec(wo.shape), _const_spec(g.shape),
                  _layer_spec(w_in.shape, layer), _layer_spec(w_out.shape, layer)],
        out_specs=pl.BlockSpec((tm // row_tokens, row_tokens * d), lambda i: (i, 0)),
        out_shape=jax.ShapeDtypeStruct((t // row_tokens, row_tokens * d), F32),
        scratch_shapes=[pltpu.VMEM((d // LANES, tm, LANES), F32)] if packed else [],
        compiler_params=_params(("parallel",)),
        name=name,
    )(x, a, b, wo, g, w_in, w_out)


def _ffn_final(x, g, w_in, w_out, gf, *, layer, tm, ff_chunk, row_tokens, name):
    rows, width = x.shape
    d = width // row_tokens
    t = rows * row_tokens
    packed = row_tokens > 1
    return pl.pallas_call(
        functools.partial(_ffn_final_kernel, ff_chunk=ff_chunk, row_tokens=row_tokens),
        grid=(t // tm,),
        in_specs=[pl.BlockSpec((tm // row_tokens, width), lambda i: (i, 0)),
                  _const_spec(g.shape), _layer_spec(w_in.shape, layer),
                  _layer_spec(w_out.shape, layer), _const_spec(gf.shape)],
        out_specs=pl.BlockSpec((tm, d), lambda i: (i, 0)),
        out_shape=jax.ShapeDtypeStruct((t, d), F32),
        scratch_shapes=[pltpu.VMEM((d // LANES, tm, LANES), F32)] if packed else [],
        compiler_params=_params(("parallel",)),
        name=name,
    )(x, g, w_in, w_out, gf)


def _expand_groups(compact, r1, c1):
    n_slabs, rows, k = compact.shape
    width = (k // c1) * SLAB_GROUPS * c1
    src = lax.broadcasted_iota(jnp.int32, (k, width), 0)
    dst = lax.broadcasted_iota(jnp.int32, (k, width), 1)
    select = (src == (dst // (SLAB_GROUPS * c1)) * c1 + dst % c1).astype(F32)
    full = jnp.dot(compact.reshape(n_slabs * rows, k), select,
                   precision=lax.Precision.HIGHEST).reshape(n_slabs, rows, width)
    row_g = (lax.broadcasted_iota(jnp.int32, full.shape, 1) // r1) % SLAB_GROUPS
    col_h = (lax.broadcasted_iota(jnp.int32, full.shape, 2) // c1) % SLAB_GROUPS
    return jnp.where(row_g == col_h, full, 0.0).astype(BF16)


def _s5_rows_kernel(x_ref, g_ref, s0re_ref, s0im_ref, coef_ref, wb_ref, wy_ref, wc_ref, d_ref,
                    wglu_ref, bglu_ref, o_ref, sre_ref, sim_ref,
                    re_s, im_s, gel_s, car_re, car_im, *, row_tokens, n_seq):
    r, width = x_ref.shape
    d = width // row_tokens
    n_slabs = d // LANES
    slab_state = re_s.shape[1] // n_slabs

    xs = [x_ref[:, t * d:(t + 1) * d] for t in range(row_tokens)]
    hs = [_rms(x, g_ref[...]) for x in xs]
    hb = [h.astype(BF16) for h in hs]

    def slab_lhs(j):
        return jnp.concatenate([h[:, j * LANES:(j + 1) * LANES] for h in hb], axis=1)

    for j in range(n_slabs):
        w = jnp.dot(slab_lhs(j), wb_ref[j], preferred_element_type=F32)
        re_s[:, j * slab_state:(j + 1) * slab_state] = w[:, :slab_state]
        im_s[:, j * slab_state:(j + 1) * slab_state] = w[:, slab_state:]

    def row_tile(i, carry):
        cr, ci = carry
        rows = pl.ds(pl.multiple_of(i * SUBLANES, SUBLANES), SUBLANES)
        wr = re_s[rows, :]
        wi = im_s[rows, :]
        for k, shift in enumerate(SCAN_SHIFTS):
            ar = coef_ref[2 * k]
            ai = coef_ref[2 * k + 1]
            sr = pltpu.roll(wr, shift, 0)
            si = pltpu.roll(wi, shift, 0)
            wr, wi = wr + ar * sr - ai * si, wi + ar * si + ai * sr
        pr = coef_ref[2 * len(SCAN_SHIFTS)]
        pi = coef_ref[2 * len(SCAN_SHIFTS) + 1]
        wr, wi = wr + pr * cr - pi * ci, wi + pr * ci + pi * cr
        first = lax.broadcasted_iota(jnp.int32, wr.shape, 0) == 0
        re_s[rows, :] = jnp.where(first, cr, pltpu.roll(wr, 1, 0))
        im_s[rows, :] = jnp.where(first, ci, pltpu.roll(wi, 1, 0))
        return wr[SUBLANES - 1:SUBLANES, :], wi[SUBLANES - 1:SUBLANES, :]

    if n_seq == 1:
        @pl.when(pl.program_id(1) == 0)
        def _():
            car_re[...] = s0re_ref[...]
            car_im[...] = s0im_ref[...]

        cr, ci = lax.fori_loop(0, r // SUBLANES, row_tile, (car_re[...], car_im[...]))
        car_re[...] = cr
        car_im[...] = ci
    else:
        a_re = coef_ref[0][1:2, :]
        a_im = coef_ref[1][1:2, :]
        cr, ci = s0re_ref[...], s0im_ref[...]
        for n in range(r // n_seq):
            rows = slice(n * n_seq, (n + 1) * n_seq)
            wr, wi = re_s[rows, :], im_s[rows, :]
            re_s[rows, :] = cr
            im_s[rows, :] = ci
            cr, ci = a_re * cr - a_im * ci + wr, a_re * ci + a_im * cr + wi
    sre_ref[...] = cr
    sim_ref[...] = ci

    for j in range(n_slabs):
        cols = slice(j * LANES, (j + 1) * LANES)
        st = slice(j * slab_state, (j + 1) * slab_state)
        s_cat = jnp.concatenate([re_s[:, st], im_s[:, st]], axis=1).astype(BF16)
        y = (jnp.dot(slab_lhs(j), wy_ref[j], preferred_element_type=F32)
             + jnp.dot(s_cat, wc_ref[j], preferred_element_type=F32))
        for t in range(row_tokens):
            yt = y[:, t * LANES:(t + 1) * LANES] + d_ref[:, cols] * hs[t][:, cols]
            gel_s[t * r:(t + 1) * r, cols] = jax.nn.gelu(yt).astype(BF16)
    z = jnp.dot(gel_s[...], wglu_ref[...], preferred_element_type=F32) + bglu_ref[...]
    for t in range(row_tokens):
        zt = z[t * r:(t + 1) * r, :]
        o_ref[:, t * d:(t + 1) * d] = xs[t] + zt[:, :d] * jax.nn.sigmoid(zt[:, d:])


def _s5_rows_mix(x, g, s0re, s0im, coef, wb, wy, wc, dskip, wglu, bglu, *, batch, rows_per_seq,
                 r, row_tokens, step_major, name):
    rows, width = x.shape
    d = width // row_tokens
    n_state = s0re.shape[-1]
    if step_major:
        grid = (1, 1)
        r = rows
        state_spec = pl.BlockSpec((batch, n_state), lambda b, i: (0, 0))
        x_spec = pl.BlockSpec((rows, width), lambda b, i: (0, 0))
    else:
        nt = rows_per_seq // r
        grid = (batch, nt)
        state_spec = pl.BlockSpec((None, 1, n_state), lambda b, i: (b, 0, 0))
        x_spec = pl.BlockSpec((r, width), lambda b, i: (b * nt + i, 0))
    consts = (g, coef, wb, wy, wc, dskip, wglu, bglu)
    x1, sre, sim = pl.pallas_call(
        functools.partial(_s5_rows_kernel, row_tokens=row_tokens,
                          n_seq=batch if step_major else 1),
        grid=grid,
        in_specs=[x_spec, _const_spec(g.shape), state_spec, state_spec]
                 + [_const_spec(c.shape) for c in consts[1:]],
        out_specs=[x_spec, state_spec, state_spec],
        out_shape=[jax.ShapeDtypeStruct((rows, width), F32),
                   jax.ShapeDtypeStruct(s0re.shape, F32),
                   jax.ShapeDtypeStruct(s0im.shape, F32)],
        scratch_shapes=[pltpu.VMEM((r, n_state), F32), pltpu.VMEM((r, n_state), F32),
                        pltpu.VMEM((r * row_tokens, d), BF16),
                        pltpu.VMEM((1, n_state), F32), pltpu.VMEM((1, n_state), F32)],
        compiler_params=_params(("parallel", "arbitrary")),
        name=name,
    )(x, g, s0re, s0im, *consts[1:])
    return x1, sre, sim


def _s5_row_tables(a_re, a_im, log_dt, b_re, b_im, c_re, c_im, row_tokens):
    groups, n_p = a_re.shape
    p_tok = row_tokens
    hi = lax.Precision.HIGHEST
    n_slabs = groups // SLAB_GROUPS
    dt = jnp.exp(log_dt)[:, None]
    ks = jnp.arange(p_tok * SUBLANES + 1, dtype=F32)[:, None, None]
    mag = jnp.exp((a_re * dt)[None] * ks)
    ang = (a_im * dt)[None] * ks
    pw_re, pw_im = mag * jnp.cos(ang), mag * jnp.sin(ang)
    den = a_re * a_re + a_im * a_im
    q_re = ((pw_re[1] - 1.0) * a_re + pw_im[1] * a_im) / den
    q_im = (pw_im[1] * a_re - (pw_re[1] - 1.0) * a_im) / den
    bb_re = q_re[:, :, None] * b_re - q_im[:, :, None] * b_im
    bb_im = q_re[:, :, None] * b_im + q_im[:, :, None] * b_re

    rows = jnp.arange(SUBLANES)
    coef = []
    for shift in SCAN_SHIFTS:
        keep = (rows >= shift)[:, None].astype(F32)
        coef += [pw_re[p_tok * shift].reshape(1, -1) * keep,
                 pw_im[p_tok * shift].reshape(1, -1) * keep]
    pick = lambda pw, idx: jnp.stack([pw[k] for k in idx])
    carry = [p_tok * (r + 1) for r in range(SUBLANES)]
    coef += [pick(pw_re, carry).reshape(SUBLANES, -1), pick(pw_im, carry).reshape(SUBLANES, -1)]
    coef = jnp.stack(coef)

    back = [p_tok - 1 - s for s in range(p_tok)]
    er, ei = pick(pw_re, back)[:, :, :, None], pick(pw_im, back)[:, :, :, None]
    lay_b = lambda m: m.reshape(p_tok, n_slabs, SLAB_GROUPS, n_p, SSM_GROUP).transpose(
        1, 0, 2, 4, 3).reshape(n_slabs, -1, n_p)
    wb = _expand_groups(jnp.concatenate([lay_b(er * bb_re - ei * bb_im),
                                         lay_b(er * bb_im + ei * bb_re)], axis=2), SSM_GROUP, n_p)

    kr, ki = pw_re[:p_tok + 1][:, :, None, :], pw_im[:p_tok + 1][:, :, None, :]
    ck_re, ck_im = c_re * kr - c_im * ki, c_re * ki + c_im * kr

    lay_c = lambda m: m.reshape(p_tok, n_slabs, SLAB_GROUPS, SSM_GROUP, n_p).transpose(
        1, 2, 4, 0, 3).reshape(n_slabs, SLAB_GROUPS * n_p, -1)
    wc = _expand_groups(jnp.concatenate([lay_c(ck_re[1:]), -lay_c(ck_im[1:])], axis=1),
                        n_p, SSM_GROUP)

    lag = (jnp.einsum('lgcp,gpd->lgdc', ck_re[:p_tok], bb_re, precision=hi)
           - jnp.einsum('lgcp,gpd->lgdc', ck_im[:p_tok], bb_im, precision=hi))
    lag = lag.reshape(p_tok, n_slabs, SLAB_GROUPS * SSM_GROUP, SSM_GROUP)
    zero = jnp.zeros_like(lag[0])
    toep = jnp.concatenate(
        [jnp.concatenate([lag[t - s] if t >= s else zero for t in range(p_tok)], axis=2)
         for s in range(p_tok)], axis=1)
    wy = _expand_groups(toep, SSM_GROUP, SSM_GROUP)
    return coef, wb, wy, wc


def kernel(x_prompt, x_sample, cache_k, cache_v, page_table, state_ssm_re, state_ssm_im,
           norm_mix, norm_ffn, norm_final, w_in_even, w_out_even, sgu_norm, sgu_w, sgu_b,
           lambda_q1, lambda_k1, lambda_q2, lambda_k2, attn_subln,
           ssm_a_re, ssm_a_im, ssm_log_dt, ssm_b_re, ssm_b_im, ssm_c_re, ssm_c_im, ssm_d,
           w_glu, b_glu, w_ffn_in, w_ffn_out):
    batch, seq, d = x_prompt.shape
    dec_batch, t_new, _ = x_sample.shape
    depth = norm_mix.shape[0]
    assert depth == 2 and seq % CHUNK == 0 and t_new == SUBLANES
    tp = batch * seq
    ts = dec_batch * t_new
    xp = x_prompt.reshape(tp, d)
    xs = x_sample.reshape(ts, d)
    tm_prompt = 512
    ff_chunk = 256
    row = lambda v: v.reshape(1, -1)

    lam_init = 0.8 - 0.6 * math.exp(-0.3 * 0)
    w_in0 = w_in_even[0].astype(BF16)
    w_out0 = w_out_even[0].astype(BF16)
    a_width = sgu_norm.shape[1]
    gw = a_width // A_GROUPS
    tril = jnp.tril(jnp.ones((CHUNK, CHUNK), bool))
    mix_p = jnp.where(tril, sgu_w[0], 0).astype(BF16)
    bias_p = jnp.repeat(sgu_b[0].T, gw, axis=1)
    small = jnp.where(tril[:t_new, :t_new], sgu_w[0][:, :t_new, :t_new], 0)
    row_i = lax.broadcasted_iota(jnp.int32, (ts, ts), 0)
    col_i = lax.broadcasted_iota(jnp.int32, (ts, ts), 1)
    rep = (row_i[:, :t_new] % t_new == col_i[:, :t_new]).astype(F32)
    tiled = jnp.einsum('at,gts,bs->gab', rep, small, rep, precision=lax.Precision.HIGHEST)
    mix_s = jnp.where(row_i // t_new == col_i // t_new, tiled, 0.0).astype(BF16)
    bias_s = jnp.tile(bias_p[:t_new], (dec_batch, 1))
    lam_vecs = [row(lambda_q1[0]), row(lambda_k1[0]), row(lambda_q2[0]), row(lambda_k2[0]),
                row(attn_subln[0])]

    a_p, q_p, k_p, vb_p, kb_p, vbb_p = _even_in(
        xp, row(norm_mix[0]), w_in0, row(sgu_norm[0]), mix_p, bias_p, tm=tm_prompt, prompt=True)
    a_s, q_s, k_s, vb_s, vn_s = _even_in(
        xs, row(norm_mix[0]), w_in0, row(sgu_norm[0]), mix_s, bias_s, tm=ts, prompt=False)
    b_p = _attn_prompt(q_p, kb_p, vbb_p, *lam_vecs, batch=batch, seq=seq, tq=512,
                       lam_init=lam_init)
    n_layers, n_phys, page_size, heads, _ = cache_k.shape
    page_view = (n_layers, n_phys, page_size * heads, HEAD_DIM)
    b_s = _attn_sample(page_table, q_s, k_s, vb_s, cache_k.reshape(page_view),
                       cache_v.reshape(page_view), 0, *lam_vecs,
                       t_new=t_new, pages_per_step=16, lam_init=lam_init)
    w_ffn_in_b = w_ffn_in.astype(BF16)
    w_ffn_out_b = w_ffn_out.astype(BF16)
    xp = _mix_ffn(xp, a_p, b_p, w_out0, row(norm_ffn[0]), w_ffn_in_b, w_ffn_out_b, layer=0,
                  tm=tm_prompt, ff_chunk=ff_chunk, row_tokens=ROW_TOKENS, name="mix_ffn_prompt")
    xs = _mix_ffn(xs, a_s, b_s, w_out0, row(norm_ffn[0]), w_ffn_in_b, w_ffn_out_b, layer=0,
                  tm=ts, ff_chunk=ff_chunk, row_tokens=1, name="mix_ffn_sample")

    groups, n_p = ssm_a_re.shape[1:]
    n_state = groups * n_p
    ssm = (ssm_a_re[0], ssm_a_im[0], ssm_log_dt[0], ssm_b_re[0], ssm_b_im[0], ssm_c_re[0],
           ssm_c_im[0])
    w_glu0 = w_glu[0].astype(BF16)
    s5_args = (*_s5_row_tables(*ssm, ROW_TOKENS), row(ssm_d[0]), w_glu0, row(b_glu[0]))
    zeros = jnp.zeros((batch, 1, n_state), F32)
    xp, p_re, p_im = _s5_rows_mix(xp, row(norm_mix[1]), zeros, zeros, *s5_args, batch=batch,
                                  rows_per_seq=seq // ROW_TOKENS, r=S5_BLOCK_ROWS,
                                  row_tokens=ROW_TOKENS, step_major=False, name="s5_prompt")
    steps = t_new // ROW_TOKENS
    pack = lambda v: v.reshape(dec_batch, steps, -1).transpose(1, 0, 2).reshape(
        steps * dec_batch, -1)
    unpack = lambda v: v.reshape(steps, dec_batch, -1).transpose(1, 0, 2).reshape(ts, d)
    xs, s_re, s_im = _s5_rows_mix(pack(xs), row(norm_mix[1]),
                                  state_ssm_re[0].reshape(dec_batch, n_state),
                                  state_ssm_im[0].reshape(dec_batch, n_state), *s5_args,
                                  batch=dec_batch, rows_per_seq=steps, r=None,
                                  row_tokens=ROW_TOKENS, step_major=True, name="s5_sample")
    xs = unpack(xs)
    yp = _ffn_final(xp, row(norm_ffn[1]), w_ffn_in_b, w_ffn_out_b, row(norm_final), layer=1,
                    tm=tm_prompt, ff_chunk=ff_chunk, row_tokens=ROW_TOKENS,
                    name="ffn_final_prompt")
    ys = _ffn_final(xs, row(norm_ffn[1]), w_ffn_in_b, w_ffn_out_b, row(norm_final), layer=1,
                    tm=ts, ff_chunk=ff_chunk, row_tokens=1, name="ffn_final_sample")

    st = lambda s, n: s.reshape(1, n, groups, n_p)
    return (yp.reshape(batch, seq, d), ys.reshape(dec_batch, t_new, d),
            k_p.reshape(1, batch, seq, heads, HEAD_DIM), vb_p.reshape(1, batch, seq, heads, HEAD_DIM),
            k_s.reshape(1, dec_batch, t_new, heads, HEAD_DIM),
            vb_s.reshape(1, dec_batch, t_new, heads, HEAD_DIM),
            vn_s.reshape(1, dec_batch, t_new, a_width),
            st(p_re, batch), st(p_im, batch), st(s_re, dec_batch), st(s_im, dec_batch))
```

```python
import functools
import math

import jax
import jax.numpy as jnp
from jax import lax
from jax.experimental import pallas as pl
from jax.experimental.pallas import tpu as pltpu

F32 = jnp.float32
BF16 = jnp.bfloat16

EPS = 1e-6
LANES = 128
SUBLANES = 8
V7X_VMEM_BYTES = 64 * 1024 * 1024
VMEM_LIMIT = V7X_VMEM_BYTES * 7 // 8
MASK_VALUE = -0.7 * float(jnp.finfo(jnp.float32).max)

CHUNK = 128
A_GROUPS = 4
HEAD_DIM = 128
QK_HALF = 64
LOG2E = math.log2(math.e)
SSM_GROUP = 16
SLAB_GROUPS = LANES // SSM_GROUP
SCAN_SHIFTS = (1, 2, 4)
ROW_TOKENS = 4
S5_BLOCK_ROWS = 128

def _const_spec(shape):
    zeros = (0,) * len(shape)
    return pl.BlockSpec(shape, lambda *_: zeros, pipeline_mode=pl.Buffered(1))


def _layer_spec(shape, layer):
    index = (layer,) + (0,) * (len(shape) - 1)
    return pl.BlockSpec((None,) + tuple(shape[1:]), lambda *_: index, pipeline_mode=pl.Buffered(1))


def _rms(x, g):
    return x * lax.rsqrt(jnp.mean(x * x, axis=-1, keepdims=True) + EPS) * g


def _params(semantics):
    return pltpu.CompilerParams(dimension_semantics=semantics, vmem_limit_bytes=VMEM_LIMIT)


def _even_in_kernel(x_ref, g_ref, w_ref, sn_ref, mix_ref, bias_ref, *outs, prompt, a_width):
    if prompt:
        a_ref, q_ref, k_ref, vb_ref, kb_ref, vbb_ref = outs
    else:
        a_ref, q_ref, k_ref, vb_ref, vn_ref = outs
    tm = x_ref.shape[0]
    rows_per_mix = mix_ref.shape[1]
    gw = a_width // A_GROUPS
    h = _rms(x_ref[...], g_ref[...]).astype(BF16)

    def proj(c0, width):
        return jnp.dot(h, w_ref[:, c0:c0 + width], preferred_element_type=F32)

    u = jax.nn.gelu(proj(0, a_width))
    v = _rms(jax.nn.gelu(proj(a_width, a_width)), sn_ref[...])
    if not prompt:
        vn_ref[...] = v
    v16 = v.astype(BF16)
    o = 2 * a_width
    bw = q_ref.shape[1]
    q_ref[...] = (proj(o, bw) * (QK_HALF ** -0.5 * LOG2E)).astype(q_ref.dtype)
    k = proj(o + bw, bw)
    vb = proj(o + 2 * bw, bw)
    heads = bw // HEAD_DIM
    for hd in range(heads):
        k_ref[pl.ds(hd, tm, stride=heads), :] = k[:, hd * HEAD_DIM:(hd + 1) * HEAD_DIM]
        vb_ref[pl.ds(hd, tm, stride=heads), :] = vb[:, hd * HEAD_DIM:(hd + 1) * HEAD_DIM]
    if prompt:
        kb_ref[...] = k.astype(BF16)
        vbb_ref[...] = vb.astype(BF16)
    for c in range(tm // rows_per_mix):
        r0 = c * rows_per_mix
        for g in range(A_GROUPS):
            c0 = g * gw
            gate = jnp.dot(mix_ref[g], v16[r0:r0 + rows_per_mix, c0:c0 + gw],
                           preferred_element_type=F32) + bias_ref[:, c0:c0 + gw]
            a_ref[r0:r0 + rows_per_mix, c0:c0 + gw] = (
                u[r0:r0 + rows_per_mix, c0:c0 + gw] * gate).astype(a_ref.dtype)


def _even_in(x, g, w, sn, mix, bias, *, tm, prompt):
    t, d = x.shape
    a_width = sn.shape[1]
    bw = (w.shape[1] - 2 * a_width) // 3
    act = BF16 if prompt else F32
    row = lambda width: pl.BlockSpec((tm, width), lambda i: (i, 0))
    heads = bw // HEAD_DIM
    head_rows = pl.BlockSpec((tm * heads, HEAD_DIM), lambda i: (i, 0))
    out_shape = [jax.ShapeDtypeStruct((t, a_width), act), jax.ShapeDtypeStruct((t, bw), act),
                 jax.ShapeDtypeStruct((t * heads, HEAD_DIM), F32),
                 jax.ShapeDtypeStruct((t * heads, HEAD_DIM), F32)]
    out_specs = [row(a_width), row(bw), head_rows, head_rows]
    if prompt:
        out_shape += [jax.ShapeDtypeStruct((t, bw), BF16)] * 2
        out_specs += [row(bw), row(bw)]
    else:
        out_shape += [jax.ShapeDtypeStruct((t, a_width), F32)]
        out_specs += [row(a_width)]
    return pl.pallas_call(
        functools.partial(_even_in_kernel, prompt=prompt, a_width=a_width),
        grid=(t // tm,),
        in_specs=[row(d), _const_spec(g.shape), _const_spec(w.shape), _const_spec(sn.shape),
                  _const_spec(mix.shape), _const_spec(bias.shape)],
        out_specs=out_specs,
        out_shape=out_shape,
        compiler_params=_params(("parallel",)),
        name="even_in_prompt" if prompt else "even_in_sample",
    )(x, g, w, sn, mix, bias)


def _split_halves(q):
    lane = lax.broadcasted_iota(jnp.int32, q.shape, 1)
    zero = jnp.zeros_like(q)
    return jnp.where(lane < QK_HALF, q, zero), jnp.where(lane >= QK_HALF, q, zero)


def _softmax_step(s, vblk, m_ref, l_ref, acc_ref, rows):
    tiles = [s[:, t * LANES:(t + 1) * LANES] for t in range(s.shape[1] // LANES)]
    m_prev = m_ref[rows, :]
    m_next = jnp.maximum(
        m_prev, jnp.max(functools.reduce(jnp.maximum, tiles), axis=1, keepdims=True))
    p_tiles = [jnp.exp2(t - m_next) for t in tiles]
    alpha = jnp.exp2(m_prev - m_next)
    l_ref[rows, :] = alpha * l_ref[rows, :] + functools.reduce(jnp.add, p_tiles)
    p = jnp.concatenate([t.astype(BF16) for t in p_tiles], axis=1)
    acc_ref[rows, :] = alpha * acc_ref[rows, :] + jnp.dot(p, vblk, preferred_element_type=F32)
    m_ref[rows, :] = m_next


def _row_sum(l):
    return jnp.sum(l, axis=1, keepdims=True)


def _diff_lambda(lq1, lk1, lq2, lk2, lam_init):
    e1 = jnp.exp(jnp.sum(lq1[...] * lk1[...], axis=1, keepdims=True))
    e2 = jnp.exp(jnp.sum(lq2[...] * lk2[...], axis=1, keepdims=True))
    return e1 - e2 + lam_init


def _head_out(o1, o2, lam, sub, lam_init):
    o = o1 - lam * o2
    return _rms(o, sub) * (1.0 - lam_init)


def _prompt_attn_step(i, q_ref, k_ref, v_ref, lam, sub_ref, o_ref, qs_ref, m_ref, l_ref, acc_ref,
                      lam_init):
    tq = q_ref.shape[0]
    q1, q2 = _split_halves(q_ref[...])
    qs_ref[0:tq, :] = q1
    qs_ref[tq:2 * tq, :] = q2
    m_ref[...] = jnp.full(m_ref.shape, MASK_VALUE, F32)
    l_ref[...] = jnp.zeros(l_ref.shape, F32)
    acc_ref[...] = jnp.zeros(acc_ref.shape, F32)
    all_rows = slice(0, 2 * tq)

    def scores(j):
        kv_rows = pl.ds(pl.multiple_of(j * tq, tq), tq)
        s = lax.dot_general(qs_ref[...], k_ref[kv_rows, :], (((1,), (1,)), ((), ())),
                            preferred_element_type=F32)
        return s, v_ref[kv_rows, :]

    def full_block(j):
        s, vblk = scores(j)
        _softmax_step(s, vblk, m_ref, l_ref, acc_ref, all_rows)

    def full_pair(jj, carry):
        full_block(2 * jj)
        full_block(2 * jj + 1)
        return carry

    lax.fori_loop(0, i // 2, full_pair, 0)

    @pl.when(i % 2 == 1)
    def _():
        full_block(i - 1)
    s, vblk = scores(i)
    row = lax.broadcasted_iota(jnp.int32, s.shape, 0)
    col = lax.broadcasted_iota(jnp.int32, s.shape, 1)
    qpos = jnp.where(row >= tq, row - tq, row)
    s = jnp.where(qpos >= col, s, MASK_VALUE)
    _softmax_step(s, vblk, m_ref, l_ref, acc_ref, all_rows)

    o1 = acc_ref[0:tq, :] / _row_sum(l_ref[0:tq, :])
    o2 = acc_ref[tq:2 * tq, :] / _row_sum(l_ref[tq:2 * tq, :])
    o_ref[...] = _head_out(o1, o2, lam, sub_ref[...], lam_init).astype(o_ref.dtype)


def _sample_attn_step(j, is_last, q_ref, kn_ref, vn_ref, k_pages, v_pages, lam, sub_ref, o_ref,
                      qs_ref, bias_ref, m_ref, l_ref, acc_ref, heads, lam_init):
    pages_per_step = len(k_pages)
    t_new = q_ref.shape[0]
    rows_per_head = 2 * t_new
    all_rows = slice(0, heads * rows_per_head)

    def head_match(shape):
        row = lax.broadcasted_iota(jnp.int32, shape, 0)
        col = lax.broadcasted_iota(jnp.int32, shape, 1)
        return row, col, lax.rem(col, heads) == row // rows_per_head

    @pl.when(j == 0)
    def _():
        pieces = []
        for h in range(heads):
            pieces += _split_halves(q_ref[:, h * HEAD_DIM:(h + 1) * HEAD_DIM])
        qs_ref[...] = jnp.concatenate(pieces, axis=0).astype(BF16)
        _, _, ok = head_match(bias_ref.shape)
        bias_ref[...] = jnp.where(ok, 0.0, MASK_VALUE)
        m_ref[...] = jnp.full(m_ref.shape, MASK_VALUE, F32)
        l_ref[...] = jnp.zeros(l_ref.shape, F32)
        acc_ref[...] = jnp.zeros(acc_ref.shape, F32)

    def scores(kblk):
        return lax.dot_general(qs_ref[...], kblk.astype(BF16), (((1,), (1,)), ((), ())),
                               preferred_element_type=F32)

    kblk = jnp.concatenate([k_pages[p][...] for p in range(pages_per_step)], axis=0)
    vblk = jnp.concatenate([v_pages[p][...] for p in range(pages_per_step)], axis=0)
    _softmax_step(scores(kblk) + bias_ref[...], vblk.astype(BF16), m_ref, l_ref, acc_ref, all_rows)

    @pl.when(is_last)
    def _():
        pad = jnp.zeros((LANES - kn_ref.shape[0], HEAD_DIM), F32)
        s = scores(jnp.concatenate([kn_ref[...], pad], axis=0))
        row, col, ok = head_match(s.shape)
        s = jnp.where(ok, s, MASK_VALUE)
        s = jnp.where(col // heads <= lax.rem(row, t_new), s, MASK_VALUE)
        vblk = jnp.concatenate([vn_ref[...], pad], axis=0)
        _softmax_step(s, vblk.astype(BF16), m_ref, l_ref, acc_ref, all_rows)
        for h in range(heads):
            r0 = h * rows_per_head
            o1 = acc_ref[r0:r0 + t_new, :] / _row_sum(l_ref[r0:r0 + t_new, :])
            o2 = (acc_ref[r0 + t_new:r0 + rows_per_head, :]
                  / _row_sum(l_ref[r0 + t_new:r0 + rows_per_head, :]))
            o_ref[:, h * HEAD_DIM:(h + 1) * HEAD_DIM] = _head_out(
                o1, o2, lam, sub_ref[...], lam_init)


def _attention_kernel(pt_ref, qp_ref, kp_ref, vp_ref, qs_in_ref, kn_ref, vn_ref, *rest,
                      pages_per_step, heads, sample_steps, lam_init):
    del pt_ref
    k_pages = rest[:pages_per_step]
    v_pages = rest[pages_per_step:2 * pages_per_step]
    (lq1, lk1, lq2, lk2, sub_ref, op_ref, os_ref,
     pq_s, pm_s, pl_s, pacc_s, sq_s, sbias_s, sm_s, sl_s, sacc_s) = rest[2 * pages_per_step:]
    i = pl.program_id(2)
    step = (pl.program_id(0) * pl.num_programs(1) + pl.program_id(1)) * pl.num_programs(2) + i
    lam = _diff_lambda(lq1, lk1, lq2, lk2, lam_init)
    _prompt_attn_step(i, qp_ref, kp_ref, vp_ref, lam, sub_ref, op_ref, pq_s, pm_s, pl_s, pacc_s,
                      lam_init)
    j = lax.rem(step, sample_steps)
    _sample_attn_step(j, j == sample_steps - 1, qs_in_ref, kn_ref, vn_ref, k_pages, v_pages, lam,
                      sub_ref, os_ref, sq_s, sbias_s, sm_s, sl_s, sacc_s, heads, lam_init)


def _attention(page_table, q_p, kb_p, vb_p, q_s, k_new, v_new, cache_k, cache_v, layer,
               lq1, lk1, lq2, lk2, sub, *, batch, seq, tq, t_new, lam_init):
    tp, bw = q_p.shape
    ts = q_s.shape[0]
    heads = bw // HEAD_DIM
    nq = seq // tq
    dec_batch, n_pages = page_table.shape
    page_rows = cache_k.shape[2]
    total_steps = batch * heads * nq
    assert total_steps % dec_batch == 0 and n_pages % (total_steps // dec_batch) == 0
    sample_steps = total_steps // dec_batch
    pages_per_step = n_pages // sample_steps
    pt_flat = page_table.reshape(-1)

    def sample_pos(b, h, i):
        step = (b * heads + h) * nq + i
        return step // sample_steps, step % sample_steps

    def page_spec(p):
        def index(b, h, i, pt):
            s, j = sample_pos(b, h, i)
            return (layer, pt[s * n_pages + j * pages_per_step + p], 0, 0)
        return pl.BlockSpec((None, None, page_rows, HEAD_DIM), index)

    seq_block = lambda shape: pl.BlockSpec(shape, lambda b, h, i, pt: (sample_pos(b, h, i)[0], 0))
    vec = lambda a: pl.BlockSpec(a.shape, lambda b, h, i, pt: (0, 0))
    qp_spec = pl.BlockSpec((tq, HEAD_DIM), lambda b, h, i, pt: (b * nq + i, h))
    kv_spec = pl.BlockSpec((seq, HEAD_DIM), lambda b, h, i, pt: (b, h))
    rows_s = heads * 2 * t_new
    grid_spec = pltpu.PrefetchScalarGridSpec(
        num_scalar_prefetch=1,
        grid=(batch, heads, nq),
        in_specs=([qp_spec, kv_spec, kv_spec,
                   seq_block((t_new, bw)), seq_block((t_new * heads, HEAD_DIM)),
                   seq_block((t_new * heads, HEAD_DIM))]
                  + [page_spec(p) for p in range(pages_per_step)] * 2
                  + [vec(lq1), vec(lk1), vec(lq2), vec(lk2), vec(sub)]),
        out_specs=[qp_spec, seq_block((t_new, bw))],
        scratch_shapes=[pltpu.VMEM((2 * tq, HEAD_DIM), BF16),
                        pltpu.VMEM((2 * tq, LANES), F32),
                        pltpu.VMEM((2 * tq, LANES), F32),
                        pltpu.VMEM((2 * tq, HEAD_DIM), F32),
                        pltpu.VMEM((rows_s, HEAD_DIM), BF16),
                        pltpu.VMEM((rows_s, pages_per_step * page_rows), F32),
                        pltpu.VMEM((rows_s, LANES), F32),
                        pltpu.VMEM((rows_s, LANES), F32),
                        pltpu.VMEM((rows_s, HEAD_DIM), F32)])
    return pl.pallas_call(
        functools.partial(_attention_kernel, pages_per_step=pages_per_step, heads=heads,
                          sample_steps=sample_steps, lam_init=lam_init),
        grid_spec=grid_spec,
        out_shape=[jax.ShapeDtypeStruct((tp, bw), BF16), jax.ShapeDtypeStruct((ts, bw), F32)],
        compiler_params=_params(("arbitrary", "arbitrary", "arbitrary")),
        name="attention",
    )(pt_flat, q_p, kb_p, vb_p, q_s, k_new, v_new,
      *([cache_k] * pages_per_step), *([cache_v] * pages_per_step), lq1, lk1, lq2, lk2, sub)


def _swiglu_residual(x1, g_ref, w_in_ref, w_out_ref, ff_chunk):
    d_ff = w_out_ref.shape[0]
    h = _rms(x1, g_ref[...]).astype(BF16)
    acc = x1
    for c0 in range(0, d_ff, ff_chunk):
        gate = jnp.dot(h, w_in_ref[:, c0:c0 + ff_chunk], preferred_element_type=F32)
        up = jnp.dot(h, w_in_ref[:, d_ff + c0:d_ff + c0 + ff_chunk], preferred_element_type=F32)
        act = (gate * jax.nn.sigmoid(gate) * up).astype(BF16)
        acc = acc + jnp.dot(act, w_out_ref[c0:c0 + ff_chunk, :], preferred_element_type=F32)
    return acc


def _mix_ffn_kernel(x_ref, a_ref, b_ref, wo_ref, g_ref, w_in_ref, w_out_ref, o_ref, *scratch,
                    ff_chunk, row_tokens):
    tm, d = x_ref.shape
    ab = jnp.concatenate([a_ref[...].astype(BF16), b_ref[...].astype(BF16)], axis=1)
    x1 = x_ref[...] + jnp.dot(ab, wo_ref[...], preferred_element_type=F32)
    y = _swiglu_residual(x1, g_ref, w_in_ref, w_out_ref, ff_chunk)
    if row_tokens == 1:
        o_ref[...] = y
        return
    slabs = scratch[0]
    for j in range(d // LANES):
        slabs[j] = y[:, j * LANES:(j + 1) * LANES]
    for t in range(row_tokens):
        for j in range(d // LANES):
            o_ref[:, t * d + j * LANES:t * d + (j + 1) * LANES] = (
                slabs[j, pl.ds(t, tm // row_tokens, stride=row_tokens), :])


def _ffn_final_kernel(x_ref, g_ref, w_in_ref, w_out_ref, gf_ref, o_ref, *scratch,
                      ff_chunk, row_tokens):
    if row_tokens == 1:
        x = x_ref[...]
    else:
        r = x_ref.shape[0]
        d = x_ref.shape[1] // row_tokens
        slabs = scratch[0]
        for t in range(row_tokens):
            for j in range(d // LANES):
                slabs[j, pl.ds(t, r, stride=row_tokens), :] = (
                    x_ref[:, t * d + j * LANES:t * d + (j + 1) * LANES])
        x = jnp.concatenate([slabs[j] for j in range(d // LANES)], axis=1)
    o_ref[...] = _rms(_swiglu_residual(x, g_ref, w_in_ref, w_out_ref, ff_chunk), gf_ref[...])


def _mix_ffn(x, a, b, wo, g, w_in, w_out, *, layer, tm, ff_chunk, row_tokens, name):
    t, d = x.shape
    row = lambda arr: pl.BlockSpec((tm, arr.shape[1]), lambda i: (i, 0))
    packed = row_tokens > 1
    return pl.pallas_call(
        functools.partial(_mix_ffn_kernel, ff_chunk=ff_chunk, row_tokens=row_tokens),
        grid=(t // tm,),
        in_specs=[row(x), row(a), row(b), _const_spec(wo.shape), _const_spec(g.shape),
                  _layer_spec(w_in.shape, layer), _layer_spec(w_out.shape, layer)],
        out_specs=pl.BlockSpec((tm // row_tokens, row_tokens * d), lambda i: (i, 0)),
        out_shape=jax.ShapeDtypeStruct((t // row_tokens, row_tokens * d), F32),
        scratch_shapes=[pltpu.VMEM((d // LANES, tm, LANES), F32)] if packed else [],
        compiler_params=_params(("parallel",)),
        name=name,
    )(x, a, b, wo, g, w_in, w_out)


def _ffn_final(x, g, w_in, w_out, gf, *, layer, tm, ff_chunk, row_tokens, name):
    rows, width = x.shape
    d = width // row_tokens
    t = rows * row_tokens
    packed = row_tokens > 1
    return pl.pallas_call(
        functools.partial(_ffn_final_kernel, ff_chunk=ff_chunk, row_tokens=row_tokens),
        grid=(t // tm,),
        in_specs=[pl.BlockSpec((tm // row_tokens, width), lambda i: (i, 0)),
                  _const_spec(g.shape), _layer_spec(w_in.shape, layer),
                  _layer_spec(w_out.shape, layer), _const_spec(gf.shape)],
        out_specs=pl.BlockSpec((tm, d), lambda i: (i, 0)),
        out_shape=jax.ShapeDtypeStruct((t, d), F32),
        scratch_shapes=[pltpu.VMEM((d // LANES, tm, LANES), F32)] if packed else [],
        compiler_params=_params(("parallel",)),
        name=name,
    )(x, g, w_in, w_out, gf)


def _expand_groups(compact, r1, c1):
    n_slabs, rows, k = compact.shape
    width = (k // c1) * SLAB_GROUPS * c1
    src = lax.broadcasted_iota(jnp.int32, (k, width), 0)
    dst = lax.broadcasted_iota(jnp.int32, (k, width), 1)
    select = (src == (dst // (SLAB_GROUPS * c1)) * c1 + dst % c1).astype(F32)
    full = jnp.dot(compact.reshape(n_slabs * rows, k), select,
                   precision=lax.Precision.HIGHEST).reshape(n_slabs, rows, width)
    row_g = (lax.broadcasted_iota(jnp.int32, full.shape, 1) // r1) % SLAB_GROUPS
    col_h = (lax.broadcasted_iota(jnp.int32, full.shape, 2) // c1) % SLAB_GROUPS
    return jnp.where(row_g == col_h, full, 0.0).astype(BF16)


def _s5_rows_kernel(x_ref, g_ref, s0re_ref, s0im_ref, coef_ref, wb_ref, wy_ref, wc_ref, d_ref,
                    wglu_ref, bglu_ref, o_ref, sre_ref, sim_ref,
                    re_s, im_s, gel_s, car_re, car_im, *, row_tokens, n_seq):
    r, width = x_ref.shape
    d = width // row_tokens
    n_slabs = d // LANES
    slab_state = re_s.shape[1] // n_slabs

    xs = [x_ref[:, t * d:(t + 1) * d] for t in range(row_tokens)]
    hs = [_rms(x, g_ref[...]) for x in xs]
    hb = [h.astype(BF16) for h in hs]

    def slab_lhs(j):
        return jnp.concatenate([h[:, j * LANES:(j + 1) * LANES] for h in hb], axis=1)

    for j in range(n_slabs):
        w = jnp.dot(slab_lhs(j), wb_ref[j], preferred_element_type=F32)
        re_s[:, j * slab_state:(j + 1) * slab_state] = w[:, :slab_state]
        im_s[:, j * slab_state:(j + 1) * slab_state] = w[:, slab_state:]

    def row_tile(i, carry):
        cr, ci = carry
        rows = pl.ds(pl.multiple_of(i * SUBLANES, SUBLANES), SUBLANES)
        wr = re_s[rows, :]
        wi = im_s[rows, :]
        for k, shift in enumerate(SCAN_SHIFTS):
            ar = coef_ref[2 * k]
            ai = coef_ref[2 * k + 1]
            sr = pltpu.roll(wr, shift, 0)
            si = pltpu.roll(wi, shift, 0)
            wr, wi = wr + ar * sr - ai * si, wi + ar * si + ai * sr
        pr = coef_ref[2 * len(SCAN_SHIFTS)]
        pi = coef_ref[2 * len(SCAN_SHIFTS) + 1]
        wr, wi = wr + pr * cr - pi * ci, wi + pr * ci + pi * cr
        first = lax.broadcasted_iota(jnp.int32, wr.shape, 0) == 0
        re_s[rows, :] = jnp.where(first, cr, pltpu.roll(wr, 1, 0))
        im_s[rows, :] = jnp.where(first, ci, pltpu.roll(wi, 1, 0))
        return wr[SUBLANES - 1:SUBLANES, :], wi[SUBLANES - 1:SUBLANES, :]

    if n_seq == 1:
        @pl.when(pl.program_id(1) == 0)
        def _():
            car_re[...] = s0re_ref[...]
            car_im[...] = s0im_ref[...]

        cr, ci = lax.fori_loop(0, r // SUBLANES, row_tile, (car_re[...], car_im[...]))
        car_re[...] = cr
        car_im[...] = ci
    else:
        a_re = coef_ref[0][1:2, :]
        a_im = coef_ref[1][1:2, :]
        cr, ci = s0re_ref[...], s0im_ref[...]
        for n in range(r // n_seq):
            rows = slice(n * n_seq, (n + 1) * n_seq)
            wr, wi = re_s[rows, :], im_s[rows, :]
            re_s[rows, :] = cr
            im_s[rows, :] = ci
            cr, ci = a_re * cr - a_im * ci + wr, a_re * ci + a_im * cr + wi
    sre_ref[...] = cr
    sim_ref[...] = ci

    for j in range(n_slabs):
        cols = slice(j * LANES, (j + 1) * LANES)
        st = slice(j * slab_state, (j + 1) * slab_state)
        s_cat = jnp.concatenate([re_s[:, st], im_s[:, st]], axis=1).astype(BF16)
        y = (jnp.dot(slab_lhs(j), wy_ref[j], preferred_element_type=F32)
             + jnp.dot(s_cat, wc_ref[j], preferred_element_type=F32))
        for t in range(row_tokens):
            yt = y[:, t * LANES:(t + 1) * LANES] + d_ref[:, cols] * hs[t][:, cols]
            gel_s[t * r:(t + 1) * r, cols] = jax.nn.gelu(yt).astype(BF16)
    z = jnp.dot(gel_s[...], wglu_ref[...], preferred_element_type=F32) + bglu_ref[...]
    for t in range(row_tokens):
        zt = z[t * r:(t + 1) * r, :]
        o_ref[:, t * d:(t + 1) * d] = xs[t] + zt[:, :d] * jax.nn.sigmoid(zt[:, d:])


def _s5_rows_mix(x, g, s0re, s0im, coef, wb, wy, wc, dskip, wglu, bglu, *, batch, rows_per_seq,
                 r, row_tokens, step_major, name):
    rows, width = x.shape
    d = width // row_tokens
    n_state = s0re.shape[-1]
    if step_major:
        grid = (1, 1)
        r = rows
        state_spec = pl.BlockSpec((batch, n_state), lambda b, i: (0, 0))
        x_spec = pl.BlockSpec((rows, width), lambda b, i: (0, 0))
    else:
        nt = rows_per_seq // r
        grid = (batch, nt)
        state_spec = pl.BlockSpec((None, 1, n_state), lambda b, i: (b, 0, 0))
        x_spec = pl.BlockSpec((r, width), lambda b, i: (b * nt + i, 0))
    consts = (g, coef, wb, wy, wc, dskip, wglu, bglu)
    x1, sre, sim = pl.pallas_call(
        functools.partial(_s5_rows_kernel, row_tokens=row_tokens,
                          n_seq=batch if step_major else 1),
        grid=grid,
        in_specs=[x_spec, _const_spec(g.shape), state_spec, state_spec]
                 + [_const_spec(c.shape) for c in consts[1:]],
        out_specs=[x_spec, state_spec, state_spec],
        out_shape=[jax.ShapeDtypeStruct((rows, width), F32),
                   jax.ShapeDtypeStruct(s0re.shape, F32),
                   jax.ShapeDtypeStruct(s0im.shape, F32)],
        scratch_shapes=[pltpu.VMEM((r, n_state), F32), pltpu.VMEM((r, n_state), F32),
                        pltpu.VMEM((r * row_tokens, d), BF16),
                        pltpu.VMEM((1, n_state), F32), pltpu.VMEM((1, n_state), F32)],
        compiler_params=_params(("parallel", "arbitrary")),
        name=name,
    )(x, g, s0re, s0im, *consts[1:])
    return x1, sre, sim


def _s5_row_tables(a_re, a_im, log_dt, b_re, b_im, c_re, c_im, row_tokens):
    groups, n_p = a_re.shape
    p_tok = row_tokens
    n_slabs = groups // SLAB_GROUPS
    dt = jnp.exp(log_dt)[:, None]
    ks = jnp.arange(p_tok * SUBLANES + 1, dtype=F32)[:, None, None]
    mag = jnp.exp((a_re * dt)[None] * ks)
    ang = (a_im * dt)[None] * ks
    pw_re, pw_im = mag * jnp.cos(ang), mag * jnp.sin(ang)
    den = a_re * a_re + a_im * a_im
    q_re = ((pw_re[1] - 1.0) * a_re + pw_im[1] * a_im) / den
    q_im = (pw_im[1] * a_re - (pw_re[1] - 1.0) * a_im) / den
    bb_re = q_re[:, :, None] * b_re - q_im[:, :, None] * b_im
    bb_im = q_re[:, :, None] * b_im + q_im[:, :, None] * b_re

    rows = jnp.arange(SUBLANES)
    coef = []
    for shift in SCAN_SHIFTS:
        keep = (rows >= shift)[:, None].astype(F32)
        coef += [pw_re[p_tok * shift].reshape(1, -1) * keep,
                 pw_im[p_tok * shift].reshape(1, -1) * keep]
    pick = lambda pw, idx: jnp.stack([pw[k] for k in idx])
    carry = [p_tok * (r + 1) for r in range(SUBLANES)]
    coef += [pick(pw_re, carry).reshape(SUBLANES, -1), pick(pw_im, carry).reshape(SUBLANES, -1)]
    coef = jnp.stack(coef)

    back = [p_tok - 1 - s for s in range(p_tok)]
    er, ei = pick(pw_re, back)[:, :, :, None], pick(pw_im, back)[:, :, :, None]
    lay_b = lambda m: m.reshape(p_tok, n_slabs, SLAB_GROUPS, n_p, SSM_GROUP).transpose(
        1, 0, 2, 4, 3).reshape(n_slabs, -1, n_p)
    wb = _expand_groups(jnp.concatenate([lay_b(er * bb_re - ei * bb_im),
                                         lay_b(er * bb_im + ei * bb_re)], axis=2), SSM_GROUP, n_p)

    kr, ki = pw_re[:p_tok + 1][:, :, None, :], pw_im[:p_tok + 1][:, :, None, :]
    ck_re, ck_im = c_re * kr - c_im * ki, c_re * ki + c_im * kr

    lay_c = lambda m: m.reshape(p_tok, n_slabs, SLAB_GROUPS, SSM_GROUP, n_p).transpose(
        1, 2, 4, 0, 3).reshape(n_slabs, SLAB_GROUPS * n_p, -1)
    wc = _expand_groups(jnp.concatenate([lay_c(ck_re[1:]), -lay_c(ck_im[1:])], axis=1),
                        n_p, SSM_GROUP)

    bt_re, bt_im = bb_re.transpose(0, 2, 1)[None, :, :, None, :], bb_im.transpose(0, 2, 1)[
        None, :, :, None, :]
    lag = jnp.sum(ck_re[:p_tok, :, None] * bt_re - ck_im[:p_tok, :, None] * bt_im, axis=-1)
    lag = lag.reshape(p_tok, n_slabs, SLAB_GROUPS * SSM_GROUP, SSM_GROUP)
    zero = jnp.zeros_like(lag[0])
    toep = jnp.concatenate(
        [jnp.concatenate([lag[t - s] if t >= s else zero for t in range(p_tok)], axis=2)
         for s in range(p_tok)], axis=1)
    wy = _expand_groups(toep, SSM_GROUP, SSM_GROUP)
    return coef, wb, wy, wc


def kernel(x_prompt, x_sample, cache_k, cache_v, page_table, state_ssm_re, state_ssm_im,
           norm_mix, norm_ffn, norm_final, w_in_even, w_out_even, sgu_norm, sgu_w, sgu_b,
           lambda_q1, lambda_k1, lambda_q2, lambda_k2, attn_subln,
           ssm_a_re, ssm_a_im, ssm_log_dt, ssm_b_re, ssm_b_im, ssm_c_re, ssm_c_im, ssm_d,
           w_glu, b_glu, w_ffn_in, w_ffn_out):
    batch, seq, d = x_prompt.shape
    dec_batch, t_new, _ = x_sample.shape
    depth = norm_mix.shape[0]
    assert depth == 2 and seq % CHUNK == 0 and t_new == SUBLANES
    tp = batch * seq
    ts = dec_batch * t_new
    xp = x_prompt.reshape(tp, d)
    xs = x_sample.reshape(ts, d)
    tm_prompt = 512
    ff_chunk = 256
    row = lambda v: v.reshape(1, -1)

    lam_init = 0.8 - 0.6 * math.exp(-0.3 * 0)
    w_in0 = w_in_even[0].astype(BF16)
    w_out0 = w_out_even[0].astype(BF16)
    a_width = sgu_norm.shape[1]
    gw = a_width // A_GROUPS
    tril = jnp.tril(jnp.ones((CHUNK, CHUNK), bool))
    mix_p = jnp.where(tril, sgu_w[0], 0).astype(BF16)
    bias_p = jnp.repeat(sgu_b[0].T, gw, axis=1)
    small = jnp.where(tril[:t_new, :t_new], sgu_w[0][:, :t_new, :t_new], 0)
    row_i = lax.broadcasted_iota(jnp.int32, (ts, ts), 0)
    col_i = lax.broadcasted_iota(jnp.int32, (ts, ts), 1)
    rep = (row_i[:, :t_new] % t_new == col_i[:, :t_new]).astype(F32)
    tiled = jnp.einsum('at,gts,bs->gab', rep, small, rep, precision=lax.Precision.HIGHEST)
    mix_s = jnp.where(row_i // t_new == col_i // t_new, tiled, 0.0).astype(BF16)
    bias_s = jnp.tile(bias_p[:t_new], (dec_batch, 1))
    lam_vecs = [row(lambda_q1[0]), row(lambda_k1[0]), row(lambda_q2[0]), row(lambda_k2[0]),
                row(attn_subln[0])]

    a_p, q_p, k_p, vb_p, kb_p, vbb_p = _even_in(
        xp, row(norm_mix[0]), w_in0, row(sgu_norm[0]), mix_p, bias_p, tm=tm_prompt, prompt=True)
    a_s, q_s, k_s, vb_s, vn_s = _even_in(
        xs, row(norm_mix[0]), w_in0, row(sgu_norm[0]), mix_s, bias_s, tm=ts, prompt=False)
    n_layers, n_phys, page_size, heads, _ = cache_k.shape
    page_view = (n_layers, n_phys, page_size * heads, HEAD_DIM)
    b_p, b_s = _attention(page_table, q_p, kb_p, vbb_p, q_s, k_s, vb_s,
                          cache_k.reshape(page_view), cache_v.reshape(page_view), 0, *lam_vecs,
                          batch=batch, seq=seq, tq=512, t_new=t_new, lam_init=lam_init)
    w_ffn_in_b = w_ffn_in.astype(BF16)
    w_ffn_out_b = w_ffn_out.astype(BF16)
    xp = _mix_ffn(xp, a_p, b_p, w_out0, row(norm_ffn[0]), w_ffn_in_b, w_ffn_out_b, layer=0,
                  tm=tm_prompt, ff_chunk=ff_chunk, row_tokens=ROW_TOKENS, name="mix_ffn_prompt")
    xs = _mix_ffn(xs, a_s, b_s, w_out0, row(norm_ffn[0]), w_ffn_in_b, w_ffn_out_b, layer=0,
                  tm=ts, ff_chunk=ff_chunk, row_tokens=1, name="mix_ffn_sample")

    groups, n_p = ssm_a_re.shape[1:]
    n_state = groups * n_p
    ssm = (ssm_a_re[0], ssm_a_im[0], ssm_log_dt[0], ssm_b_re[0], ssm_b_im[0], ssm_c_re[0],
           ssm_c_im[0])
    w_glu0 = w_glu[0].astype(BF16)
    s5_args = (*_s5_row_tables(*ssm, ROW_TOKENS), row(ssm_d[0]), w_glu0, row(b_glu[0]))
    zeros = jnp.zeros((batch, 1, n_state), F32)
    xp, p_re, p_im = _s5_rows_mix(xp, row(norm_mix[1]), zeros, zeros, *s5_args, batch=batch,
                                  rows_per_seq=seq // ROW_TOKENS, r=S5_BLOCK_ROWS,
                                  row_tokens=ROW_TOKENS, step_major=False, name="s5_prompt")
    steps = t_new // ROW_TOKENS
    pack = lambda v: v.reshape(dec_batch, steps, -1).transpose(1, 0, 2).reshape(
        steps * dec_batch, -1)
    unpack = lambda v: v.reshape(steps, dec_batch, -1).transpose(1, 0, 2).reshape(ts, d)
    xs, s_re, s_im = _s5_rows_mix(pack(xs), row(norm_mix[1]),
                                  state_ssm_re[0].reshape(dec_batch, n_state),
                                  state_ssm_im[0].reshape(dec_batch, n_state), *s5_args,
                                  batch=dec_batch, rows_per_seq=steps, r=None,
                                  row_tokens=ROW_TOKENS, step_major=True, name="s5_sample")
    xs = unpack(xs)
    yp = _ffn_final(xp, row(norm_ffn[1]), w_ffn_in_b, w_ffn_out_b, row(norm_final), layer=1,
                    tm=tm_prompt, ff_chunk=ff_chunk, row_tokens=ROW_TOKENS,
                    name="ffn_final_prompt")
    ys = _ffn_final(xs, row(norm_ffn[1]), w_ffn_in_b, w_ffn_out_b, row(norm_final), layer=1,
                    tm=ts, ff_chunk=ff_chunk, row_tokens=1, name="ffn_final_sample")

    st = lambda s, n: s.reshape(1, n, groups, n_p)
    return (yp.reshape(batch, seq, d), ys.reshape(dec_batch, t_new, d),
            k_p.reshape(1, batch, seq, heads, HEAD_DIM), vb_p.reshape(1, batch, seq, heads, HEAD_DIM),
            k_s.reshape(1, dec_batch, t_new, heads, HEAD_DIM),
            vb_s.reshape(1, dec_batch, t_new, heads, HEAD_DIM),
            vn_s.reshape(1, dec_batch, t_new, a_width),
            st(p_re, batch), st(p_im, batch), st(s_re, dec_batch), st(s_im, dec_batch))
```

```python
import functools
import math

import jax
import jax.numpy as jnp
from jax import lax
from jax.experimental import pallas as pl
from jax.experimental.pallas import tpu as pltpu

F32 = jnp.float32
BF16 = jnp.bfloat16

EPS = 1e-6
LANES = 128
SUBLANES = 8
V7X_VMEM_BYTES = 64 * 1024 * 1024
VMEM_LIMIT = V7X_VMEM_BYTES * 7 // 8
MASK_VALUE = -0.7 * float(jnp.finfo(jnp.float32).max)

CHUNK = 128
A_GROUPS = 4
HEAD_DIM = 128
QK_HALF = 64
LOG2E = math.log2(math.e)
SSM_GROUP = 16
SLAB_GROUPS = LANES // SSM_GROUP
SCAN_SHIFTS = (1, 2, 4)
ROW_TOKENS = 4
S5_BLOCK_ROWS = 128

def _const_spec(shape):
    zeros = (0,) * len(shape)
    return pl.BlockSpec(shape, lambda *_: zeros, pipeline_mode=pl.Buffered(1))


def _layer_spec(shape, layer):
    index = (layer,) + (0,) * (len(shape) - 1)
    return pl.BlockSpec((None,) + tuple(shape[1:]), lambda *_: index, pipeline_mode=pl.Buffered(1))


def _rms(x, g):
    return x * lax.rsqrt(jnp.mean(x * x, axis=-1, keepdims=True) + EPS) * g


def _params(semantics):
    return pltpu.CompilerParams(dimension_semantics=semantics, vmem_limit_bytes=VMEM_LIMIT)


def _even_in_kernel(x_ref, g_ref, w_ref, sn_ref, mix_ref, bias_ref, *outs, prompt, a_width):
    if prompt:
        a_ref, q_ref, k_ref, vb_ref, kb_ref, vbb_ref = outs
    else:
        a_ref, q_ref, k_ref, vb_ref, vn_ref = outs
    tm = x_ref.shape[0]
    rows_per_mix = mix_ref.shape[1]
    gw = a_width // A_GROUPS
    h = _rms(x_ref[...], g_ref[...]).astype(BF16)

    def proj(c0, width):
        return jnp.dot(h, w_ref[:, c0:c0 + width], preferred_element_type=F32)

    u = jax.nn.gelu(proj(0, a_width))
    v = _rms(jax.nn.gelu(proj(a_width, a_width)), sn_ref[...])
    if not prompt:
        vn_ref[...] = v
    v16 = v.astype(BF16)
    o = 2 * a_width
    bw = q_ref.shape[1]
    q_ref[...] = (proj(o, bw) * (QK_HALF ** -0.5 * LOG2E)).astype(q_ref.dtype)
    k = proj(o + bw, bw)
    vb = proj(o + 2 * bw, bw)
    heads = bw // HEAD_DIM
    for hd in range(heads):
        k_ref[pl.ds(hd, tm, stride=heads), :] = k[:, hd * HEAD_DIM:(hd + 1) * HEAD_DIM]
        vb_ref[pl.ds(hd, tm, stride=heads), :] = vb[:, hd * HEAD_DIM:(hd + 1) * HEAD_DIM]
    if prompt:
        kb_ref[...] = k.astype(BF16)
        vbb_ref[...] = vb.astype(BF16)
    for c in range(tm // rows_per_mix):
        r0 = c * rows_per_mix
        for g in range(A_GROUPS):
            c0 = g * gw
            gate = jnp.dot(mix_ref[g], v16[r0:r0 + rows_per_mix, c0:c0 + gw],
                           preferred_element_type=F32) + bias_ref[:, c0:c0 + gw]
            a_ref[r0:r0 + rows_per_mix, c0:c0 + gw] = (
                u[r0:r0 + rows_per_mix, c0:c0 + gw] * gate).astype(a_ref.dtype)


def _even_in(x, g, w, sn, mix, bias, *, tm, prompt):
    t, d = x.shape
    a_width = sn.shape[1]
    bw = (w.shape[1] - 2 * a_width) // 3
    act = BF16 if prompt else F32
    row = lambda width: pl.BlockSpec((tm, width), lambda i: (i, 0))
    heads = bw // HEAD_DIM
    head_rows = pl.BlockSpec((tm * heads, HEAD_DIM), lambda i: (i, 0))
    out_shape = [jax.ShapeDtypeStruct((t, a_width), act), jax.ShapeDtypeStruct((t, bw), act),
                 jax.ShapeDtypeStruct((t * heads, HEAD_DIM), F32),
                 jax.ShapeDtypeStruct((t * heads, HEAD_DIM), F32)]
    out_specs = [row(a_width), row(bw), head_rows, head_rows]
    if prompt:
        out_shape += [jax.ShapeDtypeStruct((t, bw), BF16)] * 2
        out_specs += [row(bw), row(bw)]
    else:
        out_shape += [jax.ShapeDtypeStruct((t, a_width), F32)]
        out_specs += [row(a_width)]
    return pl.pallas_call(
        functools.partial(_even_in_kernel, prompt=prompt, a_width=a_width),
        grid=(t // tm,),
        in_specs=[row(d), _const_spec(g.shape), _const_spec(w.shape), _const_spec(sn.shape),
                  _const_spec(mix.shape), _const_spec(bias.shape)],
        out_specs=out_specs,
        out_shape=out_shape,
        compiler_params=_params(("parallel",)),
        name="even_in_prompt" if prompt else "even_in_sample",
    )(x, g, w, sn, mix, bias)


def _split_halves(q):
    lane = lax.broadcasted_iota(jnp.int32, q.shape, 1)
    zero = jnp.zeros_like(q)
    return jnp.where(lane < QK_HALF, q, zero), jnp.where(lane >= QK_HALF, q, zero)


def _softmax_step(s, vblk, m_ref, l_ref, acc_ref, rows):
    tiles = [s[:, t * LANES:(t + 1) * LANES] for t in range(s.shape[1] // LANES)]
    m_prev = m_ref[rows, :]
    m_next = jnp.maximum(
        m_prev, jnp.max(functools.reduce(jnp.maximum, tiles), axis=1, keepdims=True))
    p_tiles = [jnp.exp2(t - m_next) for t in tiles]
    alpha = jnp.exp2(m_prev - m_next)
    l_ref[rows, :] = alpha * l_ref[rows, :] + functools.reduce(jnp.add, p_tiles)
    p = jnp.concatenate([t.astype(BF16) for t in p_tiles], axis=1)
    acc_ref[rows, :] = alpha * acc_ref[rows, :] + jnp.dot(p, vblk, preferred_element_type=F32)
    m_ref[rows, :] = m_next


def _row_sum(l):
    return jnp.sum(l, axis=1, keepdims=True)


def _diff_lambda(lq1, lk1, lq2, lk2, lam_init):
    e1 = jnp.exp(jnp.sum(lq1[...] * lk1[...], axis=1, keepdims=True))
    e2 = jnp.exp(jnp.sum(lq2[...] * lk2[...], axis=1, keepdims=True))
    return e1 - e2 + lam_init


def _head_out(o1, o2, lam, sub, lam_init):
    o = o1 - lam * o2
    return _rms(o, sub) * (1.0 - lam_init)


def _prompt_attn_step(i, q_ref, k_ref, v_ref, lam, sub_ref, o_ref, qs_ref, m_ref, l_ref, acc_ref,
                      lam_init):
    tq = q_ref.shape[0]
    q1, q2 = _split_halves(q_ref[...])
    qs_ref[0:tq, :] = q1
    qs_ref[tq:2 * tq, :] = q2
    m_ref[...] = jnp.full(m_ref.shape, MASK_VALUE, F32)
    l_ref[...] = jnp.zeros(l_ref.shape, F32)
    acc_ref[...] = jnp.zeros(acc_ref.shape, F32)
    all_rows = slice(0, 2 * tq)

    def scores(j):
        kv_rows = pl.ds(pl.multiple_of(j * tq, tq), tq)
        s = lax.dot_general(qs_ref[...], k_ref[kv_rows, :], (((1,), (1,)), ((), ())),
                            preferred_element_type=F32)
        return s, v_ref[kv_rows, :]

    def full_block(j):
        s, vblk = scores(j)
        _softmax_step(s, vblk, m_ref, l_ref, acc_ref, all_rows)

    def full_pair(jj, carry):
        full_block(2 * jj)
        full_block(2 * jj + 1)
        return carry

    lax.fori_loop(0, i // 2, full_pair, 0)

    @pl.when(i % 2 == 1)
    def _():
        full_block(i - 1)
    s, vblk = scores(i)
    row = lax.broadcasted_iota(jnp.int32, s.shape, 0)
    col = lax.broadcasted_iota(jnp.int32, s.shape, 1)
    qpos = jnp.where(row >= tq, row - tq, row)
    s = jnp.where(qpos >= col, s, MASK_VALUE)
    _softmax_step(s, vblk, m_ref, l_ref, acc_ref, all_rows)

    o1 = acc_ref[0:tq, :] / _row_sum(l_ref[0:tq, :])
    o2 = acc_ref[tq:2 * tq, :] / _row_sum(l_ref[tq:2 * tq, :])
    o_ref[...] = _head_out(o1, o2, lam, sub_ref[...], lam_init).astype(o_ref.dtype)


def _sample_attn_step(j, is_last, q_ref, kn_ref, vn_ref, k_pages, v_pages, lam, sub_ref, o_ref,
                      qs_ref, m_ref, l_ref, acc_ref, heads, lam_init):
    pages_per_step = len(k_pages)
    t_new = q_ref.shape[0]
    rows_per_head = 2 * t_new
    page_size = k_pages[0].shape[0] // heads

    @pl.when(j == 0)
    def _():
        pieces = []
        for h in range(heads):
            pieces += _split_halves(q_ref[:, h * HEAD_DIM:(h + 1) * HEAD_DIM])
        qs_ref[...] = jnp.concatenate(pieces, axis=0).astype(BF16)
        m_ref[...] = jnp.full(m_ref.shape, MASK_VALUE, F32)
        l_ref[...] = jnp.zeros(l_ref.shape, F32)
        acc_ref[...] = jnp.zeros(acc_ref.shape, F32)

    def head_rows(h):
        return slice(h * rows_per_head, (h + 1) * rows_per_head)

    def scores(h, kblk):
        return lax.dot_general(qs_ref[head_rows(h), :], kblk.astype(BF16),
                               (((1,), (1,)), ((), ())), preferred_element_type=F32)

    def of_head(ref, h, n_tokens):
        return ref[pl.ds(h, n_tokens, stride=heads), :]

    for h in range(heads):
        kblk = jnp.concatenate([of_head(k_pages[p], h, page_size)
                                for p in range(pages_per_step)], axis=0)
        vblk = jnp.concatenate([of_head(v_pages[p], h, page_size)
                                for p in range(pages_per_step)], axis=0)
        _softmax_step(scores(h, kblk), vblk.astype(BF16), m_ref, l_ref, acc_ref, head_rows(h))

    @pl.when(is_last)
    def _():
        pad = jnp.zeros((LANES - t_new, HEAD_DIM), F32)
        for h in range(heads):
            s = scores(h, jnp.concatenate([of_head(kn_ref, h, t_new), pad], axis=0))
            row = lax.broadcasted_iota(jnp.int32, s.shape, 0)
            col = lax.broadcasted_iota(jnp.int32, s.shape, 1)
            s = jnp.where(col <= lax.rem(row, t_new), s, MASK_VALUE)
            vblk = jnp.concatenate([of_head(vn_ref, h, t_new), pad], axis=0)
            _softmax_step(s, vblk.astype(BF16), m_ref, l_ref, acc_ref, head_rows(h))
            r0 = h * rows_per_head
            o1 = acc_ref[r0:r0 + t_new, :] / _row_sum(l_ref[r0:r0 + t_new, :])
            o2 = (acc_ref[r0 + t_new:r0 + rows_per_head, :]
                  / _row_sum(l_ref[r0 + t_new:r0 + rows_per_head, :]))
            o_ref[:, h * HEAD_DIM:(h + 1) * HEAD_DIM] = _head_out(
                o1, o2, lam, sub_ref[...], lam_init)


def _attention_kernel(pt_ref, qp_ref, kp_ref, vp_ref, qs_in_ref, kn_ref, vn_ref, *rest,
                      pages_per_step, heads, sample_steps, lam_init):
    del pt_ref
    k_pages = rest[:pages_per_step]
    v_pages = rest[pages_per_step:2 * pages_per_step]
    (lq1, lk1, lq2, lk2, sub_ref, op_ref, os_ref,
     pq_s, pm_s, pl_s, pacc_s, sq_s, sm_s, sl_s, sacc_s) = rest[2 * pages_per_step:]
    i = pl.program_id(2)
    step = (pl.program_id(0) * pl.num_programs(1) + pl.program_id(1)) * pl.num_programs(2) + i
    lam = _diff_lambda(lq1, lk1, lq2, lk2, lam_init)
    _prompt_attn_step(i, qp_ref, kp_ref, vp_ref, lam, sub_ref, op_ref, pq_s, pm_s, pl_s, pacc_s,
                      lam_init)
    j = lax.rem(step, sample_steps)
    _sample_attn_step(j, j == sample_steps - 1, qs_in_ref, kn_ref, vn_ref, k_pages, v_pages, lam,
                      sub_ref, os_ref, sq_s, sm_s, sl_s, sacc_s, heads, lam_init)


def _attention(page_table, q_p, kb_p, vb_p, q_s, k_new, v_new, cache_k, cache_v, layer,
               lq1, lk1, lq2, lk2, sub, *, batch, seq, tq, t_new, lam_init):
    tp, bw = q_p.shape
    ts = q_s.shape[0]
    heads = bw // HEAD_DIM
    nq = seq // tq
    dec_batch, n_pages = page_table.shape
    page_rows = cache_k.shape[2]
    total_steps = batch * heads * nq
    assert total_steps % dec_batch == 0 and n_pages % (total_steps // dec_batch) == 0
    sample_steps = total_steps // dec_batch
    pages_per_step = n_pages // sample_steps
    pt_flat = page_table.reshape(-1)

    def sample_pos(b, h, i):
        step = (b * heads + h) * nq + i
        return step // sample_steps, step % sample_steps

    def page_spec(p):
        def index(b, h, i, pt):
            s, j = sample_pos(b, h, i)
            return (layer, pt[s * n_pages + j * pages_per_step + p], 0, 0)
        return pl.BlockSpec((None, None, page_rows, HEAD_DIM), index)

    seq_block = lambda shape: pl.BlockSpec(shape, lambda b, h, i, pt: (sample_pos(b, h, i)[0], 0))
    vec = lambda a: pl.BlockSpec(a.shape, lambda b, h, i, pt: (0, 0))
    qp_spec = pl.BlockSpec((tq, HEAD_DIM), lambda b, h, i, pt: (b * nq + i, h))
    kv_spec = pl.BlockSpec((seq, HEAD_DIM), lambda b, h, i, pt: (b, h))
    rows_s = heads * 2 * t_new
    grid_spec = pltpu.PrefetchScalarGridSpec(
        num_scalar_prefetch=1,
        grid=(batch, heads, nq),
        in_specs=([qp_spec, kv_spec, kv_spec,
                   seq_block((t_new, bw)), seq_block((t_new * heads, HEAD_DIM)),
                   seq_block((t_new * heads, HEAD_DIM))]
                  + [page_spec(p) for p in range(pages_per_step)] * 2
                  + [vec(lq1), vec(lk1), vec(lq2), vec(lk2), vec(sub)]),
        out_specs=[qp_spec, seq_block((t_new, bw))],
        scratch_shapes=[pltpu.VMEM((2 * tq, HEAD_DIM), BF16),
                        pltpu.VMEM((2 * tq, LANES), F32),
                        pltpu.VMEM((2 * tq, LANES), F32),
                        pltpu.VMEM((2 * tq, HEAD_DIM), F32),
                        pltpu.VMEM((rows_s, HEAD_DIM), BF16),
                        pltpu.VMEM((rows_s, LANES), F32),
                        pltpu.VMEM((rows_s, LANES), F32),
                        pltpu.VMEM((rows_s, HEAD_DIM), F32)])
    return pl.pallas_call(
        functools.partial(_attention_kernel, pages_per_step=pages_per_step, heads=heads,
                          sample_steps=sample_steps, lam_init=lam_init),
        grid_spec=grid_spec,
        out_shape=[jax.ShapeDtypeStruct((tp, bw), BF16), jax.ShapeDtypeStruct((ts, bw), F32)],
        compiler_params=_params(("arbitrary", "arbitrary", "arbitrary")),
        name="attention",
    )(pt_flat, q_p, kb_p, vb_p, q_s, k_new, v_new,
      *([cache_k] * pages_per_step), *([cache_v] * pages_per_step), lq1, lk1, lq2, lk2, sub)


def _swiglu_residual(x1, g_ref, w_in_ref, w_out_ref, ff_chunk):
    d_ff = w_out_ref.shape[0]
    h = _rms(x1, g_ref[...]).astype(BF16)
    acc = x1
    for c0 in range(0, d_ff, ff_chunk):
        gate = jnp.dot(h, w_in_ref[:, c0:c0 + ff_chunk], preferred_element_type=F32)
        up = jnp.dot(h, w_in_ref[:, d_ff + c0:d_ff + c0 + ff_chunk], preferred_element_type=F32)
        act = (gate * jax.nn.sigmoid(gate) * up).astype(BF16)
        acc = acc + jnp.dot(act, w_out_ref[c0:c0 + ff_chunk, :], preferred_element_type=F32)
    return acc


def _mix_ffn_kernel(x_ref, a_ref, b_ref, wo_ref, g_ref, w_in_ref, w_out_ref, o_ref, *scratch,
                    ff_chunk, row_tokens):
    tm, d = x_ref.shape
    ab = jnp.concatenate([a_ref[...].astype(BF16), b_ref[...].astype(BF16)], axis=1)
    x1 = x_ref[...] + jnp.dot(ab, wo_ref[...], preferred_element_type=F32)
    y = _swiglu_residual(x1, g_ref, w_in_ref, w_out_ref, ff_chunk)
    if row_tokens == 1:
        o_ref[...] = y
        return
    slabs = scratch[0]
    for j in range(d // LANES):
        slabs[j] = y[:, j * LANES:(j + 1) * LANES]
    for t in range(row_tokens):
        for j in range(d // LANES):
            o_ref[:, t * d + j * LANES:t * d + (j + 1) * LANES] = (
                slabs[j, pl.ds(t, tm // row_tokens, stride=row_tokens), :])


def _ffn_final_kernel(x_ref, g_ref, w_in_ref, w_out_ref, gf_ref, o_ref, *scratch,
                      ff_chunk, row_tokens):
    if row_tokens == 1:
        x = x_ref[...]
    else:
        r = x_ref.shape[0]
        d = x_ref.shape[1] // row_tokens
        slabs = scratch[0]
        for t in range(row_tokens):
            for j in range(d // LANES):
                slabs[j, pl.ds(t, r, stride=row_tokens), :] = (
                    x_ref[:, t * d + j * LANES:t * d + (j + 1) * LANES])
        x = jnp.concatenate([slabs[j] for j in range(d // LANES)], axis=1)
    o_ref[...] = _rms(_swiglu_residual(x, g_ref, w_in_ref, w_out_ref, ff_chunk), gf_ref[...])


def _mix_ffn(x, a, b, wo, g, w_in, w_out, *, layer, tm, ff_chunk, row_tokens, name):
    t, d = x.shape
    row = lambda arr: pl.BlockSpec((tm, arr.shape[1]), lambda i: (i, 0))
    packed = row_tokens > 1
    return pl.pallas_call(
        functools.partial(_mix_ffn_kernel, ff_chunk=ff_chunk, row_tokens=row_tokens),
        grid=(t // tm,),
        in_specs=[row(x), row(a), row(b), _const_spec(wo.shape), _const_spec(g.shape),
                  _layer_spec(w_in.shape, layer), _layer_spec(w_out.shape, layer)],
        out_specs=pl.BlockSpec((tm // row_tokens, row_tokens * d), lambda i: (i, 0)),
        out_shape=jax.ShapeDtypeStruct((t // row_tokens, row_tokens * d), F32),
        scratch_shapes=[pltpu.VMEM((d // LANES, tm, LANES), F32)] if packed else [],
        compiler_params=_params(("parallel",)),
        name=name,
    )(x, a, b, wo, g, w_in, w_out)


def _ffn_final(x, g, w_in, w_out, gf, *, layer, tm, ff_chunk, row_tokens, name):
    rows, width = x.shape
    d = width // row_tokens
    t = rows * row_tokens
    packed = row_tokens > 1
    return pl.pallas_call(
        functools.partial(_ffn_final_kernel, ff_chunk=ff_chunk, row_tokens=row_tokens),
        grid=(t // tm,),
        in_specs=[pl.BlockSpec((tm // row_tokens, width), lambda i: (i, 0)),
                  _const_spec(g.shape), _layer_spec(w_in.shape, layer),
                  _layer_spec(w_out.shape, layer), _const_spec(gf.shape)],
        out_specs=pl.BlockSpec((tm, d), lambda i: (i, 0)),
        out_shape=jax.ShapeDtypeStruct((t, d), F32),
        scratch_shapes=[pltpu.VMEM((d // LANES, tm, LANES), F32)] if packed else [],
        compiler_params=_params(("parallel",)),
        name=name,
    )(x, g, w_in, w_out, gf)


def _expand_groups(compact, r1, c1):
    n_slabs, rows, k = compact.shape
    width = (k // c1) * SLAB_GROUPS * c1
    src = lax.broadcasted_iota(jnp.int32, (k, width), 0)
    dst = lax.broadcasted_iota(jnp.int32, (k, width), 1)
    select = (src == (dst // (SLAB_GROUPS * c1)) * c1 + dst % c1).astype(F32)
    full = jnp.dot(compact.reshape(n_slabs * rows, k), select,
                   precision=lax.Precision.HIGHEST).reshape(n_slabs, rows, width)
    row_g = (lax.broadcasted_iota(jnp.int32, full.shape, 1) // r1) % SLAB_GROUPS
    col_h = (lax.broadcasted_iota(jnp.int32, full.shape, 2) // c1) % SLAB_GROUPS
    return jnp.where(row_g == col_h, full, 0.0).astype(BF16)


def _s5_rows_kernel(x_ref, g_ref, s0re_ref, s0im_ref, coef_ref, wb_ref, wy_ref, wc_ref, d_ref,
                    wglu_ref, bglu_ref, o_ref, sre_ref, sim_ref,
                    re_s, im_s, gel_s, car_re, car_im, *, row_tokens, n_seq):
    r, width = x_ref.shape
    d = width // row_tokens
    n_slabs = d // LANES
    slab_state = re_s.shape[1] // n_slabs

    xs = [x_ref[:, t * d:(t + 1) * d] for t in range(row_tokens)]
    hs = [_rms(x, g_ref[...]) for x in xs]
    hb = [h.astype(BF16) for h in hs]

    def slab_lhs(j):
        return jnp.concatenate([h[:, j * LANES:(j + 1) * LANES] for h in hb], axis=1)

    for j in range(n_slabs):
        w = jnp.dot(slab_lhs(j), wb_ref[j], preferred_element_type=F32)
        re_s[:, j * slab_state:(j + 1) * slab_state] = w[:, :slab_state]
        im_s[:, j * slab_state:(j + 1) * slab_state] = w[:, slab_state:]

    def row_tile(i, carry):
        cr, ci = carry
        rows = pl.ds(pl.multiple_of(i * SUBLANES, SUBLANES), SUBLANES)
        wr = re_s[rows, :]
        wi = im_s[rows, :]
        for k, shift in enumerate(SCAN_SHIFTS):
            ar = coef_ref[2 * k]
            ai = coef_ref[2 * k + 1]
            sr = pltpu.roll(wr, shift, 0)
            si = pltpu.roll(wi, shift, 0)
            wr, wi = wr + ar * sr - ai * si, wi + ar * si + ai * sr
        pr = coef_ref[2 * len(SCAN_SHIFTS)]
        pi = coef_ref[2 * len(SCAN_SHIFTS) + 1]
        wr, wi = wr + pr * cr - pi * ci, wi + pr * ci + pi * cr
        first = lax.broadcasted_iota(jnp.int32, wr.shape, 0) == 0
        re_s[rows, :] = jnp.where(first, cr, pltpu.roll(wr, 1, 0))
        im_s[rows, :] = jnp.where(first, ci, pltpu.roll(wi, 1, 0))
        return wr[SUBLANES - 1:SUBLANES, :], wi[SUBLANES - 1:SUBLANES, :]

    if n_seq == 1:
        @pl.when(pl.program_id(1) == 0)
        def _():
            car_re[...] = s0re_ref[...]
            car_im[...] = s0im_ref[...]

        cr, ci = lax.fori_loop(0, r // SUBLANES, row_tile, (car_re[...], car_im[...]))
        car_re[...] = cr
        car_im[...] = ci
    else:
        a_re = coef_ref[0][1:2, :]
        a_im = coef_ref[1][1:2, :]
        cr, ci = s0re_ref[...], s0im_ref[...]
        for n in range(r // n_seq):
            rows = slice(n * n_seq, (n + 1) * n_seq)
            wr, wi = re_s[rows, :], im_s[rows, :]
            re_s[rows, :] = cr
            im_s[rows, :] = ci
            cr, ci = a_re * cr - a_im * ci + wr, a_re * ci + a_im * cr + wi
    sre_ref[...] = cr
    sim_ref[...] = ci

    for j in range(n_slabs):
        cols = slice(j * LANES, (j + 1) * LANES)
        st = slice(j * slab_state, (j + 1) * slab_state)
        s_cat = jnp.concatenate([re_s[:, st], im_s[:, st]], axis=1).astype(BF16)
        y = (jnp.dot(slab_lhs(j), wy_ref[j], preferred_element_type=F32)
             + jnp.dot(s_cat, wc_ref[j], preferred_element_type=F32))
        for t in range(row_tokens):
            yt = y[:, t * LANES:(t + 1) * LANES] + d_ref[:, cols] * hs[t][:, cols]
            gel_s[t * r:(t + 1) * r, cols] = jax.nn.gelu(yt).astype(BF16)
    z = jnp.dot(gel_s[...], wglu_ref[...], preferred_element_type=F32) + bglu_ref[...]
    for t in range(row_tokens):
        zt = z[t * r:(t + 1) * r, :]
        o_ref[:, t * d:(t + 1) * d] = xs[t] + zt[:, :d] * jax.nn.sigmoid(zt[:, d:])


def _s5_rows_mix(x, g, s0re, s0im, coef, wb, wy, wc, dskip, wglu, bglu, *, batch, rows_per_seq,
                 r, row_tokens, step_major, name):
    rows, width = x.shape
    d = width // row_tokens
    n_state = s0re.shape[-1]
    if step_major:
        grid = (1, 1)
        r = rows
        state_spec = pl.BlockSpec((batch, n_state), lambda b, i: (0, 0))
        x_spec = pl.BlockSpec((rows, width), lambda b, i: (0, 0))
    else:
        nt = rows_per_seq // r
        grid = (batch, nt)
        state_spec = pl.BlockSpec((None, 1, n_state), lambda b, i: (b, 0, 0))
        x_spec = pl.BlockSpec((r, width), lambda b, i: (b * nt + i, 0))
    consts = (g, coef, wb, wy, wc, dskip, wglu, bglu)
    x1, sre, sim = pl.pallas_call(
        functools.partial(_s5_rows_kernel, row_tokens=row_tokens,
                          n_seq=batch if step_major else 1),
        grid=grid,
        in_specs=[x_spec, _const_spec(g.shape), state_spec, state_spec]
                 + [_const_spec(c.shape) for c in consts[1:]],
        out_specs=[x_spec, state_spec, state_spec],
        out_shape=[jax.ShapeDtypeStruct((rows, width), F32),
                   jax.ShapeDtypeStruct(s0re.shape, F32),
                   jax.ShapeDtypeStruct(s0im.shape, F32)],
        scratch_shapes=[pltpu.VMEM((r, n_state), F32), pltpu.VMEM((r, n_state), F32),
                        pltpu.VMEM((r * row_tokens, d), BF16),
                        pltpu.VMEM((1, n_state), F32), pltpu.VMEM((1, n_state), F32)],
        compiler_params=_params(("parallel", "arbitrary")),
        name=name,
    )(x, g, s0re, s0im, *consts[1:])
    return x1, sre, sim


def _s5_row_tables(a_re, a_im, log_dt, b_re, b_im, c_re, c_im, row_tokens):
    groups, n_p = a_re.shape
    p_tok = row_tokens
    n_slabs = groups // SLAB_GROUPS
    dt = jnp.exp(log_dt)[:, None]
    ks = jnp.arange(p_tok * SUBLANES + 1, dtype=F32)[:, None, None]
    mag = jnp.exp((a_re * dt)[None] * ks)
    ang = (a_im * dt)[None] * ks
    pw_re, pw_im = mag * jnp.cos(ang), mag * jnp.sin(ang)
    den = a_re * a_re + a_im * a_im
    q_re = ((pw_re[1] - 1.0) * a_re + pw_im[1] * a_im) / den
    q_im = (pw_im[1] * a_re - (pw_re[1] - 1.0) * a_im) / den
    bb_re = q_re[:, :, None] * b_re - q_im[:, :, None] * b_im
    bb_im = q_re[:, :, None] * b_im + q_im[:, :, None] * b_re

    rows = jnp.arange(SUBLANES)
    coef = []
    for shift in SCAN_SHIFTS:
        keep = (rows >= shift)[:, None].astype(F32)
        coef += [pw_re[p_tok * shift].reshape(1, -1) * keep,
                 pw_im[p_tok * shift].reshape(1, -1) * keep]
    pick = lambda pw, idx: jnp.stack([pw[k] for k in idx])
    carry = [p_tok * (r + 1) for r in range(SUBLANES)]
    coef += [pick(pw_re, carry).reshape(SUBLANES, -1), pick(pw_im, carry).reshape(SUBLANES, -1)]
    coef = jnp.stack(coef)

    back = [p_tok - 1 - s for s in range(p_tok)]
    er, ei = pick(pw_re, back)[:, :, :, None], pick(pw_im, back)[:, :, :, None]
    lay_b = lambda m: m.reshape(p_tok, n_slabs, SLAB_GROUPS, n_p, SSM_GROUP).transpose(
        1, 0, 2, 4, 3).reshape(n_slabs, -1, n_p)
    wb = _expand_groups(jnp.concatenate([lay_b(er * bb_re - ei * bb_im),
                                         lay_b(er * bb_im + ei * bb_re)], axis=2), SSM_GROUP, n_p)

    kr, ki = pw_re[:p_tok + 1][:, :, None, :], pw_im[:p_tok + 1][:, :, None, :]
    ck_re, ck_im = c_re * kr - c_im * ki, c_re * ki + c_im * kr

    lay_c = lambda m: m.reshape(p_tok, n_slabs, SLAB_GROUPS, SSM_GROUP, n_p).transpose(
        1, 2, 4, 0, 3).reshape(n_slabs, SLAB_GROUPS * n_p, -1)
    wc = _expand_groups(jnp.concatenate([lay_c(ck_re[1:]), -lay_c(ck_im[1:])], axis=1),
                        n_p, SSM_GROUP)

    bt_re, bt_im = bb_re.transpose(0, 2, 1)[None, :, :, None, :], bb_im.transpose(0, 2, 1)[
        None, :, :, None, :]
    lag = jnp.sum(ck_re[:p_tok, :, None] * bt_re - ck_im[:p_tok, :, None] * bt_im, axis=-1)
    lag = lag.reshape(p_tok, n_slabs, SLAB_GROUPS * SSM_GROUP, SSM_GROUP)
    zero = jnp.zeros_like(lag[0])
    toep = jnp.concatenate(
        [jnp.concatenate([lag[t - s] if t >= s else zero for t in range(p_tok)], axis=2)
         for s in range(p_tok)], axis=1)
    wy = _expand_groups(toep, SSM_GROUP, SSM_GROUP)
    return coef, wb, wy, wc


def kernel(x_prompt, x_sample, cache_k, cache_v, page_table, state_ssm_re, state_ssm_im,
           norm_mix, norm_ffn, norm_final, w_in_even, w_out_even, sgu_norm, sgu_w, sgu_b,
           lambda_q1, lambda_k1, lambda_q2, lambda_k2, attn_subln,
           ssm_a_re, ssm_a_im, ssm_log_dt, ssm_b_re, ssm_b_im, ssm_c_re, ssm_c_im, ssm_d,
           w_glu, b_glu, w_ffn_in, w_ffn_out):
    batch, seq, d = x_prompt.shape
    dec_batch, t_new, _ = x_sample.shape
    depth = norm_mix.shape[0]
    assert depth == 2 and seq % CHUNK == 0 and t_new == SUBLANES
    tp = batch * seq
    ts = dec_batch * t_new
    xp = x_prompt.reshape(tp, d)
    xs = x_sample.reshape(ts, d)
    tm_prompt = 512
    ff_chunk = 256
    row = lambda v: v.reshape(1, -1)

    lam_init = 0.8 - 0.6 * math.exp(-0.3 * 0)
    w_in0 = w_in_even[0].astype(BF16)
    w_out0 = w_out_even[0].astype(BF16)
    a_width = sgu_norm.shape[1]
    gw = a_width // A_GROUPS
    tril = jnp.tril(jnp.ones((CHUNK, CHUNK), bool))
    mix_p = jnp.where(tril, sgu_w[0], 0).astype(BF16)
    bias_p = jnp.repeat(sgu_b[0].T, gw, axis=1)
    small = jnp.where(tril[:t_new, :t_new], sgu_w[0][:, :t_new, :t_new], 0)
    row_i = lax.broadcasted_iota(jnp.int32, (ts, ts), 0)
    col_i = lax.broadcasted_iota(jnp.int32, (ts, ts), 1)
    rep = (row_i[:, :t_new] % t_new == col_i[:, :t_new]).astype(F32)
    tiled = jnp.einsum('at,gts,bs->gab', rep, small, rep, precision=lax.Precision.HIGHEST)
    mix_s = jnp.where(row_i // t_new == col_i // t_new, tiled, 0.0).astype(BF16)
    bias_s = jnp.tile(bias_p[:t_new], (dec_batch, 1))
    lam_vecs = [row(lambda_q1[0]), row(lambda_k1[0]), row(lambda_q2[0]), row(lambda_k2[0]),
                row(attn_subln[0])]

    a_p, q_p, k_p, vb_p, kb_p, vbb_p = _even_in(
        xp, row(norm_mix[0]), w_in0, row(sgu_norm[0]), mix_p, bias_p, tm=tm_prompt, prompt=True)
    a_s, q_s, k_s, vb_s, vn_s = _even_in(
        xs, row(norm_mix[0]), w_in0, row(sgu_norm[0]), mix_s, bias_s, tm=ts, prompt=False)
    n_layers, n_phys, page_size, heads, _ = cache_k.shape
    page_view = (n_layers, n_phys, page_size * heads, HEAD_DIM)
    b_p, b_s = _attention(page_table, q_p, kb_p, vbb_p, q_s, k_s, vb_s,
                          cache_k.reshape(page_view), cache_v.reshape(page_view), 0, *lam_vecs,
                          batch=batch, seq=seq, tq=512, t_new=t_new, lam_init=lam_init)
    w_ffn_in_b = w_ffn_in.astype(BF16)
    w_ffn_out_b = w_ffn_out.astype(BF16)
    xp = _mix_ffn(xp, a_p, b_p, w_out0, row(norm_ffn[0]), w_ffn_in_b, w_ffn_out_b, layer=0,
                  tm=tm_prompt, ff_chunk=ff_chunk, row_tokens=ROW_TOKENS, name="mix_ffn_prompt")
    xs = _mix_ffn(xs, a_s, b_s, w_out0, row(norm_ffn[0]), w_ffn_in_b, w_ffn_out_b, layer=0,
                  tm=ts, ff_chunk=ff_chunk, row_tokens=1, name="mix_ffn_sample")

    groups, n_p = ssm_a_re.shape[1:]
    n_state = groups * n_p
    ssm = (ssm_a_re[0], ssm_a_im[0], ssm_log_dt[0], ssm_b_re[0], ssm_b_im[0], ssm_c_re[0],
           ssm_c_im[0])
    w_glu0 = w_glu[0].astype(BF16)
    s5_args = (*_s5_row_tables(*ssm, ROW_TOKENS), row(ssm_d[0]), w_glu0, row(b_glu[0]))
    zeros = jnp.zeros((batch, 1, n_state), F32)
    xp, p_re, p_im = _s5_rows_mix(xp, row(norm_mix[1]), zeros, zeros, *s5_args, batch=batch,
                                  rows_per_seq=seq // ROW_TOKENS, r=S5_BLOCK_ROWS,
                                  row_tokens=ROW_TOKENS, step_major=False, name="s5_prompt")
    steps = t_new // ROW_TOKENS
    pack = lambda v: v.reshape(dec_batch, steps, -1).transpose(1, 0, 2).reshape(
        steps * dec_batch, -1)
    unpack = lambda v: v.reshape(steps, dec_batch, -1).transpose(1, 0, 2).reshape(ts, d)
    xs, s_re, s_im = _s5_rows_mix(pack(xs), row(norm_mix[1]),
                                  state_ssm_re[0].reshape(dec_batch, n_state),
                                  state_ssm_im[0].reshape(dec_batch, n_state), *s5_args,
                                  batch=dec_batch, rows_per_seq=steps, r=None,
                                  row_tokens=ROW_TOKENS, step_major=True, name="s5_sample")
    xs = unpack(xs)
    yp = _ffn_final(xp, row(norm_ffn[1]), w_ffn_in_b, w_ffn_out_b, row(norm_final), layer=1,
                    tm=tm_prompt, ff_chunk=ff_chunk, row_tokens=ROW_TOKENS,
                    name="ffn_final_prompt")
    ys = _ffn_final(xs, row(norm_ffn[1]), w_ffn_in_b, w_ffn_out_b, row(norm_final), layer=1,
                    tm=ts, ff_chunk=ff_chunk, row_tokens=1, name="ffn_final_sample")

    st = lambda s, n: s.reshape(1, n, groups, n_p)
    return (yp.reshape(batch, seq, d), ys.reshape(dec_batch, t_new, d),
            k_p.reshape(1, batch, seq, heads, HEAD_DIM), vb_p.reshape(1, batch, seq, heads, HEAD_DIM),
            k_s.reshape(1, dec_batch, t_new, heads, HEAD_DIM),
            vb_s.reshape(1, dec_batch, t_new, heads, HEAD_DIM),
            vn_s.reshape(1, dec_batch, t_new, a_width),
            st(p_re, batch), st(p_im, batch), st(s_re, dec_batch), st(s_im, dec_batch))
```

```python
import functools
import math

import jax
import jax.numpy as jnp
from jax import lax
from jax.experimental import pallas as pl
from jax.experimental.pallas import tpu as pltpu

F32 = jnp.float32
BF16 = jnp.bfloat16

EPS = 1e-6
LANES = 128
SUBLANES = 8
V7X_VMEM_BYTES = 64 * 1024 * 1024
VMEM_LIMIT = V7X_VMEM_BYTES * 7 // 8
MASK_VALUE = -0.7 * float(jnp.finfo(jnp.float32).max)

CHUNK = 128
A_GROUPS = 4
HEAD_DIM = 128
QK_HALF = 64
LOG2E = math.log2(math.e)
SSM_GROUP = 16
SLAB_GROUPS = LANES // SSM_GROUP
SCAN_SHIFTS = (1, 2, 4)
ROW_TOKENS = 4
S5_BLOCK_ROWS = 128

def _const_spec(shape):
    zeros = (0,) * len(shape)
    return pl.BlockSpec(shape, lambda *_: zeros, pipeline_mode=pl.Buffered(1))


def _layer_spec(shape, layer):
    index = (layer,) + (0,) * (len(shape) - 1)
    return pl.BlockSpec((None,) + tuple(shape[1:]), lambda *_: index, pipeline_mode=pl.Buffered(1))


def _rms(x, g):
    return x * lax.rsqrt(jnp.mean(x * x, axis=-1, keepdims=True) + EPS) * g


def _params(semantics):
    return pltpu.CompilerParams(dimension_semantics=semantics, vmem_limit_bytes=VMEM_LIMIT)


def _even_in_kernel(x_ref, g_ref, w_ref, sn_ref, mix_ref, bias_ref, *rest, prompt, a_width,
                    n_cast):
    cast_in, outs, cast_out = rest[:n_cast], rest[n_cast:len(rest) - n_cast], rest[len(rest) - n_cast:]
    for src, dst in zip(cast_in, cast_out):
        dst[...] = src[...].astype(BF16)
    if prompt:
        a_ref, q_ref, k_ref, vb_ref, kb_ref, vbb_ref = outs
    else:
        a_ref, q_ref, k_ref, vb_ref, vn_ref = outs
    tm = x_ref.shape[0]
    rows_per_mix = mix_ref.shape[1]
    gw = a_width // A_GROUPS
    h = _rms(x_ref[...], g_ref[...]).astype(BF16)

    def proj(c0, width):
        return jnp.dot(h, w_ref[:, c0:c0 + width], preferred_element_type=F32)

    u = jax.nn.gelu(proj(0, a_width))
    v = _rms(jax.nn.gelu(proj(a_width, a_width)), sn_ref[...])
    if not prompt:
        vn_ref[...] = v
    v16 = v.astype(BF16)
    o = 2 * a_width
    bw = q_ref.shape[1]
    q_ref[...] = (proj(o, bw) * (QK_HALF ** -0.5 * LOG2E)).astype(q_ref.dtype)
    k = proj(o + bw, bw)
    vb = proj(o + 2 * bw, bw)
    heads = bw // HEAD_DIM
    for hd in range(heads):
        k_ref[pl.ds(hd, tm, stride=heads), :] = k[:, hd * HEAD_DIM:(hd + 1) * HEAD_DIM]
        vb_ref[pl.ds(hd, tm, stride=heads), :] = vb[:, hd * HEAD_DIM:(hd + 1) * HEAD_DIM]
    if prompt:
        kb_ref[...] = k.astype(BF16)
        vbb_ref[...] = vb.astype(BF16)
    for c in range(tm // rows_per_mix):
        r0 = c * rows_per_mix
        for g in range(A_GROUPS):
            c0 = g * gw
            gate = jnp.dot(mix_ref[g], v16[r0:r0 + rows_per_mix, c0:c0 + gw],
                           preferred_element_type=F32) + bias_ref[:, c0:c0 + gw]
            a_ref[r0:r0 + rows_per_mix, c0:c0 + gw] = (
                u[r0:r0 + rows_per_mix, c0:c0 + gw] * gate).astype(a_ref.dtype)


def _even_in(x, g, w, sn, mix, bias, *, tm, prompt, cast_weights=()):
    t, d = x.shape
    steps = t // tm
    cast_specs = [pl.BlockSpec((c.shape[0] // steps, c.shape[1]), lambda i: (i, 0))
                  for c in cast_weights]
    a_width = sn.shape[1]
    bw = (w.shape[1] - 2 * a_width) // 3
    act = BF16 if prompt else F32
    row = lambda width: pl.BlockSpec((tm, width), lambda i: (i, 0))
    heads = bw // HEAD_DIM
    head_rows = pl.BlockSpec((tm * heads, HEAD_DIM), lambda i: (i, 0))
    out_shape = [jax.ShapeDtypeStruct((t, a_width), act), jax.ShapeDtypeStruct((t, bw), act),
                 jax.ShapeDtypeStruct((t * heads, HEAD_DIM), F32),
                 jax.ShapeDtypeStruct((t * heads, HEAD_DIM), F32)]
    out_specs = [row(a_width), row(bw), head_rows, head_rows]
    if prompt:
        out_shape += [jax.ShapeDtypeStruct((t, bw), BF16)] * 2
        out_specs += [row(bw), row(bw)]
    else:
        out_shape += [jax.ShapeDtypeStruct((t, a_width), F32)]
        out_specs += [row(a_width)]
    n_out = len(out_shape)
    outs = pl.pallas_call(
        functools.partial(_even_in_kernel, prompt=prompt, a_width=a_width,
                          n_cast=len(cast_weights)),
        grid=(steps,),
        in_specs=[row(d), _const_spec(g.shape), _const_spec(w.shape), _const_spec(sn.shape),
                  _const_spec(mix.shape), _const_spec(bias.shape)] + cast_specs,
        out_specs=out_specs + cast_specs,
        out_shape=out_shape + [jax.ShapeDtypeStruct(c.shape, BF16) for c in cast_weights],
        compiler_params=_params(("parallel",)),
        name="even_in_prompt" if prompt else "even_in_sample",
    )(x, g, w, sn, mix, bias, *cast_weights)
    return outs[:n_out], outs[n_out:]


def _split_halves(q):
    lane = lax.broadcasted_iota(jnp.int32, q.shape, 1)
    zero = jnp.zeros_like(q)
    return jnp.where(lane < QK_HALF, q, zero), jnp.where(lane >= QK_HALF, q, zero)


def _softmax_step(s, vblk, m_ref, l_ref, acc_ref, rows):
    tiles = [s[:, t * LANES:(t + 1) * LANES] for t in range(s.shape[1] // LANES)]
    m_prev = m_ref[rows, :]
    m_next = jnp.maximum(
        m_prev, jnp.max(functools.reduce(jnp.maximum, tiles), axis=1, keepdims=True))
    p_tiles = [jnp.exp2(t - m_next) for t in tiles]
    alpha = jnp.exp2(m_prev - m_next)
    l_ref[rows, :] = alpha * l_ref[rows, :] + functools.reduce(jnp.add, p_tiles)
    p = jnp.concatenate([t.astype(BF16) for t in p_tiles], axis=1)
    acc_ref[rows, :] = alpha * acc_ref[rows, :] + jnp.dot(p, vblk, preferred_element_type=F32)
    m_ref[rows, :] = m_next


def _row_sum(l):
    return jnp.sum(l, axis=1, keepdims=True)


def _diff_lambda(lq1, lk1, lq2, lk2, lam_init):
    e1 = jnp.exp(jnp.sum(lq1[...] * lk1[...], axis=1, keepdims=True))
    e2 = jnp.exp(jnp.sum(lq2[...] * lk2[...], axis=1, keepdims=True))
    return e1 - e2 + lam_init


def _head_out(o1, o2, lam, sub, lam_init):
    o = o1 - lam * o2
    return _rms(o, sub) * (1.0 - lam_init)


def _prompt_attn_step(i, q_ref, k_ref, v_ref, lam, sub_ref, o_ref, qs_ref, m_ref, l_ref, acc_ref,
                      lam_init):
    tq = q_ref.shape[0]
    q1, q2 = _split_halves(q_ref[...])
    qs_ref[0:tq, :] = q1
    qs_ref[tq:2 * tq, :] = q2
    m_ref[...] = jnp.full(m_ref.shape, MASK_VALUE, F32)
    l_ref[...] = jnp.zeros(l_ref.shape, F32)
    acc_ref[...] = jnp.zeros(acc_ref.shape, F32)
    all_rows = slice(0, 2 * tq)

    def scores(j):
        kv_rows = pl.ds(pl.multiple_of(j * tq, tq), tq)
        s = lax.dot_general(qs_ref[...], k_ref[kv_rows, :], (((1,), (1,)), ((), ())),
                            preferred_element_type=F32)
        return s, v_ref[kv_rows, :]

    def full_block(j):
        s, vblk = scores(j)
        _softmax_step(s, vblk, m_ref, l_ref, acc_ref, all_rows)

    def full_quad(jj, carry):
        for u in range(4):
            full_block(4 * jj + u)
        return carry

    lax.fori_loop(0, i // 4, full_quad, 0)

    @pl.when(i % 4 >= 2)
    def _():
        full_block(4 * (i // 4))
        full_block(4 * (i // 4) + 1)

    @pl.when(i % 2 == 1)
    def _():
        full_block(i - 1)
    s, vblk = scores(i)
    row = lax.broadcasted_iota(jnp.int32, s.shape, 0)
    col = lax.broadcasted_iota(jnp.int32, s.shape, 1)
    qpos = jnp.where(row >= tq, row - tq, row)
    s = jnp.where(qpos >= col, s, MASK_VALUE)
    _softmax_step(s, vblk, m_ref, l_ref, acc_ref, all_rows)

    o1 = acc_ref[0:tq, :] / _row_sum(l_ref[0:tq, :])
    o2 = acc_ref[tq:2 * tq, :] / _row_sum(l_ref[tq:2 * tq, :])
    o_ref[...] = _head_out(o1, o2, lam, sub_ref[...], lam_init).astype(o_ref.dtype)


def _sample_attn_step(j, is_last, q_ref, kn_ref, vn_ref, k_pages, v_pages, lam, sub_ref, o_ref,
                      qs_ref, bias_ref, m_ref, l_ref, acc_ref, heads, lam_init):
    pages_per_step = len(k_pages)
    t_new = q_ref.shape[0]
    rows_per_head = 2 * t_new
    all_rows = slice(0, heads * rows_per_head)

    def head_match(shape):
        row = lax.broadcasted_iota(jnp.int32, shape, 0)
        col = lax.broadcasted_iota(jnp.int32, shape, 1)
        return row, col, lax.rem(col, heads) == row // rows_per_head

    @pl.when(j == 0)
    def _():
        pieces = []
        for h in range(heads):
            pieces += _split_halves(q_ref[:, h * HEAD_DIM:(h + 1) * HEAD_DIM])
        qs_ref[...] = jnp.concatenate(pieces, axis=0).astype(BF16)
        _, _, ok = head_match(bias_ref.shape)
        bias_ref[...] = jnp.where(ok, 0.0, MASK_VALUE)
        m_ref[...] = jnp.full(m_ref.shape, MASK_VALUE, F32)
        l_ref[...] = jnp.zeros(l_ref.shape, F32)
        acc_ref[...] = jnp.zeros(acc_ref.shape, F32)

    def scores(kblk):
        return lax.dot_general(qs_ref[...], kblk.astype(BF16), (((1,), (1,)), ((), ())),
                               preferred_element_type=F32)

    kblk = jnp.concatenate([k_pages[p][...] for p in range(pages_per_step)], axis=0)
    vblk = jnp.concatenate([v_pages[p][...] for p in range(pages_per_step)], axis=0)
    _softmax_step(scores(kblk) + bias_ref[...], vblk.astype(BF16), m_ref, l_ref, acc_ref, all_rows)

    @pl.when(is_last)
    def _():
        pad = jnp.zeros((LANES - kn_ref.shape[0], HEAD_DIM), F32)
        s = scores(jnp.concatenate([kn_ref[...], pad], axis=0))
        row, col, ok = head_match(s.shape)
        s = jnp.where(ok, s, MASK_VALUE)
        s = jnp.where(col // heads <= lax.rem(row, t_new), s, MASK_VALUE)
        vblk = jnp.concatenate([vn_ref[...], pad], axis=0)
        _softmax_step(s, vblk.astype(BF16), m_ref, l_ref, acc_ref, all_rows)
        for h in range(heads):
            r0 = h * rows_per_head
            o1 = acc_ref[r0:r0 + t_new, :] / _row_sum(l_ref[r0:r0 + t_new, :])
            o2 = (acc_ref[r0 + t_new:r0 + rows_per_head, :]
                  / _row_sum(l_ref[r0 + t_new:r0 + rows_per_head, :]))
            o_ref[:, h * HEAD_DIM:(h + 1) * HEAD_DIM] = _head_out(
                o1, o2, lam, sub_ref[...], lam_init)


def _attention_kernel(pt_ref, qp_ref, kp_ref, vp_ref, qs_in_ref, kn_ref, vn_ref, *rest,
                      pages_per_step, heads, sample_steps, lam_init):
    del pt_ref
    k_pages = rest[:pages_per_step]
    v_pages = rest[pages_per_step:2 * pages_per_step]
    (lq1, lk1, lq2, lk2, sub_ref, op_ref, os_ref,
     pq_s, pm_s, pl_s, pacc_s, sq_s, sbias_s, sm_s, sl_s, sacc_s) = rest[2 * pages_per_step:]
    i = pl.program_id(2)
    step = (pl.program_id(0) * pl.num_programs(1) + pl.program_id(1)) * pl.num_programs(2) + i
    lam = _diff_lambda(lq1, lk1, lq2, lk2, lam_init)
    _prompt_attn_step(i, qp_ref, kp_ref, vp_ref, lam, sub_ref, op_ref, pq_s, pm_s, pl_s, pacc_s,
                      lam_init)
    j = lax.rem(step, sample_steps)
    _sample_attn_step(j, j == sample_steps - 1, qs_in_ref, kn_ref, vn_ref, k_pages, v_pages, lam,
                      sub_ref, os_ref, sq_s, sbias_s, sm_s, sl_s, sacc_s, heads, lam_init)


def _attention(page_table, q_p, kb_p, vb_p, q_s, k_new, v_new, cache_k, cache_v, layer,
               lq1, lk1, lq2, lk2, sub, *, batch, seq, tq, t_new, lam_init):
    tp, bw = q_p.shape
    ts = q_s.shape[0]
    heads = bw // HEAD_DIM
    nq = seq // tq
    dec_batch, n_pages = page_table.shape
    page_rows = cache_k.shape[2]
    total_steps = batch * heads * nq
    assert total_steps % dec_batch == 0 and n_pages % (total_steps // dec_batch) == 0
    sample_steps = total_steps // dec_batch
    pages_per_step = n_pages // sample_steps
    pt_flat = page_table.reshape(-1)

    def sample_pos(b, h, i):
        step = (b * heads + h) * nq + i
        return step // sample_steps, step % sample_steps

    def page_spec(p):
        def index(b, h, i, pt):
            s, j = sample_pos(b, h, i)
            return (layer, pt[s * n_pages + j * pages_per_step + p], 0, 0)
        return pl.BlockSpec((None, None, page_rows, HEAD_DIM), index)

    seq_block = lambda shape: pl.BlockSpec(shape, lambda b, h, i, pt: (sample_pos(b, h, i)[0], 0))
    vec = lambda a: pl.BlockSpec(a.shape, lambda b, h, i, pt: (0, 0))
    qp_spec = pl.BlockSpec((tq, HEAD_DIM), lambda b, h, i, pt: (b * nq + i, h))
    kv_spec = pl.BlockSpec((seq, HEAD_DIM), lambda b, h, i, pt: (b, h))
    rows_s = heads * 2 * t_new
    grid_spec = pltpu.PrefetchScalarGridSpec(
        num_scalar_prefetch=1,
        grid=(batch, heads, nq),
        in_specs=([qp_spec, kv_spec, kv_spec,
                   seq_block((t_new, bw)), seq_block((t_new * heads, HEAD_DIM)),
                   seq_block((t_new * heads, HEAD_DIM))]
                  + [page_spec(p) for p in range(pages_per_step)] * 2
                  + [vec(lq1), vec(lk1), vec(lq2), vec(lk2), vec(sub)]),
        out_specs=[qp_spec, seq_block((t_new, bw))],
        scratch_shapes=[pltpu.VMEM((2 * tq, HEAD_DIM), BF16),
                        pltpu.VMEM((2 * tq, LANES), F32),
                        pltpu.VMEM((2 * tq, LANES), F32),
                        pltpu.VMEM((2 * tq, HEAD_DIM), F32),
                        pltpu.VMEM((rows_s, HEAD_DIM), BF16),
                        pltpu.VMEM((rows_s, pages_per_step * page_rows), F32),
                        pltpu.VMEM((rows_s, LANES), F32),
                        pltpu.VMEM((rows_s, LANES), F32),
                        pltpu.VMEM((rows_s, HEAD_DIM), F32)])
    return pl.pallas_call(
        functools.partial(_attention_kernel, pages_per_step=pages_per_step, heads=heads,
                          sample_steps=sample_steps, lam_init=lam_init),
        grid_spec=grid_spec,
        out_shape=[jax.ShapeDtypeStruct((tp, bw), BF16), jax.ShapeDtypeStruct((ts, bw), F32)],
        compiler_params=_params(("arbitrary", "arbitrary", "arbitrary")),
        name="attention",
    )(pt_flat, q_p, kb_p, vb_p, q_s, k_new, v_new,
      *([cache_k] * pages_per_step), *([cache_v] * pages_per_step), lq1, lk1, lq2, lk2, sub)


def _swiglu_residual(x1, g_ref, w_in_ref, w_out_ref, ff_chunk):
    d_ff = w_out_ref.shape[0]
    h = _rms(x1, g_ref[...]).astype(BF16)
    acc = x1
    for c0 in range(0, d_ff, ff_chunk):
        gate = jnp.dot(h, w_in_ref[:, c0:c0 + ff_chunk], preferred_element_type=F32)
        up = jnp.dot(h, w_in_ref[:, d_ff + c0:d_ff + c0 + ff_chunk], preferred_element_type=F32)
        act = (gate * jax.nn.sigmoid(gate) * up).astype(BF16)
        acc = acc + jnp.dot(act, w_out_ref[c0:c0 + ff_chunk, :], preferred_element_type=F32)
    return acc


def _mix_ffn_kernel(x_ref, a_ref, b_ref, wo_ref, g_ref, w_in_ref, w_out_ref, o_ref, *scratch,
                    ff_chunk, row_tokens):
    tm, d = x_ref.shape
    ab = jnp.concatenate([a_ref[...].astype(BF16), b_ref[...].astype(BF16)], axis=1)
    x1 = x_ref[...] + jnp.dot(ab, wo_ref[...], preferred_element_type=F32)
    y = _swiglu_residual(x1, g_ref, w_in_ref, w_out_ref, ff_chunk)
    if row_tokens == 1:
        o_ref[...] = y
        return
    slabs = scratch[0]
    for j in range(d // LANES):
        slabs[j] = y[:, j * LANES:(j + 1) * LANES]
    for t in range(row_tokens):
        for j in range(d // LANES):
            o_ref[:, t * d + j * LANES:t * d + (j + 1) * LANES] = (
                slabs[j, pl.ds(t, tm // row_tokens, stride=row_tokens), :])


def _ffn_final_kernel(x_ref, g_ref, w_in_ref, w_out_ref, gf_ref, o_ref, *scratch,
                      ff_chunk, row_tokens):
    if row_tokens == 1:
        x = x_ref[...]
    else:
        r = x_ref.shape[0]
        d = x_ref.shape[1] // row_tokens
        slabs = scratch[0]
        for t in range(row_tokens):
            for j in range(d // LANES):
                slabs[j, pl.ds(t, r, stride=row_tokens), :] = (
                    x_ref[:, t * d + j * LANES:t * d + (j + 1) * LANES])
        x = jnp.concatenate([slabs[j] for j in range(d // LANES)], axis=1)
    o_ref[...] = _rms(_swiglu_residual(x, g_ref, w_in_ref, w_out_ref, ff_chunk), gf_ref[...])


def _mix_ffn(x, a, b, wo, g, w_in, w_out, *, layer, tm, ff_chunk, row_tokens, name):
    t, d = x.shape
    row = lambda arr: pl.BlockSpec((tm, arr.shape[1]), lambda i: (i, 0))
    packed = row_tokens > 1
    return pl.pallas_call(
        functools.partial(_mix_ffn_kernel, ff_chunk=ff_chunk, row_tokens=row_tokens),
        grid=(t // tm,),
        in_specs=[row(x), row(a), row(b), _const_spec(wo.shape), _const_spec(g.shape),
                  _layer_spec(w_in.shape, layer), _layer_spec(w_out.shape, layer)],
        out_specs=pl.BlockSpec((tm // row_tokens, row_tokens * d), lambda i: (i, 0)),
        out_shape=jax.ShapeDtypeStruct((t // row_tokens, row_tokens * d), F32),
        scratch_shapes=[pltpu.VMEM((d // LANES, tm, LANES), F32)] if packed else [],
        compiler_params=_params(("parallel",)),
        name=name,
    )(x, a, b, wo, g, w_in, w_out)


def _ffn_final(x, g, w_in, w_out, gf, *, layer, tm, ff_chunk, row_tokens, name):
    rows, width = x.shape
    d = width // row_tokens
    t = rows * row_tokens
    packed = row_tokens > 1
    return pl.pallas_call(
        functools.partial(_ffn_final_kernel, ff_chunk=ff_chunk, row_tokens=row_tokens),
        grid=(t // tm,),
        in_specs=[pl.BlockSpec((tm // row_tokens, width), lambda i: (i, 0)),
                  _const_spec(g.shape), _layer_spec(w_in.shape, layer),
                  _layer_spec(w_out.shape, layer), _const_spec(gf.shape)],
        out_specs=pl.BlockSpec((tm, d), lambda i: (i, 0)),
        out_shape=jax.ShapeDtypeStruct((t, d), F32),
        scratch_shapes=[pltpu.VMEM((d // LANES, tm, LANES), F32)] if packed else [],
        compiler_params=_params(("parallel",)),
        name=name,
    )(x, g, w_in, w_out, gf)


def _expand_groups(compact, r1, c1):
    n_slabs, rows, k = compact.shape
    width = (k // c1) * SLAB_GROUPS * c1
    src = lax.broadcasted_iota(jnp.int32, (k, width), 0)
    dst = lax.broadcasted_iota(jnp.int32, (k, width), 1)
    select = (src == (dst // (SLAB_GROUPS * c1)) * c1 + dst % c1).astype(F32)
    full = jnp.dot(compact.reshape(n_slabs * rows, k), select,
                   precision=lax.Precision.HIGHEST).reshape(n_slabs, rows, width)
    row_g = (lax.broadcasted_iota(jnp.int32, full.shape, 1) // r1) % SLAB_GROUPS
    col_h = (lax.broadcasted_iota(jnp.int32, full.shape, 2) // c1) % SLAB_GROUPS
    return jnp.where(row_g == col_h, full, 0.0).astype(BF16)


def _s5_rows_kernel(x_ref, g_ref, s0re_ref, s0im_ref, coef_ref, wb_ref, wy_ref, wc_ref, d_ref,
                    wglu_ref, bglu_ref, o_ref, sre_ref, sim_ref,
                    re_s, im_s, gel_s, car_re, car_im, *, row_tokens, n_seq):
    r, width = x_ref.shape
    d = width // row_tokens
    n_slabs = d // LANES
    slab_state = re_s.shape[1] // n_slabs

    xs = [x_ref[:, t * d:(t + 1) * d] for t in range(row_tokens)]
    hs = [_rms(x, g_ref[...]) for x in xs]
    hb = [h.astype(BF16) for h in hs]

    def slab_lhs(j):
        return jnp.concatenate([h[:, j * LANES:(j + 1) * LANES] for h in hb], axis=1)

    def state_cols(j):
        return slice(j * slab_state, (j + 1) * slab_state)

    def window(j):
        w = jnp.dot(slab_lhs(j), wb_ref[j], preferred_element_type=F32)
        re_s[:, state_cols(j)] = w[:, :slab_state]
        im_s[:, state_cols(j)] = w[:, slab_state:]

    def scan_rows(j):
        st = state_cols(j)
        cr, ci = car_re[:, st], car_im[:, st]
        first = lax.broadcasted_iota(jnp.int32, (SUBLANES, slab_state), 0) == 0
        for i in range(r // SUBLANES):
            rows = slice(i * SUBLANES, (i + 1) * SUBLANES)
            wr, wi = re_s[rows, st], im_s[rows, st]
            for k, shift in enumerate(SCAN_SHIFTS):
                ar, ai = coef_ref[2 * k, :, st], coef_ref[2 * k + 1, :, st]
                sr = pltpu.roll(wr, shift, 0)
                si = pltpu.roll(wi, shift, 0)
                wr, wi = wr + ar * sr - ai * si, wi + ar * si + ai * sr
            pr = coef_ref[2 * len(SCAN_SHIFTS), :, st]
            pi = coef_ref[2 * len(SCAN_SHIFTS) + 1, :, st]
            wr, wi = wr + pr * cr - pi * ci, wi + pr * ci + pi * cr
            re_s[rows, st] = jnp.where(first, cr, pltpu.roll(wr, 1, 0))
            im_s[rows, st] = jnp.where(first, ci, pltpu.roll(wi, 1, 0))
            cr, ci = wr[SUBLANES - 1:SUBLANES, :], wi[SUBLANES - 1:SUBLANES, :]
        car_re[:, st] = cr
        car_im[:, st] = ci
        sre_ref[:, st] = cr
        sim_ref[:, st] = ci

    def slab_outputs(j):
        cols = slice(j * LANES, (j + 1) * LANES)
        st = state_cols(j)
        s_cat = jnp.concatenate([re_s[:, st], im_s[:, st]], axis=1).astype(BF16)
        y = (jnp.dot(slab_lhs(j), wy_ref[j], preferred_element_type=F32)
             + jnp.dot(s_cat, wc_ref[j], preferred_element_type=F32))
        for t in range(row_tokens):
            yt = y[:, t * LANES:(t + 1) * LANES] + d_ref[:, cols] * hs[t][:, cols]
            gel_s[t * r:(t + 1) * r, cols] = jax.nn.gelu(yt).astype(BF16)

    if n_seq == 1:
        @pl.when(pl.program_id(1) == 0)
        def _():
            car_re[...] = s0re_ref[...]
            car_im[...] = s0im_ref[...]

        for j in range(n_slabs):
            window(j)
            scan_rows(j)
            slab_outputs(j)
    else:
        for j in range(n_slabs):
            window(j)
        a_re = coef_ref[0][1:2, :]
        a_im = coef_ref[1][1:2, :]
        cr, ci = s0re_ref[...], s0im_ref[...]
        for n in range(r // n_seq):
            rows = slice(n * n_seq, (n + 1) * n_seq)
            wr, wi = re_s[rows, :], im_s[rows, :]
            re_s[rows, :] = cr
            im_s[rows, :] = ci
            cr, ci = a_re * cr - a_im * ci + wr, a_re * ci + a_im * cr + wi
        sre_ref[...] = cr
        sim_ref[...] = ci
        for j in range(n_slabs):
            slab_outputs(j)
    z = jnp.dot(gel_s[...], wglu_ref[...], preferred_element_type=F32) + bglu_ref[...]
    for t in range(row_tokens):
        zt = z[t * r:(t + 1) * r, :]
        o_ref[:, t * d:(t + 1) * d] = xs[t] + zt[:, :d] * jax.nn.sigmoid(zt[:, d:])


def _s5_rows_mix(x, g, s0re, s0im, coef, wb, wy, wc, dskip, wglu, bglu, *, batch, rows_per_seq,
                 r, row_tokens, step_major, name):
    rows, width = x.shape
    d = width // row_tokens
    n_state = s0re.shape[-1]
    if step_major:
        grid = (1, 1)
        r = rows
        state_spec = pl.BlockSpec((batch, n_state), lambda b, i: (0, 0))
        x_spec = pl.BlockSpec((rows, width), lambda b, i: (0, 0))
    else:
        nt = rows_per_seq // r
        grid = (batch, nt)
        state_spec = pl.BlockSpec((None, 1, n_state), lambda b, i: (b, 0, 0))
        x_spec = pl.BlockSpec((r, width), lambda b, i: (b * nt + i, 0))
    consts = (g, coef, wb, wy, wc, dskip, wglu, bglu)
    x1, sre, sim = pl.pallas_call(
        functools.partial(_s5_rows_kernel, row_tokens=row_tokens,
                          n_seq=batch if step_major else 1),
        grid=grid,
        in_specs=[x_spec, _const_spec(g.shape), state_spec, state_spec]
                 + [_const_spec(c.shape) for c in consts[1:]],
        out_specs=[x_spec, state_spec, state_spec],
        out_shape=[jax.ShapeDtypeStruct((rows, width), F32),
                   jax.ShapeDtypeStruct(s0re.shape, F32),
                   jax.ShapeDtypeStruct(s0im.shape, F32)],
        scratch_shapes=[pltpu.VMEM((r, n_state), F32), pltpu.VMEM((r, n_state), F32),
                        pltpu.VMEM((r * row_tokens, d), BF16),
                        pltpu.VMEM((1, n_state), F32), pltpu.VMEM((1, n_state), F32)],
        compiler_params=_params(("parallel", "arbitrary")),
        name=name,
    )(x, g, s0re, s0im, *consts[1:])
    return x1, sre, sim


def _s5_row_tables(a_re, a_im, log_dt, b_re, b_im, c_re, c_im, row_tokens):
    groups, n_p = a_re.shape
    p_tok = row_tokens
    n_slabs = groups // SLAB_GROUPS
    dt = jnp.exp(log_dt)[:, None]
    ks = jnp.arange(p_tok * SUBLANES + 1, dtype=F32)[:, None, None]
    mag = jnp.exp((a_re * dt)[None] * ks)
    ang = (a_im * dt)[None] * ks
    pw_re, pw_im = mag * jnp.cos(ang), mag * jnp.sin(ang)
    den = a_re * a_re + a_im * a_im
    q_re = ((pw_re[1] - 1.0) * a_re + pw_im[1] * a_im) / den
    q_im = (pw_im[1] * a_re - (pw_re[1] - 1.0) * a_im) / den
    bb_re = q_re[:, :, None] * b_re - q_im[:, :, None] * b_im
    bb_im = q_re[:, :, None] * b_im + q_im[:, :, None] * b_re

    rows = jnp.arange(SUBLANES)
    coef = []
    for shift in SCAN_SHIFTS:
        keep = (rows >= shift)[:, None].astype(F32)
        coef += [pw_re[p_tok * shift].reshape(1, -1) * keep,
                 pw_im[p_tok * shift].reshape(1, -1) * keep]
    pick = lambda pw, idx: jnp.stack([pw[k] for k in idx])
    carry = [p_tok * (r + 1) for r in range(SUBLANES)]
    coef += [pick(pw_re, carry).reshape(SUBLANES, -1), pick(pw_im, carry).reshape(SUBLANES, -1)]
    coef = jnp.stack(coef)

    back = [p_tok - 1 - s for s in range(p_tok)]
    er, ei = pick(pw_re, back)[:, :, :, None], pick(pw_im, back)[:, :, :, None]
    lay_b = lambda m: m.reshape(p_tok, n_slabs, SLAB_GROUPS, n_p, SSM_GROUP).transpose(
        1, 0, 2, 4, 3).reshape(n_slabs, -1, n_p)
    wb = _expand_groups(jnp.concatenate([lay_b(er * bb_re - ei * bb_im),
                                         lay_b(er * bb_im + ei * bb_re)], axis=2), SSM_GROUP, n_p)

    kr, ki = pw_re[:p_tok + 1][:, :, None, :], pw_im[:p_tok + 1][:, :, None, :]
    ck_re, ck_im = c_re * kr - c_im * ki, c_re * ki + c_im * kr

    lay_c = lambda m: m.reshape(p_tok, n_slabs, SLAB_GROUPS, SSM_GROUP, n_p).transpose(
        1, 2, 4, 0, 3).reshape(n_slabs, SLAB_GROUPS * n_p, -1)
    wc = _expand_groups(jnp.concatenate([lay_c(ck_re[1:]), -lay_c(ck_im[1:])], axis=1),
                        n_p, SSM_GROUP)

    bt_re, bt_im = bb_re.transpose(0, 2, 1)[None, :, :, None, :], bb_im.transpose(0, 2, 1)[
        None, :, :, None, :]
    lag = jnp.sum(ck_re[:p_tok, :, None] * bt_re - ck_im[:p_tok, :, None] * bt_im, axis=-1)
    lag = lag.reshape(p_tok, n_slabs, SLAB_GROUPS * SSM_GROUP, SSM_GROUP)
    zero = jnp.zeros_like(lag[0])
    toep = jnp.concatenate(
        [jnp.concatenate([lag[t - s] if t >= s else zero for t in range(p_tok)], axis=2)
         for s in range(p_tok)], axis=1)
    wy = _expand_groups(toep, SSM_GROUP, SSM_GROUP)
    return coef, wb, wy, wc


def kernel(x_prompt, x_sample, cache_k, cache_v, page_table, state_ssm_re, state_ssm_im,
           norm_mix, norm_ffn, norm_final, w_in_even, w_out_even, sgu_norm, sgu_w, sgu_b,
           lambda_q1, lambda_k1, lambda_q2, lambda_k2, attn_subln,
           ssm_a_re, ssm_a_im, ssm_log_dt, ssm_b_re, ssm_b_im, ssm_c_re, ssm_c_im, ssm_d,
           w_glu, b_glu, w_ffn_in, w_ffn_out):
    batch, seq, d = x_prompt.shape
    dec_batch, t_new, _ = x_sample.shape
    depth = norm_mix.shape[0]
    assert depth == 2 and seq % CHUNK == 0 and t_new == SUBLANES
    tp = batch * seq
    ts = dec_batch * t_new
    xp = x_prompt.reshape(tp, d)
    xs = x_sample.reshape(ts, d)
    tm_prompt = 512
    ff_chunk = 256
    row = lambda v: v.reshape(1, -1)

    lam_init = 0.8 - 0.6 * math.exp(-0.3 * 0)
    w_in0 = w_in_even[0].astype(BF16)
    a_width = sgu_norm.shape[1]
    gw = a_width // A_GROUPS
    tril = jnp.tril(jnp.ones((CHUNK, CHUNK), bool))
    mix_p = jnp.where(tril, sgu_w[0], 0).astype(BF16)
    bias_p = jnp.repeat(sgu_b[0].T, gw, axis=1)
    small = jnp.where(tril[:t_new, :t_new], sgu_w[0][:, :t_new, :t_new], 0)
    row_i = lax.broadcasted_iota(jnp.int32, (ts, ts), 0)
    col_i = lax.broadcasted_iota(jnp.int32, (ts, ts), 1)
    rep = (row_i[:, :t_new] % t_new == col_i[:, :t_new]).astype(F32)
    tiled = jnp.einsum('at,gts,bs->gab', rep, small, rep, precision=lax.Precision.HIGHEST)
    mix_s = jnp.where(row_i // t_new == col_i // t_new, tiled, 0.0).astype(BF16)
    bias_s = jnp.tile(bias_p[:t_new], (dec_batch, 1))
    lam_vecs = [row(lambda_q1[0]), row(lambda_k1[0]), row(lambda_q2[0]), row(lambda_k2[0]),
                row(attn_subln[0])]

    later_weights = (w_ffn_in.reshape(-1, w_ffn_in.shape[-1]),
                     w_ffn_out.reshape(-1, w_ffn_out.shape[-1]), w_glu[0], w_out_even[0])
    (a_p, q_p, k_p, vb_p, kb_p, vbb_p), (w_ffn_in_b, w_ffn_out_b, w_glu0, w_out0) = _even_in(
        xp, row(norm_mix[0]), w_in0, row(sgu_norm[0]), mix_p, bias_p, tm=tm_prompt, prompt=True,
        cast_weights=later_weights)
    w_ffn_in_b = w_ffn_in_b.reshape(w_ffn_in.shape)
    w_ffn_out_b = w_ffn_out_b.reshape(w_ffn_out.shape)
    (a_s, q_s, k_s, vb_s, vn_s), _ = _even_in(
        xs, row(norm_mix[0]), w_in0, row(sgu_norm[0]), mix_s, bias_s, tm=ts, prompt=False)
    n_layers, n_phys, page_size, heads, _ = cache_k.shape
    page_view = (n_layers, n_phys, page_size * heads, HEAD_DIM)
    b_p, b_s = _attention(page_table, q_p, kb_p, vbb_p, q_s, k_s, vb_s,
                          cache_k.reshape(page_view), cache_v.reshape(page_view), 0, *lam_vecs,
                          batch=batch, seq=seq, tq=512, t_new=t_new, lam_init=lam_init)
    xp = _mix_ffn(xp, a_p, b_p, w_out0, row(norm_ffn[0]), w_ffn_in_b, w_ffn_out_b, layer=0,
                  tm=tm_prompt, ff_chunk=ff_chunk, row_tokens=ROW_TOKENS, name="mix_ffn_prompt")
    xs = _mix_ffn(xs, a_s, b_s, w_out0, row(norm_ffn[0]), w_ffn_in_b, w_ffn_out_b, layer=0,
                  tm=ts, ff_chunk=ff_chunk, row_tokens=1, name="mix_ffn_sample")

    groups, n_p = ssm_a_re.shape[1:]
    n_state = groups * n_p
    ssm = (ssm_a_re[0], ssm_a_im[0], ssm_log_dt[0], ssm_b_re[0], ssm_b_im[0], ssm_c_re[0],
           ssm_c_im[0])
    s5_args = (*_s5_row_tables(*ssm, ROW_TOKENS), row(ssm_d[0]), w_glu0, row(b_glu[0]))
    zeros = jnp.zeros((batch, 1, n_state), F32)
    xp, p_re, p_im = _s5_rows_mix(xp, row(norm_mix[1]), zeros, zeros, *s5_args, batch=batch,
                                  rows_per_seq=seq // ROW_TOKENS, r=S5_BLOCK_ROWS,
                                  row_tokens=ROW_TOKENS, step_major=False, name="s5_prompt")
    steps = t_new // ROW_TOKENS
    pack = lambda v: v.reshape(dec_batch, steps, -1).transpose(1, 0, 2).reshape(
        steps * dec_batch, -1)
    unpack = lambda v: v.reshape(steps, dec_batch, -1).transpose(1, 0, 2).reshape(ts, d)
    xs, s_re, s_im = _s5_rows_mix(pack(xs), row(norm_mix[1]),
                                  state_ssm_re[0].reshape(dec_batch, n_state),
                                  state_ssm_im[0].reshape(dec_batch, n_state), *s5_args,
                                  batch=dec_batch, rows_per_seq=steps, r=None,
                                  row_tokens=ROW_TOKENS, step_major=True, name="s5_sample")
    xs = unpack(xs)
    yp = _ffn_final(xp, row(norm_ffn[1]), w_ffn_in_b, w_ffn_out_b, row(norm_final), layer=1,
                    tm=tm_prompt, ff_chunk=ff_chunk, row_tokens=ROW_TOKENS,
                    name="ffn_final_prompt")
    ys = _ffn_final(xs, row(norm_ffn[1]), w_ffn_in_b, w_ffn_out_b, row(norm_final), layer=1,
                    tm=ts, ff_chunk=ff_chunk, row_tokens=1, name="ffn_final_sample")

    st = lambda s, n: s.reshape(1, n, groups, n_p)
    return (yp.reshape(batch, seq, d), ys.reshape(dec_batch, t_new, d),
            k_p.reshape(1, batch, seq, heads, HEAD_DIM), vb_p.reshape(1, batch, seq, heads, HEAD_DIM),
            k_s.reshape(1, dec_batch, t_new, heads, HEAD_DIM),
            vb_s.reshape(1, dec_batch, t_new, heads, HEAD_DIM),
            vn_s.reshape(1, dec_batch, t_new, a_width),
            st(p_re, batch), st(p_im, batch), st(s_re, dec_batch), st(s_im, dec_batch))
```

```python
import functools
import math

import jax
import jax.numpy as jnp
from jax import lax
from jax.experimental import pallas as pl
from jax.experimental.pallas import tpu as pltpu

F32 = jnp.float32
BF16 = jnp.bfloat16

EPS = 1e-6
LANES = 128
SUBLANES = 8
V7X_VMEM_BYTES = 64 * 1024 * 1024
VMEM_LIMIT = V7X_VMEM_BYTES * 7 // 8
MASK_VALUE = -0.7 * float(jnp.finfo(jnp.float32).max)

CHUNK = 128
A_GROUPS = 4
HEAD_DIM = 128
QK_HALF = 64
LOG2E = math.log2(math.e)
SSM_GROUP = 16
SLAB_GROUPS = LANES // SSM_GROUP
SCAN_SHIFTS = (1, 2, 4)
ROW_TOKENS = 4
S5_BLOCK_ROWS = 128

def _const_spec(shape):
    zeros = (0,) * len(shape)
    return pl.BlockSpec(shape, lambda *_: zeros, pipeline_mode=pl.Buffered(1))


def _layer_spec(shape, layer):
    index = (layer,) + (0,) * (len(shape) - 1)
    return pl.BlockSpec((None,) + tuple(shape[1:]), lambda *_: index, pipeline_mode=pl.Buffered(1))


def _rms(x, g):
    return x * lax.rsqrt(jnp.mean(x * x, axis=-1, keepdims=True) + EPS) * g


def _params(semantics):
    return pltpu.CompilerParams(dimension_semantics=semantics, vmem_limit_bytes=VMEM_LIMIT)


def _even_in_kernel(x_ref, g_ref, w_ref, sn_ref, mix_ref, bias_ref, *rest, prompt, a_width,
                    n_cast):
    cast_in, outs, cast_out = rest[:n_cast], rest[n_cast:len(rest) - n_cast], rest[len(rest) - n_cast:]
    for src, dst in zip(cast_in, cast_out):
        dst[...] = src[...].astype(BF16)
    if prompt:
        a_ref, q_ref, k_ref, vb_ref, kb_ref, vbb_ref = outs
    else:
        a_ref, q_ref, k_ref, vb_ref, vn_ref = outs
    tm = x_ref.shape[0]
    rows_per_mix = mix_ref.shape[1]
    gw = a_width // A_GROUPS
    h = _rms(x_ref[...], g_ref[...]).astype(BF16)

    def proj(c0, width):
        return jnp.dot(h, w_ref[:, c0:c0 + width], preferred_element_type=F32)

    u = jax.nn.gelu(proj(0, a_width))
    v = _rms(jax.nn.gelu(proj(a_width, a_width)), sn_ref[...])
    if not prompt:
        vn_ref[...] = v
    v16 = v.astype(BF16)
    o = 2 * a_width
    bw = q_ref.shape[1]
    q_ref[...] = (proj(o, bw) * (QK_HALF ** -0.5 * LOG2E)).astype(q_ref.dtype)
    k = proj(o + bw, bw)
    vb = proj(o + 2 * bw, bw)
    heads = bw // HEAD_DIM
    for hd in range(heads):
        k_ref[pl.ds(hd, tm, stride=heads), :] = k[:, hd * HEAD_DIM:(hd + 1) * HEAD_DIM]
        vb_ref[pl.ds(hd, tm, stride=heads), :] = vb[:, hd * HEAD_DIM:(hd + 1) * HEAD_DIM]
    if prompt:
        kb_ref[...] = k.astype(BF16)
        vbb_ref[...] = vb.astype(BF16)
    for c in range(tm // rows_per_mix):
        r0 = c * rows_per_mix
        for g in range(A_GROUPS):
            c0 = g * gw
            gate = jnp.dot(mix_ref[g], v16[r0:r0 + rows_per_mix, c0:c0 + gw],
                           preferred_element_type=F32) + bias_ref[:, c0:c0 + gw]
            a_ref[r0:r0 + rows_per_mix, c0:c0 + gw] = (
                u[r0:r0 + rows_per_mix, c0:c0 + gw] * gate).astype(a_ref.dtype)


def _even_in(x, g, w, sn, mix, bias, *, tm, prompt, cast_weights=()):
    t, d = x.shape
    steps = t // tm
    cast_specs = [pl.BlockSpec((c.shape[0] // steps, c.shape[1]), lambda i: (i, 0))
                  for c in cast_weights]
    a_width = sn.shape[1]
    bw = (w.shape[1] - 2 * a_width) // 3
    act = BF16 if prompt else F32
    row = lambda width: pl.BlockSpec((tm, width), lambda i: (i, 0))
    heads = bw // HEAD_DIM
    head_rows = pl.BlockSpec((tm * heads, HEAD_DIM), lambda i: (i, 0))
    out_shape = [jax.ShapeDtypeStruct((t, a_width), act), jax.ShapeDtypeStruct((t, bw), act),
                 jax.ShapeDtypeStruct((t * heads, HEAD_DIM), F32),
                 jax.ShapeDtypeStruct((t * heads, HEAD_DIM), F32)]
    out_specs = [row(a_width), row(bw), head_rows, head_rows]
    if prompt:
        out_shape += [jax.ShapeDtypeStruct((t, bw), BF16)] * 2
        out_specs += [row(bw), row(bw)]
    else:
        out_shape += [jax.ShapeDtypeStruct((t, a_width), F32)]
        out_specs += [row(a_width)]
    n_out = len(out_shape)
    outs = pl.pallas_call(
        functools.partial(_even_in_kernel, prompt=prompt, a_width=a_width,
                          n_cast=len(cast_weights)),
        grid=(steps,),
        in_specs=[row(d), _const_spec(g.shape), _const_spec(w.shape), _const_spec(sn.shape),
                  _const_spec(mix.shape), _const_spec(bias.shape)] + cast_specs,
        out_specs=out_specs + cast_specs,
        out_shape=out_shape + [jax.ShapeDtypeStruct(c.shape, BF16) for c in cast_weights],
        compiler_params=_params(("parallel",)),
        name="even_in_prompt" if prompt else "even_in_sample",
    )(x, g, w, sn, mix, bias, *cast_weights)
    return outs[:n_out], outs[n_out:]


def _split_halves(q):
    lane = lax.broadcasted_iota(jnp.int32, q.shape, 1)
    zero = jnp.zeros_like(q)
    return jnp.where(lane < QK_HALF, q, zero), jnp.where(lane >= QK_HALF, q, zero)


def _softmax_step(s, vblk, m_ref, l_ref, acc_ref, rows):
    tiles = [s[:, t * LANES:(t + 1) * LANES] for t in range(s.shape[1] // LANES)]
    m_prev = m_ref[rows, :]
    m_next = jnp.maximum(
        m_prev, jnp.max(functools.reduce(jnp.maximum, tiles), axis=1, keepdims=True))
    p_tiles = [jnp.exp2(t - m_next) for t in tiles]
    alpha = jnp.exp2(m_prev - m_next)
    l_ref[rows, :] = alpha * l_ref[rows, :] + functools.reduce(jnp.add, p_tiles)
    p = jnp.concatenate([t.astype(BF16) for t in p_tiles], axis=1)
    acc_ref[rows, :] = alpha * acc_ref[rows, :] + jnp.dot(p, vblk, preferred_element_type=F32)
    m_ref[rows, :] = m_next


def _row_sum(l):
    return jnp.sum(l, axis=1, keepdims=True)


def _diff_lambda(lq1, lk1, lq2, lk2, lam_init):
    e1 = jnp.exp(jnp.sum(lq1[...] * lk1[...], axis=1, keepdims=True))
    e2 = jnp.exp(jnp.sum(lq2[...] * lk2[...], axis=1, keepdims=True))
    return e1 - e2 + lam_init


def _head_out(o1, o2, lam, sub, lam_init):
    o = o1 - lam * o2
    return _rms(o, sub) * (1.0 - lam_init)


def _prompt_attn_step(i, q_ref, k_ref, v_ref, lam, sub_ref, o_ref, qs_ref, m_ref, l_ref, acc_ref,
                      lam_init):
    tq = q_ref.shape[0]
    q1, q2 = _split_halves(q_ref[...])
    qs_ref[0:tq, :] = q1
    qs_ref[tq:2 * tq, :] = q2
    m_ref[...] = jnp.full(m_ref.shape, MASK_VALUE, F32)
    l_ref[...] = jnp.zeros(l_ref.shape, F32)
    acc_ref[...] = jnp.zeros(acc_ref.shape, F32)
    all_rows = slice(0, 2 * tq)

    def scores(j):
        kv_rows = pl.ds(pl.multiple_of(j * tq, tq), tq)
        s = lax.dot_general(qs_ref[...], k_ref[kv_rows, :], (((1,), (1,)), ((), ())),
                            preferred_element_type=F32)
        return s, v_ref[kv_rows, :]

    def full_block(j):
        s, vblk = scores(j)
        _softmax_step(s, vblk, m_ref, l_ref, acc_ref, all_rows)

    def full_quad(jj, carry):
        for u in range(4):
            full_block(4 * jj + u)
        return carry

    lax.fori_loop(0, i // 4, full_quad, 0)

    @pl.when(i % 4 >= 2)
    def _():
        full_block(4 * (i // 4))
        full_block(4 * (i // 4) + 1)

    @pl.when(i % 2 == 1)
    def _():
        full_block(i - 1)
    s, vblk = scores(i)
    row = lax.broadcasted_iota(jnp.int32, s.shape, 0)
    col = lax.broadcasted_iota(jnp.int32, s.shape, 1)
    qpos = jnp.where(row >= tq, row - tq, row)
    s = jnp.where(qpos >= col, s, MASK_VALUE)
    _softmax_step(s, vblk, m_ref, l_ref, acc_ref, all_rows)

    o1 = acc_ref[0:tq, :] / _row_sum(l_ref[0:tq, :])
    o2 = acc_ref[tq:2 * tq, :] / _row_sum(l_ref[tq:2 * tq, :])
    o_ref[...] = _head_out(o1, o2, lam, sub_ref[...], lam_init).astype(o_ref.dtype)


def _head_match(shape, heads, rows_per_head):
    row = lax.broadcasted_iota(jnp.int32, shape, 0)
    col = lax.broadcasted_iota(jnp.int32, shape, 1)
    return row, col, lax.rem(col, heads) == row // rows_per_head


def _sample_attn_init(q_ref, qs_ref, bias_ref, m_ref, l_ref, acc_ref, heads):
    pieces = []
    for h in range(heads):
        pieces += _split_halves(q_ref[:, h * HEAD_DIM:(h + 1) * HEAD_DIM])
    qs_ref[...] = jnp.concatenate(pieces, axis=0).astype(BF16)
    _, _, ok = _head_match(bias_ref.shape, heads, 2 * q_ref.shape[0])
    bias_ref[...] = jnp.where(ok, 0.0, MASK_VALUE)
    m_ref[...] = jnp.full(m_ref.shape, MASK_VALUE, F32)
    l_ref[...] = jnp.zeros(l_ref.shape, F32)
    acc_ref[...] = jnp.zeros(acc_ref.shape, F32)


def _sample_attn_step(is_last, q_ref, kn_ref, vn_ref, k_pages, v_pages, lam, sub_ref, o_ref,
                      qs_ref, bias_ref, m_ref, l_ref, acc_ref, heads, lam_init):
    pages_per_step = len(k_pages)
    t_new = q_ref.shape[0]
    rows_per_head = 2 * t_new
    all_rows = slice(0, heads * rows_per_head)

    def scores(kblk):
        return lax.dot_general(qs_ref[...], kblk.astype(BF16), (((1,), (1,)), ((), ())),
                               preferred_element_type=F32)

    kblk = jnp.concatenate([k_pages[p][...] for p in range(pages_per_step)], axis=0)
    vblk = jnp.concatenate([v_pages[p][...] for p in range(pages_per_step)], axis=0)
    _softmax_step(scores(kblk) + bias_ref[...], vblk.astype(BF16), m_ref, l_ref, acc_ref, all_rows)

    @pl.when(is_last)
    def _():
        pad = jnp.zeros((LANES - kn_ref.shape[0], HEAD_DIM), F32)
        s = scores(jnp.concatenate([kn_ref[...], pad], axis=0))
        row, col, ok = _head_match(s.shape, heads, rows_per_head)
        s = jnp.where(ok, s, MASK_VALUE)
        s = jnp.where(col // heads <= lax.rem(row, t_new), s, MASK_VALUE)
        vblk = jnp.concatenate([vn_ref[...], pad], axis=0)
        _softmax_step(s, vblk.astype(BF16), m_ref, l_ref, acc_ref, all_rows)
        for h in range(heads):
            r0 = h * rows_per_head
            o1 = acc_ref[r0:r0 + t_new, :] / _row_sum(l_ref[r0:r0 + t_new, :])
            o2 = (acc_ref[r0 + t_new:r0 + rows_per_head, :]
                  / _row_sum(l_ref[r0 + t_new:r0 + rows_per_head, :]))
            o_ref[:, h * HEAD_DIM:(h + 1) * HEAD_DIM] = _head_out(
                o1, o2, lam, sub_ref[...], lam_init)


def _attention_kernel(pt_ref, qp_ref, kp_ref, vp_ref, qs_in_ref, kn_ref, vn_ref, *rest,
                      pages_per_step, heads, sample_steps, lam_init):
    del pt_ref
    k_pages = rest[:pages_per_step]
    v_pages = rest[pages_per_step:2 * pages_per_step]
    (lq1, lk1, lq2, lk2, sub_ref, op_ref, os_ref,
     pq_s, pm_s, pl_s, pacc_s, sq_s, sbias_s, sm_s, sl_s, sacc_s) = rest[2 * pages_per_step:]
    i = pl.program_id(2)
    step = (pl.program_id(0) * pl.num_programs(1) + pl.program_id(1)) * pl.num_programs(2) + i
    j = lax.rem(step, sample_steps)

    @pl.when(j == 0)
    def _():
        _sample_attn_init(qs_in_ref, sq_s, sbias_s, sm_s, sl_s, sacc_s, heads)

    lam = _diff_lambda(lq1, lk1, lq2, lk2, lam_init)
    _prompt_attn_step(i, qp_ref, kp_ref, vp_ref, lam, sub_ref, op_ref, pq_s, pm_s, pl_s, pacc_s,
                      lam_init)
    _sample_attn_step(j == sample_steps - 1, qs_in_ref, kn_ref, vn_ref, k_pages, v_pages, lam,
                      sub_ref, os_ref, sq_s, sbias_s, sm_s, sl_s, sacc_s, heads, lam_init)


def _attention(page_table, q_p, kb_p, vb_p, q_s, k_new, v_new, cache_k, cache_v, layer,
               lq1, lk1, lq2, lk2, sub, *, batch, seq, tq, t_new, lam_init):
    tp, bw = q_p.shape
    ts = q_s.shape[0]
    heads = bw // HEAD_DIM
    nq = seq // tq
    dec_batch, n_pages = page_table.shape
    page_rows = cache_k.shape[2]
    total_steps = batch * heads * nq
    assert total_steps % dec_batch == 0 and n_pages % (total_steps // dec_batch) == 0
    sample_steps = total_steps // dec_batch
    pages_per_step = n_pages // sample_steps
    pt_flat = page_table.reshape(-1)

    def sample_pos(b, h, i):
        step = (b * heads + h) * nq + i
        return step // sample_steps, step % sample_steps

    def page_spec(p):
        def index(b, h, i, pt):
            s, j = sample_pos(b, h, i)
            return (layer, pt[s * n_pages + j * pages_per_step + p], 0, 0)
        return pl.BlockSpec((None, None, page_rows, HEAD_DIM), index)

    seq_block = lambda shape: pl.BlockSpec(shape, lambda b, h, i, pt: (sample_pos(b, h, i)[0], 0))
    vec = lambda a: pl.BlockSpec(a.shape, lambda b, h, i, pt: (0, 0))
    qp_spec = pl.BlockSpec((tq, HEAD_DIM), lambda b, h, i, pt: (b * nq + i, h))
    kv_spec = pl.BlockSpec((seq, HEAD_DIM), lambda b, h, i, pt: (b, h))
    rows_s = heads * 2 * t_new
    grid_spec = pltpu.PrefetchScalarGridSpec(
        num_scalar_prefetch=1,
        grid=(batch, heads, nq),
        in_specs=([qp_spec, kv_spec, kv_spec,
                   seq_block((t_new, bw)), seq_block((t_new * heads, HEAD_DIM)),
                   seq_block((t_new * heads, HEAD_DIM))]
                  + [page_spec(p) for p in range(pages_per_step)] * 2
                  + [vec(lq1), vec(lk1), vec(lq2), vec(lk2), vec(sub)]),
        out_specs=[qp_spec, seq_block((t_new, bw))],
        scratch_shapes=[pltpu.VMEM((2 * tq, HEAD_DIM), BF16),
                        pltpu.VMEM((2 * tq, LANES), F32),
                        pltpu.VMEM((2 * tq, LANES), F32),
                        pltpu.VMEM((2 * tq, HEAD_DIM), F32),
                        pltpu.VMEM((rows_s, HEAD_DIM), BF16),
                        pltpu.VMEM((rows_s, pages_per_step * page_rows), F32),
                        pltpu.VMEM((rows_s, LANES), F32),
                        pltpu.VMEM((rows_s, LANES), F32),
                        pltpu.VMEM((rows_s, HEAD_DIM), F32)])
    return pl.pallas_call(
        functools.partial(_attention_kernel, pages_per_step=pages_per_step, heads=heads,
                          sample_steps=sample_steps, lam_init=lam_init),
        grid_spec=grid_spec,
        out_shape=[jax.ShapeDtypeStruct((tp, bw), BF16), jax.ShapeDtypeStruct((ts, bw), F32)],
        compiler_params=_params(("arbitrary", "arbitrary", "arbitrary")),
        name="attention",
    )(pt_flat, q_p, kb_p, vb_p, q_s, k_new, v_new,
      *([cache_k] * pages_per_step), *([cache_v] * pages_per_step), lq1, lk1, lq2, lk2, sub)


def _swiglu_residual(x1, g_ref, w_in_ref, w_out_ref, ff_chunk):
    d_ff = w_out_ref.shape[0]
    h = _rms(x1, g_ref[...]).astype(BF16)
    acc = x1
    for c0 in range(0, d_ff, ff_chunk):
        gate = jnp.dot(h, w_in_ref[:, c0:c0 + ff_chunk], preferred_element_type=F32)
        up = jnp.dot(h, w_in_ref[:, d_ff + c0:d_ff + c0 + ff_chunk], preferred_element_type=F32)
        act = (gate * jax.nn.sigmoid(gate) * up).astype(BF16)
        acc = acc + jnp.dot(act, w_out_ref[c0:c0 + ff_chunk, :], preferred_element_type=F32)
    return acc


def _mix_ffn_kernel(xp_ref, ap_ref, bp_ref, xs_ref, as_ref, bs_ref, wo_ref, g_ref, w_in_ref,
                    w_out_ref, op_ref, os_ref, slabs, *, ff_chunk, row_tokens):
    def mixed(x_ref, a_ref, b_ref):
        ab = jnp.concatenate([a_ref[...].astype(BF16), b_ref[...].astype(BF16)], axis=1)
        x1 = x_ref[...] + jnp.dot(ab, wo_ref[...], preferred_element_type=F32)
        return _swiglu_residual(x1, g_ref, w_in_ref, w_out_ref, ff_chunk)

    @pl.when(pl.program_id(0) < pl.num_programs(0) - 1)
    def _():
        tm, d = xp_ref.shape
        y = mixed(xp_ref, ap_ref, bp_ref)
        for j in range(d // LANES):
            slabs[j] = y[:, j * LANES:(j + 1) * LANES]
        for t in range(row_tokens):
            for j in range(d // LANES):
                op_ref[:, t * d + j * LANES:t * d + (j + 1) * LANES] = (
                    slabs[j, pl.ds(t, tm // row_tokens, stride=row_tokens), :])

    @pl.when(pl.program_id(0) == pl.num_programs(0) - 1)
    def _():
        os_ref[...] = mixed(xs_ref, as_ref, bs_ref)


def _ffn_final_kernel(xp_ref, xs_ref, g_ref, w_in_ref, w_out_ref, gf_ref, op_ref, os_ref, slabs,
                      *, ff_chunk, row_tokens):
    def final(x):
        return _rms(_swiglu_residual(x, g_ref, w_in_ref, w_out_ref, ff_chunk), gf_ref[...])

    @pl.when(pl.program_id(0) < pl.num_programs(0) - 1)
    def _():
        r = xp_ref.shape[0]
        d = xp_ref.shape[1] // row_tokens
        for t in range(row_tokens):
            for j in range(d // LANES):
                slabs[j, pl.ds(t, r, stride=row_tokens), :] = (
                    xp_ref[:, t * d + j * LANES:t * d + (j + 1) * LANES])
        op_ref[...] = final(jnp.concatenate([slabs[j] for j in range(d // LANES)], axis=1))

    @pl.when(pl.program_id(0) == pl.num_programs(0) - 1)
    def _():
        os_ref[...] = final(xs_ref[...])


def _prompt_tiles(n_tiles):
    return lambda i: (jnp.minimum(i, n_tiles - 1), 0)


def _whole(arr):
    return pl.BlockSpec(arr.shape, lambda i: (0,) * arr.ndim)


def _mix_ffn(xp, ap, bp, xs, a_s, b_s, wo, g, w_in, w_out, *, layer, tm, ff_chunk, row_tokens):
    t, d = xp.shape
    n_tiles = t // tm
    row = lambda arr: pl.BlockSpec((tm, arr.shape[1]), _prompt_tiles(n_tiles))
    return pl.pallas_call(
        functools.partial(_mix_ffn_kernel, ff_chunk=ff_chunk, row_tokens=row_tokens),
        grid=(n_tiles + 1,),
        in_specs=[row(xp), row(ap), row(bp), _whole(xs), _whole(a_s), _whole(b_s),
                  _const_spec(wo.shape), _const_spec(g.shape),
                  _layer_spec(w_in.shape, layer), _layer_spec(w_out.shape, layer)],
        out_specs=[pl.BlockSpec((tm // row_tokens, row_tokens * d), _prompt_tiles(n_tiles)),
                   _whole(xs)],
        out_shape=[jax.ShapeDtypeStruct((t // row_tokens, row_tokens * d), F32),
                   jax.ShapeDtypeStruct(xs.shape, F32)],
        scratch_shapes=[pltpu.VMEM((d // LANES, tm, LANES), F32)],
        compiler_params=_params(("arbitrary",)),
        name="mix_ffn",
    )(xp, ap, bp, xs, a_s, b_s, wo, g, w_in, w_out)


def _ffn_final(xp, xs, g, w_in, w_out, gf, *, layer, tm, ff_chunk, row_tokens):
    rows, width = xp.shape
    d = width // row_tokens
    t = rows * row_tokens
    n_tiles = t // tm
    return pl.pallas_call(
        functools.partial(_ffn_final_kernel, ff_chunk=ff_chunk, row_tokens=row_tokens),
        grid=(n_tiles + 1,),
        in_specs=[pl.BlockSpec((tm // row_tokens, width), _prompt_tiles(n_tiles)), _whole(xs),
                  _const_spec(g.shape), _layer_spec(w_in.shape, layer),
                  _layer_spec(w_out.shape, layer), _const_spec(gf.shape)],
        out_specs=[pl.BlockSpec((tm, d), _prompt_tiles(n_tiles)), _whole(xs)],
        out_shape=[jax.ShapeDtypeStruct((t, d), F32), jax.ShapeDtypeStruct(xs.shape, F32)],
        scratch_shapes=[pltpu.VMEM((d // LANES, tm, LANES), F32)],
        compiler_params=_params(("arbitrary",)),
        name="ffn_final",
    )(xp, xs, g, w_in, w_out, gf)


def _expand_groups(compact, r1, c1):
    n_slabs, rows, k = compact.shape
    width = (k // c1) * SLAB_GROUPS * c1
    src = lax.broadcasted_iota(jnp.int32, (k, width), 0)
    dst = lax.broadcasted_iota(jnp.int32, (k, width), 1)
    select = (src == (dst // (SLAB_GROUPS * c1)) * c1 + dst % c1).astype(F32)
    full = jnp.dot(compact.reshape(n_slabs * rows, k), select,
                   precision=lax.Precision.HIGHEST).reshape(n_slabs, rows, width)
    row_g = (lax.broadcasted_iota(jnp.int32, full.shape, 1) // r1) % SLAB_GROUPS
    col_h = (lax.broadcasted_iota(jnp.int32, full.shape, 2) // c1) % SLAB_GROUPS
    return jnp.where(row_g == col_h, full, 0.0).astype(BF16)


def _s5_rows_kernel(x_ref, g_ref, s0re_ref, s0im_ref, coef_ref, wb_ref, wy_ref, wc_ref, d_ref,
                    wglu_ref, bglu_ref, o_ref, sre_ref, sim_ref,
                    re_s, im_s, gel_s, car_re, car_im, *, row_tokens, n_seq):
    r, width = x_ref.shape
    d = width // row_tokens
    n_slabs = d // LANES
    slab_state = re_s.shape[1] // n_slabs

    xs = [x_ref[:, t * d:(t + 1) * d] for t in range(row_tokens)]
    hs = [_rms(x, g_ref[...]) for x in xs]
    hb = [h.astype(BF16) for h in hs]

    def slab_lhs(j):
        return jnp.concatenate([h[:, j * LANES:(j + 1) * LANES] for h in hb], axis=1)

    def state_cols(j):
        return slice(j * slab_state, (j + 1) * slab_state)

    def window(j):
        w = jnp.dot(slab_lhs(j), wb_ref[j], preferred_element_type=F32)
        re_s[:, state_cols(j)] = w[:, :slab_state]
        im_s[:, state_cols(j)] = w[:, slab_state:]

    def scan_rows(j):
        st = state_cols(j)
        cr, ci = car_re[:, st], car_im[:, st]
        first = lax.broadcasted_iota(jnp.int32, (SUBLANES, slab_state), 0) == 0
        for i in range(r // SUBLANES):
            rows = slice(i * SUBLANES, (i + 1) * SUBLANES)
            wr, wi = re_s[rows, st], im_s[rows, st]
            for k, shift in enumerate(SCAN_SHIFTS):
                ar, ai = coef_ref[2 * k, :, st], coef_ref[2 * k + 1, :, st]
                sr = pltpu.roll(wr, shift, 0)
                si = pltpu.roll(wi, shift, 0)
                wr, wi = wr + ar * sr - ai * si, wi + ar * si + ai * sr
            pr = coef_ref[2 * len(SCAN_SHIFTS), :, st]
            pi = coef_ref[2 * len(SCAN_SHIFTS) + 1, :, st]
            wr, wi = wr + pr * cr - pi * ci, wi + pr * ci + pi * cr
            re_s[rows, st] = jnp.where(first, cr, pltpu.roll(wr, 1, 0))
            im_s[rows, st] = jnp.where(first, ci, pltpu.roll(wi, 1, 0))
            cr, ci = wr[SUBLANES - 1:SUBLANES, :], wi[SUBLANES - 1:SUBLANES, :]
        car_re[:, st] = cr
        car_im[:, st] = ci
        sre_ref[:, st] = cr
        sim_ref[:, st] = ci

    def slab_outputs(j):
        cols = slice(j * LANES, (j + 1) * LANES)
        st = state_cols(j)
        s_cat = jnp.concatenate([re_s[:, st], im_s[:, st]], axis=1).astype(BF16)
        y = (jnp.dot(slab_lhs(j), wy_ref[j], preferred_element_type=F32)
             + jnp.dot(s_cat, wc_ref[j], preferred_element_type=F32))
        for t in range(row_tokens):
            yt = y[:, t * LANES:(t + 1) * LANES] + d_ref[:, cols] * hs[t][:, cols]
            gel_s[t * r:(t + 1) * r, cols] = jax.nn.gelu(yt).astype(BF16)

    if n_seq == 1:
        @pl.when(pl.program_id(1) == 0)
        def _():
            car_re[...] = s0re_ref[...]
            car_im[...] = s0im_ref[...]

        for j in range(n_slabs):
            window(j)
            scan_rows(j)
            slab_outputs(j)
    else:
        for j in range(n_slabs):
            window(j)
        a_re = coef_ref[0][1:2, :]
        a_im = coef_ref[1][1:2, :]
        cr, ci = s0re_ref[...], s0im_ref[...]
        for n in range(r // n_seq):
            rows = slice(n * n_seq, (n + 1) * n_seq)
            wr, wi = re_s[rows, :], im_s[rows, :]
            re_s[rows, :] = cr
            im_s[rows, :] = ci
            cr, ci = a_re * cr - a_im * ci + wr, a_re * ci + a_im * cr + wi
        sre_ref[...] = cr
        sim_ref[...] = ci
        for j in range(n_slabs):
            slab_outputs(j)
    z = jnp.dot(gel_s[...], wglu_ref[...], preferred_element_type=F32) + bglu_ref[...]
    for t in range(row_tokens):
        zt = z[t * r:(t + 1) * r, :]
        o_ref[:, t * d:(t + 1) * d] = xs[t] + zt[:, :d] * jax.nn.sigmoid(zt[:, d:])


def _s5_rows_mix(x, g, s0re, s0im, coef, wb, wy, wc, dskip, wglu, bglu, *, batch, rows_per_seq,
                 r, row_tokens, step_major, name):
    rows, width = x.shape
    d = width // row_tokens
    n_state = s0re.shape[-1]
    if step_major:
        grid = (1, 1)
        r = rows
        state_spec = pl.BlockSpec((batch, n_state), lambda b, i: (0, 0))
        x_spec = pl.BlockSpec((rows, width), lambda b, i: (0, 0))
    else:
        nt = rows_per_seq // r
        grid = (batch, nt)
        state_spec = pl.BlockSpec((None, 1, n_state), lambda b, i: (b, 0, 0))
        x_spec = pl.BlockSpec((r, width), lambda b, i: (b * nt + i, 0))
    consts = (g, coef, wb, wy, wc, dskip, wglu, bglu)
    x1, sre, sim = pl.pallas_call(
        functools.partial(_s5_rows_kernel, row_tokens=row_tokens,
                          n_seq=batch if step_major else 1),
        grid=grid,
        in_specs=[x_spec, _const_spec(g.shape), state_spec, state_spec]
                 + [_const_spec(c.shape) for c in consts[1:]],
        out_specs=[x_spec, state_spec, state_spec],
        out_shape=[jax.ShapeDtypeStruct((rows, width), F32),
                   jax.ShapeDtypeStruct(s0re.shape, F32),
                   jax.ShapeDtypeStruct(s0im.shape, F32)],
        scratch_shapes=[pltpu.VMEM((r, n_state), F32), pltpu.VMEM((r, n_state), F32),
                        pltpu.VMEM((r * row_tokens, d), BF16),
                        pltpu.VMEM((1, n_state), F32), pltpu.VMEM((1, n_state), F32)],
        compiler_params=_params(("parallel", "arbitrary")),
        name=name,
    )(x, g, s0re, s0im, *consts[1:])
    return x1, sre, sim


def _s5_row_tables(a_re, a_im, log_dt, b_re, b_im, c_re, c_im, row_tokens):
    groups, n_p = a_re.shape
    p_tok = row_tokens
    n_slabs = groups // SLAB_GROUPS
    dt = jnp.exp(log_dt)[:, None]
    ks = jnp.arange(p_tok * SUBLANES + 1, dtype=F32)[:, None, None]
    mag = jnp.exp((a_re * dt)[None] * ks)
    ang = (a_im * dt)[None] * ks
    pw_re, pw_im = mag * jnp.cos(ang), mag * jnp.sin(ang)
    den = a_re * a_re + a_im * a_im
    q_re = ((pw_re[1] - 1.0) * a_re + pw_im[1] * a_im) / den
    q_im = (pw_im[1] * a_re - (pw_re[1] - 1.0) * a_im) / den
    bb_re = q_re[:, :, None] * b_re - q_im[:, :, None] * b_im
    bb_im = q_re[:, :, None] * b_im + q_im[:, :, None] * b_re

    rows = jnp.arange(SUBLANES)
    coef = []
    for shift in SCAN_SHIFTS:
        keep = (rows >= shift)[:, None].astype(F32)
        coef += [pw_re[p_tok * shift].reshape(1, -1) * keep,
                 pw_im[p_tok * shift].reshape(1, -1) * keep]
    pick = lambda pw, idx: jnp.stack([pw[k] for k in idx])
    carry = [p_tok * (r + 1) for r in range(SUBLANES)]
    coef += [pick(pw_re, carry).reshape(SUBLANES, -1), pick(pw_im, carry).reshape(SUBLANES, -1)]
    coef = jnp.stack(coef)

    back = [p_tok - 1 - s for s in range(p_tok)]
    er, ei = pick(pw_re, back)[:, :, :, None], pick(pw_im, back)[:, :, :, None]
    lay_b = lambda m: m.reshape(p_tok, n_slabs, SLAB_GROUPS, n_p, SSM_GROUP).transpose(
        1, 0, 2, 4, 3).reshape(n_slabs, -1, n_p)
    wb = _expand_groups(jnp.concatenate([lay_b(er * bb_re - ei * bb_im),
                                         lay_b(er * bb_im + ei * bb_re)], axis=2), SSM_GROUP, n_p)

    kr, ki = pw_re[:p_tok + 1][:, :, None, :], pw_im[:p_tok + 1][:, :, None, :]
    ck_re, ck_im = c_re * kr - c_im * ki, c_re * ki + c_im * kr

    lay_c = lambda m: m.reshape(p_tok, n_slabs, SLAB_GROUPS, SSM_GROUP, n_p).transpose(
        1, 2, 4, 0, 3).reshape(n_slabs, SLAB_GROUPS * n_p, -1)
    wc = _expand_groups(jnp.concatenate([lay_c(ck_re[1:]), -lay_c(ck_im[1:])], axis=1),
                        n_p, SSM_GROUP)

    bt_re, bt_im = bb_re.transpose(0, 2, 1)[None, :, :, None, :], bb_im.transpose(0, 2, 1)[
        None, :, :, None, :]
    lag = jnp.sum(ck_re[:p_tok, :, None] * bt_re - ck_im[:p_tok, :, None] * bt_im, axis=-1)
    lag = lag.reshape(p_tok, n_slabs, SLAB_GROUPS * SSM_GROUP, SSM_GROUP)
    zero = jnp.zeros_like(lag[0])
    toep = jnp.concatenate(
        [jnp.concatenate([lag[t - s] if t >= s else zero for t in range(p_tok)], axis=2)
         for s in range(p_tok)], axis=1)
    wy = _expand_groups(toep, SSM_GROUP, SSM_GROUP)
    return coef, wb, wy, wc


def kernel(x_prompt, x_sample, cache_k, cache_v, page_table, state_ssm_re, state_ssm_im,
           norm_mix, norm_ffn, norm_final, w_in_even, w_out_even, sgu_norm, sgu_w, sgu_b,
           lambda_q1, lambda_k1, lambda_q2, lambda_k2, attn_subln,
           ssm_a_re, ssm_a_im, ssm_log_dt, ssm_b_re, ssm_b_im, ssm_c_re, ssm_c_im, ssm_d,
           w_glu, b_glu, w_ffn_in, w_ffn_out):
    batch, seq, d = x_prompt.shape
    dec_batch, t_new, _ = x_sample.shape
    depth = norm_mix.shape[0]
    assert depth == 2 and seq % CHUNK == 0 and t_new == SUBLANES
    tp = batch * seq
    ts = dec_batch * t_new
    xp = x_prompt.reshape(tp, d)
    xs = x_sample.reshape(ts, d)
    tm_prompt = 512
    ff_chunk = 256
    row = lambda v: v.reshape(1, -1)

    lam_init = 0.8 - 0.6 * math.exp(-0.3 * 0)
    w_in0 = w_in_even[0].astype(BF16)
    a_width = sgu_norm.shape[1]
    gw = a_width // A_GROUPS
    tril = jnp.tril(jnp.ones((CHUNK, CHUNK), bool))
    mix_p = jnp.where(tril, sgu_w[0], 0).astype(BF16)
    bias_p = jnp.repeat(sgu_b[0].T, gw, axis=1)
    small = jnp.where(tril[:t_new, :t_new], sgu_w[0][:, :t_new, :t_new], 0)
    row_i = lax.broadcasted_iota(jnp.int32, (ts, ts), 0)
    col_i = lax.broadcasted_iota(jnp.int32, (ts, ts), 1)
    rep = (row_i[:, :t_new] % t_new == col_i[:, :t_new]).astype(F32)
    tiled = jnp.einsum('at,gts,bs->gab', rep, small, rep, precision=lax.Precision.HIGHEST)
    mix_s = jnp.where(row_i // t_new == col_i // t_new, tiled, 0.0).astype(BF16)
    bias_s = jnp.tile(bias_p[:t_new], (dec_batch, 1))
    lam_vecs = [row(lambda_q1[0]), row(lambda_k1[0]), row(lambda_q2[0]), row(lambda_k2[0]),
                row(attn_subln[0])]

    later_weights = (w_ffn_in.reshape(-1, w_ffn_in.shape[-1]),
                     w_ffn_out.reshape(-1, w_ffn_out.shape[-1]), w_glu[0], w_out_even[0])
    (a_p, q_p, k_p, vb_p, kb_p, vbb_p), (w_ffn_in_b, w_ffn_out_b, w_glu0, w_out0) = _even_in(
        xp, row(norm_mix[0]), w_in0, row(sgu_norm[0]), mix_p, bias_p, tm=tm_prompt, prompt=True,
        cast_weights=later_weights)
    w_ffn_in_b = w_ffn_in_b.reshape(w_ffn_in.shape)
    w_ffn_out_b = w_ffn_out_b.reshape(w_ffn_out.shape)
    (a_s, q_s, k_s, vb_s, vn_s), _ = _even_in(
        xs, row(norm_mix[0]), w_in0, row(sgu_norm[0]), mix_s, bias_s, tm=ts, prompt=False)
    n_layers, n_phys, page_size, heads, _ = cache_k.shape
    page_view = (n_layers, n_phys, page_size * heads, HEAD_DIM)
    b_p, b_s = _attention(page_table, q_p, kb_p, vbb_p, q_s, k_s, vb_s,
                          cache_k.reshape(page_view), cache_v.reshape(page_view), 0, *lam_vecs,
                          batch=batch, seq=seq, tq=512, t_new=t_new, lam_init=lam_init)
    xp, xs = _mix_ffn(xp, a_p, b_p, xs, a_s, b_s, w_out0, row(norm_ffn[0]), w_ffn_in_b,
                      w_ffn_out_b, layer=0, tm=tm_prompt, ff_chunk=ff_chunk,
                      row_tokens=ROW_TOKENS)

    groups, n_p = ssm_a_re.shape[1:]
    n_state = groups * n_p
    ssm = (ssm_a_re[0], ssm_a_im[0], ssm_log_dt[0], ssm_b_re[0], ssm_b_im[0], ssm_c_re[0],
           ssm_c_im[0])
    s5_args = (*_s5_row_tables(*ssm, ROW_TOKENS), row(ssm_d[0]), w_glu0, row(b_glu[0]))
    zeros = jnp.zeros((batch, 1, n_state), F32)
    xp, p_re, p_im = _s5_rows_mix(xp, row(norm_mix[1]), zeros, zeros, *s5_args, batch=batch,
                                  rows_per_seq=seq // ROW_TOKENS, r=S5_BLOCK_ROWS,
                                  row_tokens=ROW_TOKENS, step_major=False, name="s5_prompt")
    steps = t_new // ROW_TOKENS
    pack = lambda v: v.reshape(dec_batch, steps, -1).transpose(1, 0, 2).reshape(
        steps * dec_batch, -1)
    unpack = lambda v: v.reshape(steps, dec_batch, -1).transpose(1, 0, 2).reshape(ts, d)
    xs, s_re, s_im = _s5_rows_mix(pack(xs), row(norm_mix[1]),
                                  state_ssm_re[0].reshape(dec_batch, n_state),
                                  state_ssm_im[0].reshape(dec_batch, n_state), *s5_args,
                                  batch=dec_batch, rows_per_seq=steps, r=None,
                                  row_tokens=ROW_TOKENS, step_major=True, name="s5_sample")
    xs = unpack(xs)
    yp, ys = _ffn_final(xp, xs, row(norm_ffn[1]), w_ffn_in_b, w_ffn_out_b, row(norm_final),
                        layer=1, tm=tm_prompt, ff_chunk=ff_chunk, row_tokens=ROW_TOKENS)

    st = lambda s, n: s.reshape(1, n, groups, n_p)
    return (yp.reshape(batch, seq, d), ys.reshape(dec_batch, t_new, d),
            k_p.reshape(1, batch, seq, heads, HEAD_DIM), vb_p.reshape(1, batch, seq, heads, HEAD_DIM),
            k_s.reshape(1, dec_batch, t_new, heads, HEAD_DIM),
            vb_s.reshape(1, dec_batch, t_new, heads, HEAD_DIM),
            vn_s.reshape(1, dec_batch, t_new, a_width),
            st(p_re, batch), st(p_im, batch), st(s_re, dec_batch), st(s_im, dec_batch))
```

```python
import functools
import math

import jax
import jax.numpy as jnp
from jax import lax
from jax.experimental import pallas as pl
from jax.experimental.pallas import tpu as pltpu

F32 = jnp.float32
BF16 = jnp.bfloat16

EPS = 1e-6
LANES = 128
SUBLANES = 8
V7X_VMEM_BYTES = 64 * 1024 * 1024
VMEM_LIMIT = V7X_VMEM_BYTES * 7 // 8
MASK_VALUE = -0.7 * float(jnp.finfo(jnp.float32).max)

CHUNK = 128
A_GROUPS = 4
HEAD_DIM = 128
QK_HALF = 64
LOG2E = math.log2(math.e)
SSM_GROUP = 16
SLAB_GROUPS = LANES // SSM_GROUP
SCAN_SHIFTS = (1, 2, 4)
ROW_TOKENS = 4
S5_BLOCK_ROWS = 128

def _const_spec(shape):
    zeros = (0,) * len(shape)
    return pl.BlockSpec(shape, lambda *_: zeros, pipeline_mode=pl.Buffered(1))


def _layer_spec(shape, layer):
    index = (layer,) + (0,) * (len(shape) - 1)
    return pl.BlockSpec((None,) + tuple(shape[1:]), lambda *_: index, pipeline_mode=pl.Buffered(1))


def _rms(x, g):
    return x * lax.rsqrt(jnp.mean(x * x, axis=-1, keepdims=True) + EPS) * g


def _params(semantics):
    return pltpu.CompilerParams(dimension_semantics=semantics, vmem_limit_bytes=VMEM_LIMIT)


def _even_in_kernel(x_ref, g_ref, w_ref, sn_ref, mix_ref, bias_ref, *rest, prompt, a_width,
                    n_cast):
    cast_in, outs, cast_out = rest[:n_cast], rest[n_cast:len(rest) - n_cast], rest[len(rest) - n_cast:]
    for src, dst in zip(cast_in, cast_out):
        dst[...] = src[...].astype(BF16)
    if prompt:
        a_ref, q_ref, k_ref, vb_ref, kb_ref, vbb_ref = outs
    else:
        a_ref, q_ref, k_ref, vb_ref, vn_ref = outs
    tm = x_ref.shape[0]
    rows_per_mix = mix_ref.shape[1]
    gw = a_width // A_GROUPS
    h = _rms(x_ref[...], g_ref[...]).astype(BF16)

    def proj(c0, width):
        return jnp.dot(h, w_ref[:, c0:c0 + width], preferred_element_type=F32)

    u = jax.nn.gelu(proj(0, a_width))
    v = _rms(jax.nn.gelu(proj(a_width, a_width)), sn_ref[...])
    if not prompt:
        vn_ref[...] = v
    v16 = v.astype(BF16)
    o = 2 * a_width
    bw = q_ref.shape[1]
    q_ref[...] = (proj(o, bw) * (QK_HALF ** -0.5 * LOG2E)).astype(q_ref.dtype)
    k = proj(o + bw, bw)
    vb = proj(o + 2 * bw, bw)
    heads = bw // HEAD_DIM
    for hd in range(heads):
        k_ref[pl.ds(hd, tm, stride=heads), :] = k[:, hd * HEAD_DIM:(hd + 1) * HEAD_DIM]
        vb_ref[pl.ds(hd, tm, stride=heads), :] = vb[:, hd * HEAD_DIM:(hd + 1) * HEAD_DIM]
    if prompt:
        kb_ref[...] = k.astype(BF16)
        vbb_ref[...] = vb.astype(BF16)
    for c in range(tm // rows_per_mix):
        r0 = c * rows_per_mix
        for g in range(A_GROUPS):
            c0 = g * gw
            gate = jnp.dot(mix_ref[g], v16[r0:r0 + rows_per_mix, c0:c0 + gw],
                           preferred_element_type=F32) + bias_ref[:, c0:c0 + gw]
            a_ref[r0:r0 + rows_per_mix, c0:c0 + gw] = (
                u[r0:r0 + rows_per_mix, c0:c0 + gw] * gate).astype(a_ref.dtype)


def _even_in(x, g, w, sn, mix, bias, *, tm, prompt, cast_weights=()):
    t, d = x.shape
    steps = t // tm
    cast_specs = [pl.BlockSpec((c.shape[0] // steps, c.shape[1]), lambda i: (i, 0))
                  for c in cast_weights]
    a_width = sn.shape[1]
    bw = (w.shape[1] - 2 * a_width) // 3
    act = BF16 if prompt else F32
    row = lambda width: pl.BlockSpec((tm, width), lambda i: (i, 0))
    heads = bw // HEAD_DIM
    head_rows = pl.BlockSpec((tm * heads, HEAD_DIM), lambda i: (i, 0))
    out_shape = [jax.ShapeDtypeStruct((t, a_width), act), jax.ShapeDtypeStruct((t, bw), act),
                 jax.ShapeDtypeStruct((t * heads, HEAD_DIM), F32),
                 jax.ShapeDtypeStruct((t * heads, HEAD_DIM), F32)]
    out_specs = [row(a_width), row(bw), head_rows, head_rows]
    if prompt:
        out_shape += [jax.ShapeDtypeStruct((t, bw), BF16)] * 2
        out_specs += [row(bw), row(bw)]
    else:
        out_shape += [jax.ShapeDtypeStruct((t, a_width), F32)]
        out_specs += [row(a_width)]
    n_out = len(out_shape)
    outs = pl.pallas_call(
        functools.partial(_even_in_kernel, prompt=prompt, a_width=a_width,
                          n_cast=len(cast_weights)),
        grid=(steps,),
        in_specs=[row(d), _const_spec(g.shape), _const_spec(w.shape), _const_spec(sn.shape),
                  _const_spec(mix.shape), _const_spec(bias.shape)] + cast_specs,
        out_specs=out_specs + cast_specs,
        out_shape=out_shape + [jax.ShapeDtypeStruct(c.shape, BF16) for c in cast_weights],
        compiler_params=_params(("parallel",)),
        name="even_in_prompt" if prompt else "even_in_sample",
    )(x, g, w, sn, mix, bias, *cast_weights)
    return outs[:n_out], outs[n_out:]


def _split_halves(q):
    lane = lax.broadcasted_iota(jnp.int32, q.shape, 1)
    zero = jnp.zeros_like(q)
    return jnp.where(lane < QK_HALF, q, zero), jnp.where(lane >= QK_HALF, q, zero)


def _softmax_step(s, vblk, m_ref, l_ref, acc_ref, rows):
    tiles = [s[:, t * LANES:(t + 1) * LANES] for t in range(s.shape[1] // LANES)]
    m_prev = m_ref[rows, :]
    m_next = jnp.maximum(
        m_prev, jnp.max(functools.reduce(jnp.maximum, tiles), axis=1, keepdims=True))
    p_tiles = [jnp.exp2(t - m_next) for t in tiles]
    alpha = jnp.exp2(m_prev - m_next)
    l_ref[rows, :] = alpha * l_ref[rows, :] + functools.reduce(jnp.add, p_tiles)
    p = jnp.concatenate([t.astype(BF16) for t in p_tiles], axis=1)
    acc_ref[rows, :] = alpha * acc_ref[rows, :] + jnp.dot(p, vblk, preferred_element_type=F32)
    m_ref[rows, :] = m_next


def _row_sum(l):
    return jnp.sum(l, axis=1, keepdims=True)


def _diff_lambda(lq1, lk1, lq2, lk2, lam_init):
    e1 = jnp.exp(jnp.sum(lq1[...] * lk1[...], axis=1, keepdims=True))
    e2 = jnp.exp(jnp.sum(lq2[...] * lk2[...], axis=1, keepdims=True))
    return e1 - e2 + lam_init


def _head_out(o1, o2, lam, sub, lam_init):
    o = o1 - lam * o2
    return _rms(o, sub) * (1.0 - lam_init)


def _prompt_attn_step(i, q_ref, k_ref, v_ref, lam, sub_ref, o_ref, qs_ref, m_ref, l_ref, acc_ref,
                      lam_init):
    tq = q_ref.shape[0]
    q1, q2 = _split_halves(q_ref[...])
    qs_ref[0:tq, :] = q1
    qs_ref[tq:2 * tq, :] = q2
    m_ref[...] = jnp.full(m_ref.shape, MASK_VALUE, F32)
    l_ref[...] = jnp.zeros(l_ref.shape, F32)
    acc_ref[...] = jnp.zeros(acc_ref.shape, F32)
    all_rows = slice(0, 2 * tq)

    def scores(j):
        kv_rows = pl.ds(pl.multiple_of(j * tq, tq), tq)
        s = lax.dot_general(qs_ref[...], k_ref[kv_rows, :], (((1,), (1,)), ((), ())),
                            preferred_element_type=F32)
        return s, v_ref[kv_rows, :]

    def full_block(j):
        s, vblk = scores(j)
        _softmax_step(s, vblk, m_ref, l_ref, acc_ref, all_rows)

    def full_quad(jj, carry):
        for u in range(4):
            full_block(4 * jj + u)
        return carry

    lax.fori_loop(0, i // 4, full_quad, 0)

    @pl.when(i % 4 >= 2)
    def _():
        full_block(4 * (i // 4))
        full_block(4 * (i // 4) + 1)

    @pl.when(i % 2 == 1)
    def _():
        full_block(i - 1)
    s, vblk = scores(i)
    row = lax.broadcasted_iota(jnp.int32, s.shape, 0)
    col = lax.broadcasted_iota(jnp.int32, s.shape, 1)
    qpos = jnp.where(row >= tq, row - tq, row)
    s = jnp.where(qpos >= col, s, MASK_VALUE)
    _softmax_step(s, vblk, m_ref, l_ref, acc_ref, all_rows)

    o1 = acc_ref[0:tq, :] / _row_sum(l_ref[0:tq, :])
    o2 = acc_ref[tq:2 * tq, :] / _row_sum(l_ref[tq:2 * tq, :])
    o_ref[...] = _head_out(o1, o2, lam, sub_ref[...], lam_init).astype(o_ref.dtype)


def _head_match(shape, heads, rows_per_head):
    row = lax.broadcasted_iota(jnp.int32, shape, 0)
    col = lax.broadcasted_iota(jnp.int32, shape, 1)
    return row, col, lax.rem(col, heads) == row // rows_per_head


def _sample_attn_init(q_ref, qs_ref, bias_ref, m_ref, l_ref, acc_ref, heads):
    pieces = []
    for h in range(heads):
        pieces += _split_halves(q_ref[:, h * HEAD_DIM:(h + 1) * HEAD_DIM])
    qs_ref[...] = jnp.concatenate(pieces, axis=0).astype(BF16)
    _, _, ok = _head_match(bias_ref.shape, heads, 2 * q_ref.shape[0])
    bias_ref[...] = jnp.where(ok, 0.0, MASK_VALUE)
    m_ref[...] = jnp.full(m_ref.shape, MASK_VALUE, F32)
    l_ref[...] = jnp.zeros(l_ref.shape, F32)
    acc_ref[...] = jnp.zeros(acc_ref.shape, F32)


def _sample_attn_step(is_last, q_ref, kn_ref, vn_ref, k_pages, v_pages, lam, sub_ref, o_ref,
                      qs_ref, bias_ref, m_ref, l_ref, acc_ref, heads, lam_init):
    pages_per_step = len(k_pages)
    t_new = q_ref.shape[0]
    rows_per_head = 2 * t_new
    all_rows = slice(0, heads * rows_per_head)

    def scores(kblk):
        return lax.dot_general(qs_ref[...], kblk.astype(BF16), (((1,), (1,)), ((), ())),
                               preferred_element_type=F32)

    kblk = jnp.concatenate([k_pages[p][...] for p in range(pages_per_step)], axis=0)
    vblk = jnp.concatenate([v_pages[p][...] for p in range(pages_per_step)], axis=0)
    _softmax_step(scores(kblk) + bias_ref[...], vblk.astype(BF16), m_ref, l_ref, acc_ref, all_rows)

    @pl.when(is_last)
    def _():
        pad = jnp.zeros((LANES - kn_ref.shape[0], HEAD_DIM), F32)
        s = scores(jnp.concatenate([kn_ref[...], pad], axis=0))
        row, col, ok = _head_match(s.shape, heads, rows_per_head)
        s = jnp.where(ok, s, MASK_VALUE)
        s = jnp.where(col // heads <= lax.rem(row, t_new), s, MASK_VALUE)
        vblk = jnp.concatenate([vn_ref[...], pad], axis=0)
        _softmax_step(s, vblk.astype(BF16), m_ref, l_ref, acc_ref, all_rows)
        for h in range(heads):
            r0 = h * rows_per_head
            o1 = acc_ref[r0:r0 + t_new, :] / _row_sum(l_ref[r0:r0 + t_new, :])
            o2 = (acc_ref[r0 + t_new:r0 + rows_per_head, :]
                  / _row_sum(l_ref[r0 + t_new:r0 + rows_per_head, :]))
            o_ref[:, h * HEAD_DIM:(h + 1) * HEAD_DIM] = _head_out(
                o1, o2, lam, sub_ref[...], lam_init)


def _attention_kernel(pt_ref, qp_ref, kp_ref, vp_ref, qs_in_ref, kn_ref, vn_ref, *rest,
                      pages_per_step, heads, sample_steps, lam_init):
    del pt_ref
    k_pages = rest[:pages_per_step]
    v_pages = rest[pages_per_step:2 * pages_per_step]
    (lq1, lk1, lq2, lk2, sub_ref, op_ref, os_ref,
     pq_s, pm_s, pl_s, pacc_s, sq_s, sbias_s, sm_s, sl_s, sacc_s) = rest[2 * pages_per_step:]
    i = pl.program_id(2)
    step = (pl.program_id(0) * pl.num_programs(1) + pl.program_id(1)) * pl.num_programs(2) + i
    j = lax.rem(step, sample_steps)

    @pl.when(j == 0)
    def _():
        _sample_attn_init(qs_in_ref, sq_s, sbias_s, sm_s, sl_s, sacc_s, heads)

    lam = _diff_lambda(lq1, lk1, lq2, lk2, lam_init)
    _prompt_attn_step(i, qp_ref, kp_ref, vp_ref, lam, sub_ref, op_ref, pq_s, pm_s, pl_s, pacc_s,
                      lam_init)
    _sample_attn_step(j == sample_steps - 1, qs_in_ref, kn_ref, vn_ref, k_pages, v_pages, lam,
                      sub_ref, os_ref, sq_s, sbias_s, sm_s, sl_s, sacc_s, heads, lam_init)


def _attention(page_table, q_p, kb_p, vb_p, q_s, k_new, v_new, cache_k, cache_v, layer,
               lq1, lk1, lq2, lk2, sub, *, batch, seq, tq, t_new, lam_init):
    tp, bw = q_p.shape
    ts = q_s.shape[0]
    heads = bw // HEAD_DIM
    nq = seq // tq
    dec_batch, n_pages = page_table.shape
    page_rows = cache_k.shape[2]
    total_steps = batch * heads * nq
    assert total_steps % dec_batch == 0 and n_pages % (total_steps // dec_batch) == 0
    sample_steps = total_steps // dec_batch
    pages_per_step = n_pages // sample_steps
    pt_flat = page_table.reshape(-1)

    def sample_pos(b, h, i):
        step = (b * heads + h) * nq + i
        return step // sample_steps, step % sample_steps

    def page_spec(p):
        def index(b, h, i, pt):
            s, j = sample_pos(b, h, i)
            return (layer, pt[s * n_pages + j * pages_per_step + p], 0, 0)
        return pl.BlockSpec((None, None, page_rows, HEAD_DIM), index)

    seq_block = lambda shape: pl.BlockSpec(shape, lambda b, h, i, pt: (sample_pos(b, h, i)[0], 0))
    vec = lambda a: pl.BlockSpec(a.shape, lambda b, h, i, pt: (0, 0))
    qp_spec = pl.BlockSpec((tq, HEAD_DIM), lambda b, h, i, pt: (b * nq + i, h))
    kv_spec = pl.BlockSpec((seq, HEAD_DIM), lambda b, h, i, pt: (b, h))
    rows_s = heads * 2 * t_new
    grid_spec = pltpu.PrefetchScalarGridSpec(
        num_scalar_prefetch=1,
        grid=(batch, heads, nq),
        in_specs=([qp_spec, kv_spec, kv_spec,
                   seq_block((t_new, bw)), seq_block((t_new * heads, HEAD_DIM)),
                   seq_block((t_new * heads, HEAD_DIM))]
                  + [page_spec(p) for p in range(pages_per_step)] * 2
                  + [vec(lq1), vec(lk1), vec(lq2), vec(lk2), vec(sub)]),
        out_specs=[qp_spec, seq_block((t_new, bw))],
        scratch_shapes=[pltpu.VMEM((2 * tq, HEAD_DIM), BF16),
                        pltpu.VMEM((2 * tq, LANES), F32),
                        pltpu.VMEM((2 * tq, LANES), F32),
                        pltpu.VMEM((2 * tq, HEAD_DIM), F32),
                        pltpu.VMEM((rows_s, HEAD_DIM), BF16),
                        pltpu.VMEM((rows_s, pages_per_step * page_rows), F32),
                        pltpu.VMEM((rows_s, LANES), F32),
                        pltpu.VMEM((rows_s, LANES), F32),
                        pltpu.VMEM((rows_s, HEAD_DIM), F32)])
    return pl.pallas_call(
        functools.partial(_attention_kernel, pages_per_step=pages_per_step, heads=heads,
                          sample_steps=sample_steps, lam_init=lam_init),
        grid_spec=grid_spec,
        out_shape=[jax.ShapeDtypeStruct((tp, bw), BF16), jax.ShapeDtypeStruct((ts, bw), F32)],
        compiler_params=_params(("arbitrary", "arbitrary", "arbitrary")),
        name="attention",
    )(pt_flat, q_p, kb_p, vb_p, q_s, k_new, v_new,
      *([cache_k] * pages_per_step), *([cache_v] * pages_per_step), lq1, lk1, lq2, lk2, sub)


def _swiglu_residual(x1, g_ref, w_in_ref, w_out_ref, ff_chunk):
    d_ff = w_out_ref.shape[0]
    h = _rms(x1, g_ref[...]).astype(BF16)
    acc = x1
    for c0 in range(0, d_ff, ff_chunk):
        gate = jnp.dot(h, w_in_ref[:, c0:c0 + ff_chunk], preferred_element_type=F32)
        up = jnp.dot(h, w_in_ref[:, d_ff + c0:d_ff + c0 + ff_chunk], preferred_element_type=F32)
        act = (gate * jax.nn.sigmoid(gate) * up).astype(BF16)
        acc = acc + jnp.dot(act, w_out_ref[c0:c0 + ff_chunk, :], preferred_element_type=F32)
    return acc


def _mix_ffn_kernel(xp_ref, ap_ref, bp_ref, xs_ref, as_ref, bs_ref, wo_ref, g_ref, w_in_ref,
                    w_out_ref, op_ref, os_ref, slabs, *, ff_chunk, row_tokens):
    def mixed(x_ref, a_ref, b_ref):
        ab = jnp.concatenate([a_ref[...].astype(BF16), b_ref[...].astype(BF16)], axis=1)
        x1 = x_ref[...] + jnp.dot(ab, wo_ref[...], preferred_element_type=F32)
        return _swiglu_residual(x1, g_ref, w_in_ref, w_out_ref, ff_chunk)

    @pl.when(pl.program_id(0) < pl.num_programs(0) - 1)
    def _():
        tm, d = xp_ref.shape
        y = mixed(xp_ref, ap_ref, bp_ref)
        for j in range(d // LANES):
            slabs[j] = y[:, j * LANES:(j + 1) * LANES]
        for t in range(row_tokens):
            for j in range(d // LANES):
                op_ref[:, t * d + j * LANES:t * d + (j + 1) * LANES] = (
                    slabs[j, pl.ds(t, tm // row_tokens, stride=row_tokens), :])

    @pl.when(pl.program_id(0) == pl.num_programs(0) - 1)
    def _():
        os_ref[...] = mixed(xs_ref, as_ref, bs_ref)


def _ffn_final_kernel(xp_ref, xs_ref, g_ref, w_in_ref, w_out_ref, gf_ref, op_ref, os_ref, slabs,
                      *, ff_chunk, row_tokens):
    def final(x):
        return _rms(_swiglu_residual(x, g_ref, w_in_ref, w_out_ref, ff_chunk), gf_ref[...])

    @pl.when(pl.program_id(0) < pl.num_programs(0) - 1)
    def _():
        r = xp_ref.shape[0]
        d = xp_ref.shape[1] // row_tokens
        for t in range(row_tokens):
            for j in range(d // LANES):
                slabs[j, pl.ds(t, r, stride=row_tokens), :] = (
                    xp_ref[:, t * d + j * LANES:t * d + (j + 1) * LANES])
        op_ref[...] = final(jnp.concatenate([slabs[j] for j in range(d // LANES)], axis=1))

    @pl.when(pl.program_id(0) == pl.num_programs(0) - 1)
    def _():
        os_ref[...] = final(xs_ref[...])


def _prompt_tiles(n_tiles):
    return lambda i: (jnp.minimum(i, n_tiles - 1), 0)


def _whole(arr):
    return pl.BlockSpec(arr.shape, lambda i: (0,) * arr.ndim)


def _mix_ffn(xp, ap, bp, xs, a_s, b_s, wo, g, w_in, w_out, *, layer, tm, ff_chunk, row_tokens):
    t, d = xp.shape
    n_tiles = t // tm
    row = lambda arr: pl.BlockSpec((tm, arr.shape[1]), _prompt_tiles(n_tiles))
    return pl.pallas_call(
        functools.partial(_mix_ffn_kernel, ff_chunk=ff_chunk, row_tokens=row_tokens),
        grid=(n_tiles + 1,),
        in_specs=[row(xp), row(ap), row(bp), _whole(xs), _whole(a_s), _whole(b_s),
                  _const_spec(wo.shape), _const_spec(g.shape),
                  _layer_spec(w_in.shape, layer), _layer_spec(w_out.shape, layer)],
        out_specs=[pl.BlockSpec((tm // row_tokens, row_tokens * d), _prompt_tiles(n_tiles)),
                   _whole(xs)],
        out_shape=[jax.ShapeDtypeStruct((t // row_tokens, row_tokens * d), F32),
                   jax.ShapeDtypeStruct(xs.shape, F32)],
        scratch_shapes=[pltpu.VMEM((d // LANES, tm, LANES), F32)],
        compiler_params=_params(("arbitrary",)),
        name="mix_ffn",
    )(xp, ap, bp, xs, a_s, b_s, wo, g, w_in, w_out)


def _ffn_final(xp, xs, g, w_in, w_out, gf, *, layer, tm, ff_chunk, row_tokens):
    rows, width = xp.shape
    d = width // row_tokens
    t = rows * row_tokens
    n_tiles = t // tm
    return pl.pallas_call(
        functools.partial(_ffn_final_kernel, ff_chunk=ff_chunk, row_tokens=row_tokens),
        grid=(n_tiles + 1,),
        in_specs=[pl.BlockSpec((tm // row_tokens, width), _prompt_tiles(n_tiles)), _whole(xs),
                  _const_spec(g.shape), _layer_spec(w_in.shape, layer),
                  _layer_spec(w_out.shape, layer), _const_spec(gf.shape)],
        out_specs=[pl.BlockSpec((tm, d), _prompt_tiles(n_tiles)), _whole(xs)],
        out_shape=[jax.ShapeDtypeStruct((t, d), F32), jax.ShapeDtypeStruct(xs.shape, F32)],
        scratch_shapes=[pltpu.VMEM((d // LANES, tm, LANES), F32)],
        compiler_params=_params(("arbitrary",)),
        name="ffn_final",
    )(xp, xs, g, w_in, w_out, gf)


def _expand_groups(compact, r1, c1):
    n_slabs, rows, k = compact.shape
    width = (k // c1) * SLAB_GROUPS * c1
    src = lax.broadcasted_iota(jnp.int32, (k, width), 0)
    dst = lax.broadcasted_iota(jnp.int32, (k, width), 1)
    select = (src == (dst // (SLAB_GROUPS * c1)) * c1 + dst % c1).astype(F32)
    full = jnp.dot(compact.reshape(n_slabs * rows, k), select,
                   precision=lax.Precision.HIGHEST).reshape(n_slabs, rows, width)
    row_g = (lax.broadcasted_iota(jnp.int32, full.shape, 1) // r1) % SLAB_GROUPS
    col_h = (lax.broadcasted_iota(jnp.int32, full.shape, 2) // c1) % SLAB_GROUPS
    return jnp.where(row_g == col_h, full, 0.0).astype(BF16)


def _s5_rows_kernel(x_ref, g_ref, s0re_ref, s0im_ref, coef_ref, wb_ref, wy_ref, wc_ref, d_ref,
                    wglu_ref, bglu_ref, o_ref, sre_ref, sim_ref,
                    re_s, im_s, gel_s, car_re, car_im, *, row_tokens, n_seq):
    r, width = x_ref.shape
    d = width // row_tokens
    n_slabs = d // LANES
    slab_state = re_s.shape[1] // n_slabs

    xs = [x_ref[:, t * d:(t + 1) * d] for t in range(row_tokens)]
    hs = [_rms(x, g_ref[...]) for x in xs]
    hb = [h.astype(BF16) for h in hs]

    def slab_lhs(j):
        return jnp.concatenate([h[:, j * LANES:(j + 1) * LANES] for h in hb], axis=1)

    def state_cols(j):
        return slice(j * slab_state, (j + 1) * slab_state)

    def window(j):
        w = jnp.dot(slab_lhs(j), wb_ref[j], preferred_element_type=F32)
        re_s[:, state_cols(j)] = w[:, :slab_state]
        im_s[:, state_cols(j)] = w[:, slab_state:]

    n_tiles = r // SUBLANES

    def scan_rows(j, lo, hi):
        st = state_cols(j)
        cr, ci = car_re[:, st], car_im[:, st]
        first = lax.broadcasted_iota(jnp.int32, (SUBLANES, slab_state), 0) == 0
        for i in range(lo, hi):
            rows = slice(i * SUBLANES, (i + 1) * SUBLANES)
            wr, wi = re_s[rows, st], im_s[rows, st]
            for k, shift in enumerate(SCAN_SHIFTS):
                ar, ai = coef_ref[2 * k, :, st], coef_ref[2 * k + 1, :, st]
                sr = pltpu.roll(wr, shift, 0)
                si = pltpu.roll(wi, shift, 0)
                wr, wi = wr + ar * sr - ai * si, wi + ar * si + ai * sr
            pr = coef_ref[2 * len(SCAN_SHIFTS), :, st]
            pi = coef_ref[2 * len(SCAN_SHIFTS) + 1, :, st]
            wr, wi = wr + pr * cr - pi * ci, wi + pr * ci + pi * cr
            re_s[rows, st] = jnp.where(first, cr, pltpu.roll(wr, 1, 0))
            im_s[rows, st] = jnp.where(first, ci, pltpu.roll(wi, 1, 0))
            cr, ci = wr[SUBLANES - 1:SUBLANES, :], wi[SUBLANES - 1:SUBLANES, :]
        car_re[:, st] = cr
        car_im[:, st] = ci
        if hi == n_tiles:
            sre_ref[:, st] = cr
            sim_ref[:, st] = ci

    def slab_outputs(j):
        cols = slice(j * LANES, (j + 1) * LANES)
        st = state_cols(j)
        s_cat = jnp.concatenate([re_s[:, st], im_s[:, st]], axis=1).astype(BF16)
        y = (jnp.dot(slab_lhs(j), wy_ref[j], preferred_element_type=F32)
             + jnp.dot(s_cat, wc_ref[j], preferred_element_type=F32))
        for t in range(row_tokens):
            yt = y[:, t * LANES:(t + 1) * LANES] + d_ref[:, cols] * hs[t][:, cols]
            gel_s[t * r:(t + 1) * r, cols] = jax.nn.gelu(yt).astype(BF16)

    if n_seq == 1:
        @pl.when(pl.program_id(1) == 0)
        def _():
            car_re[...] = s0re_ref[...]
            car_im[...] = s0im_ref[...]

        window(0)
        for j in range(n_slabs):
            if j + 1 < n_slabs:
                window(j + 1)
            if j >= 1:
                slab_outputs(j - 1)
            scan_rows(j, 0, n_tiles)
        slab_outputs(n_slabs - 1)
    else:
        for j in range(n_slabs):
            window(j)
        a_re = coef_ref[0][1:2, :]
        a_im = coef_ref[1][1:2, :]
        cr, ci = s0re_ref[...], s0im_ref[...]
        for n in range(r // n_seq):
            rows = slice(n * n_seq, (n + 1) * n_seq)
            wr, wi = re_s[rows, :], im_s[rows, :]
            re_s[rows, :] = cr
            im_s[rows, :] = ci
            cr, ci = a_re * cr - a_im * ci + wr, a_re * ci + a_im * cr + wi
        sre_ref[...] = cr
        sim_ref[...] = ci
        for j in range(n_slabs):
            slab_outputs(j)
    z = jnp.dot(gel_s[...], wglu_ref[...], preferred_element_type=F32) + bglu_ref[...]
    for t in range(row_tokens):
        zt = z[t * r:(t + 1) * r, :]
        o_ref[:, t * d:(t + 1) * d] = xs[t] + zt[:, :d] * jax.nn.sigmoid(zt[:, d:])


def _s5_rows_mix(x, g, s0re, s0im, coef, wb, wy, wc, dskip, wglu, bglu, *, batch, rows_per_seq,
                 r, row_tokens, step_major, name):
    rows, width = x.shape
    d = width // row_tokens
    n_state = s0re.shape[-1]
    if step_major:
        grid = (1, 1)
        r = rows
        state_spec = pl.BlockSpec((batch, n_state), lambda b, i: (0, 0))
        x_spec = pl.BlockSpec((rows, width), lambda b, i: (0, 0))
    else:
        nt = rows_per_seq // r
        grid = (batch, nt)
        state_spec = pl.BlockSpec((None, 1, n_state), lambda b, i: (b, 0, 0))
        x_spec = pl.BlockSpec((r, width), lambda b, i: (b * nt + i, 0))
    consts = (g, coef, wb, wy, wc, dskip, wglu, bglu)
    x1, sre, sim = pl.pallas_call(
        functools.partial(_s5_rows_kernel, row_tokens=row_tokens,
                          n_seq=batch if step_major else 1),
        grid=grid,
        in_specs=[x_spec, _const_spec(g.shape), state_spec, state_spec]
                 + [_const_spec(c.shape) for c in consts[1:]],
        out_specs=[x_spec, state_spec, state_spec],
        out_shape=[jax.ShapeDtypeStruct((rows, width), F32),
                   jax.ShapeDtypeStruct(s0re.shape, F32),
                   jax.ShapeDtypeStruct(s0im.shape, F32)],
        scratch_shapes=[pltpu.VMEM((r, n_state), F32), pltpu.VMEM((r, n_state), F32),
                        pltpu.VMEM((r * row_tokens, d), BF16),
                        pltpu.VMEM((1, n_state), F32), pltpu.VMEM((1, n_state), F32)],
        compiler_params=_params(("parallel", "arbitrary")),
        name=name,
    )(x, g, s0re, s0im, *consts[1:])
    return x1, sre, sim


def _s5_row_tables(a_re, a_im, log_dt, b_re, b_im, c_re, c_im, row_tokens):
    groups, n_p = a_re.shape
    p_tok = row_tokens
    n_slabs = groups // SLAB_GROUPS
    dt = jnp.exp(log_dt)[:, None]
    ks = jnp.arange(p_tok * SUBLANES + 1, dtype=F32)[:, None, None]
    mag = jnp.exp((a_re * dt)[None] * ks)
    ang = (a_im * dt)[None] * ks
    pw_re, pw_im = mag * jnp.cos(ang), mag * jnp.sin(ang)
    den = a_re * a_re + a_im * a_im
    q_re = ((pw_re[1] - 1.0) * a_re + pw_im[1] * a_im) / den
    q_im = (pw_im[1] * a_re - (pw_re[1] - 1.0) * a_im) / den
    bb_re = q_re[:, :, None] * b_re - q_im[:, :, None] * b_im
    bb_im = q_re[:, :, None] * b_im + q_im[:, :, None] * b_re

    rows = jnp.arange(SUBLANES)
    coef = []
    for shift in SCAN_SHIFTS:
        keep = (rows >= shift)[:, None].astype(F32)
        coef += [pw_re[p_tok * shift].reshape(1, -1) * keep,
                 pw_im[p_tok * shift].reshape(1, -1) * keep]
    pick = lambda pw, idx: jnp.stack([pw[k] for k in idx])
    carry = [p_tok * (r + 1) for r in range(SUBLANES)]
    coef += [pick(pw_re, carry).reshape(SUBLANES, -1), pick(pw_im, carry).reshape(SUBLANES, -1)]
    coef = jnp.stack(coef)

    back = [p_tok - 1 - s for s in range(p_tok)]
    er, ei = pick(pw_re, back)[:, :, :, None], pick(pw_im, back)[:, :, :, None]
    lay_b = lambda m: m.reshape(p_tok, n_slabs, SLAB_GROUPS, n_p, SSM_GROUP).transpose(
        1, 0, 2, 4, 3).reshape(n_slabs, -1, n_p)
    wb = _expand_groups(jnp.concatenate([lay_b(er * bb_re - ei * bb_im),
                                         lay_b(er * bb_im + ei * bb_re)], axis=2), SSM_GROUP, n_p)

    kr, ki = pw_re[:p_tok + 1][:, :, None, :], pw_im[:p_tok + 1][:, :, None, :]
    ck_re, ck_im = c_re * kr - c_im * ki, c_re * ki + c_im * kr

    lay_c = lambda m: m.reshape(p_tok, n_slabs, SLAB_GROUPS, SSM_GROUP, n_p).transpose(
        1, 2, 4, 0, 3).reshape(n_slabs, SLAB_GROUPS * n_p, -1)
    wc = _expand_groups(jnp.concatenate([lay_c(ck_re[1:]), -lay_c(ck_im[1:])], axis=1),
                        n_p, SSM_GROUP)

    bt_re, bt_im = bb_re.transpose(0, 2, 1)[None, :, :, None, :], bb_im.transpose(0, 2, 1)[
        None, :, :, None, :]
    lag = jnp.sum(ck_re[:p_tok, :, None] * bt_re - ck_im[:p_tok, :, None] * bt_im, axis=-1)
    lag = lag.reshape(p_tok, n_slabs, SLAB_GROUPS * SSM_GROUP, SSM_GROUP)
    zero = jnp.zeros_like(lag[0])
    toep = jnp.concatenate(
        [jnp.concatenate([lag[t - s] if t >= s else zero for t in range(p_tok)], axis=2)
         for s in range(p_tok)], axis=1)
    wy = _expand_groups(toep, SSM_GROUP, SSM_GROUP)
    return coef, wb, wy, wc


def kernel(x_prompt, x_sample, cache_k, cache_v, page_table, state_ssm_re, state_ssm_im,
           norm_mix, norm_ffn, norm_final, w_in_even, w_out_even, sgu_norm, sgu_w, sgu_b,
           lambda_q1, lambda_k1, lambda_q2, lambda_k2, attn_subln,
           ssm_a_re, ssm_a_im, ssm_log_dt, ssm_b_re, ssm_b_im, ssm_c_re, ssm_c_im, ssm_d,
           w_glu, b_glu, w_ffn_in, w_ffn_out):
    batch, seq, d = x_prompt.shape
    dec_batch, t_new, _ = x_sample.shape
    depth = norm_mix.shape[0]
    assert depth == 2 and seq % CHUNK == 0 and t_new == SUBLANES
    tp = batch * seq
    ts = dec_batch * t_new
    xp = x_prompt.reshape(tp, d)
    xs = x_sample.reshape(ts, d)
    tm_prompt = 512
    ff_chunk = 256
    row = lambda v: v.reshape(1, -1)

    lam_init = 0.8 - 0.6 * math.exp(-0.3 * 0)
    w_in0 = w_in_even[0].astype(BF16)
    a_width = sgu_norm.shape[1]
    gw = a_width // A_GROUPS
    tril = jnp.tril(jnp.ones((CHUNK, CHUNK), bool))
    mix_p = jnp.where(tril, sgu_w[0], 0).astype(BF16)
    bias_p = jnp.repeat(sgu_b[0].T, gw, axis=1)
    small = jnp.where(tril[:t_new, :t_new], sgu_w[0][:, :t_new, :t_new], 0)
    row_i = lax.broadcasted_iota(jnp.int32, (ts, ts), 0)
    col_i = lax.broadcasted_iota(jnp.int32, (ts, ts), 1)
    rep = (row_i[:, :t_new] % t_new == col_i[:, :t_new]).astype(F32)
    tiled = jnp.einsum('at,gts,bs->gab', rep, small, rep, precision=lax.Precision.HIGHEST)
    mix_s = jnp.where(row_i // t_new == col_i // t_new, tiled, 0.0).astype(BF16)
    bias_s = jnp.tile(bias_p[:t_new], (dec_batch, 1))
    lam_vecs = [row(lambda_q1[0]), row(lambda_k1[0]), row(lambda_q2[0]), row(lambda_k2[0]),
                row(attn_subln[0])]

    later_weights = (w_ffn_in.reshape(-1, w_ffn_in.shape[-1]),
                     w_ffn_out.reshape(-1, w_ffn_out.shape[-1]), w_glu[0], w_out_even[0])
    (a_p, q_p, k_p, vb_p, kb_p, vbb_p), (w_ffn_in_b, w_ffn_out_b, w_glu0, w_out0) = _even_in(
        xp, row(norm_mix[0]), w_in0, row(sgu_norm[0]), mix_p, bias_p, tm=tm_prompt, prompt=True,
        cast_weights=later_weights)
    w_ffn_in_b = w_ffn_in_b.reshape(w_ffn_in.shape)
    w_ffn_out_b = w_ffn_out_b.reshape(w_ffn_out.shape)
    (a_s, q_s, k_s, vb_s, vn_s), _ = _even_in(
        xs, row(norm_mix[0]), w_in0, row(sgu_norm[0]), mix_s, bias_s, tm=ts, prompt=False)
    n_layers, n_phys, page_size, heads, _ = cache_k.shape
    page_view = (n_layers, n_phys, page_size * heads, HEAD_DIM)
    b_p, b_s = _attention(page_table, q_p, kb_p, vbb_p, q_s, k_s, vb_s,
                          cache_k.reshape(page_view), cache_v.reshape(page_view), 0, *lam_vecs,
                          batch=batch, seq=seq, tq=512, t_new=t_new, lam_init=lam_init)
    xp, xs = _mix_ffn(xp, a_p, b_p, xs, a_s, b_s, w_out0, row(norm_ffn[0]), w_ffn_in_b,
                      w_ffn_out_b, layer=0, tm=tm_prompt, ff_chunk=ff_chunk,
                      row_tokens=ROW_TOKENS)

    groups, n_p = ssm_a_re.shape[1:]
    n_state = groups * n_p
    ssm = (ssm_a_re[0], ssm_a_im[0], ssm_log_dt[0], ssm_b_re[0], ssm_b_im[0], ssm_c_re[0],
           ssm_c_im[0])
    s5_args = (*_s5_row_tables(*ssm, ROW_TOKENS), row(ssm_d[0]), w_glu0, row(b_glu[0]))
    zeros = jnp.zeros((batch, 1, n_state), F32)
    xp, p_re, p_im = _s5_rows_mix(xp, row(norm_mix[1]), zeros, zeros, *s5_args, batch=batch,
                                  rows_per_seq=seq // ROW_TOKENS, r=S5_BLOCK_ROWS,
                                  row_tokens=ROW_TOKENS, step_major=False, name="s5_prompt")
    steps = t_new // ROW_TOKENS
    pack = lambda v: v.reshape(dec_batch, steps, -1).transpose(1, 0, 2).reshape(
        steps * dec_batch, -1)
    unpack = lambda v: v.reshape(steps, dec_batch, -1).transpose(1, 0, 2).reshape(ts, d)
    xs, s_re, s_im = _s5_rows_mix(pack(xs), row(norm_mix[1]),
                                  state_ssm_re[0].reshape(dec_batch, n_state),
                                  state_ssm_im[0].reshape(dec_batch, n_state), *s5_args,
                                  batch=dec_batch, rows_per_seq=steps, r=None,
                                  row_tokens=ROW_TOKENS, step_major=True, name="s5_sample")
    xs = unpack(xs)
    yp, ys = _ffn_final(xp, xs, row(norm_ffn[1]), w_ffn_in_b, w_ffn_out_b, row(norm_final),
                        layer=1, tm=tm_prompt, ff_chunk=ff_chunk, row_tokens=ROW_TOKENS)

    st = lambda s, n: s.reshape(1, n, groups, n_p)
    return (yp.reshape(batch, seq, d), ys.reshape(dec_batch, t_new, d),
            k_p.reshape(1, batch, seq, heads, HEAD_DIM), vb_p.reshape(1, batch, seq, heads, HEAD_DIM),
            k_s.reshape(1, dec_batch, t_new, heads, HEAD_DIM),
            vb_s.reshape(1, dec_batch, t_new, heads, HEAD_DIM),
            vn_s.reshape(1, dec_batch, t_new, a_width),
            st(p_re, batch), st(p_im, batch), st(s_re, dec_batch), st(s_im, dec_batch))
```

```python
import functools
import math

import jax
import jax.numpy as jnp
from jax import lax
from jax.experimental import pallas as pl
from jax.experimental.pallas import tpu as pltpu

F32 = jnp.float32
BF16 = jnp.bfloat16

EPS = 1e-6
LANES = 128
SUBLANES = 8
V7X_VMEM_BYTES = 64 * 1024 * 1024
VMEM_LIMIT = V7X_VMEM_BYTES * 7 // 8
MASK_VALUE = -0.7 * float(jnp.finfo(jnp.float32).max)

CHUNK = 128
A_GROUPS = 4
HEAD_DIM = 128
QK_HALF = 64
LOG2E = math.log2(math.e)
SSM_GROUP = 16
SLAB_GROUPS = LANES // SSM_GROUP
SCAN_SHIFTS = (1, 2, 4)
ROW_TOKENS = 4
TOKEN_TILE = 512
S5_BLOCK_ROWS = TOKEN_TILE // ROW_TOKENS
ATTN_Q_TILE = 512
FF_CHUNK = 256

def _const_spec(shape):
    zeros = (0,) * len(shape)
    return pl.BlockSpec(shape, lambda *_: zeros, pipeline_mode=pl.Buffered(1))


def _layer_spec(shape, layer):
    index = (layer,) + (0,) * (len(shape) - 1)
    return pl.BlockSpec((None,) + tuple(shape[1:]), lambda *_: index, pipeline_mode=pl.Buffered(1))


def _rms(x, g):
    return x * lax.rsqrt(jnp.mean(x * x, axis=-1, keepdims=True) + EPS) * g


def _params(semantics):
    return pltpu.CompilerParams(dimension_semantics=semantics, vmem_limit_bytes=VMEM_LIMIT)


def _even_in_kernel(x_ref, g_ref, w_ref, sn_ref, mix_ref, bias_ref, *rest, prompt, a_width,
                    n_cast):
    cast_in, outs, cast_out = rest[:n_cast], rest[n_cast:len(rest) - n_cast], rest[len(rest) - n_cast:]
    for src, dst in zip(cast_in, cast_out):
        dst[...] = src[...].astype(BF16)
    if prompt:
        a_ref, q_ref, k_ref, vb_ref, kb_ref, vbb_ref = outs
    else:
        a_ref, q_ref, k_ref, vb_ref, vn_ref = outs
    tm = x_ref.shape[0]
    rows_per_mix = mix_ref.shape[1]
    gw = a_width // A_GROUPS
    h = _rms(x_ref[...], g_ref[...]).astype(BF16)

    def proj(c0, width):
        return jnp.dot(h, w_ref[:, c0:c0 + width], preferred_element_type=F32)

    u = jax.nn.gelu(proj(0, a_width))
    v = _rms(jax.nn.gelu(proj(a_width, a_width)), sn_ref[...])
    if not prompt:
        vn_ref[...] = v
    v16 = v.astype(BF16)
    o = 2 * a_width
    bw = q_ref.shape[1]
    q_ref[...] = (proj(o, bw) * (QK_HALF ** -0.5 * LOG2E)).astype(q_ref.dtype)
    k = proj(o + bw, bw)
    vb = proj(o + 2 * bw, bw)
    heads = bw // HEAD_DIM
    for hd in range(heads):
        k_ref[pl.ds(hd, tm, stride=heads), :] = k[:, hd * HEAD_DIM:(hd + 1) * HEAD_DIM]
        vb_ref[pl.ds(hd, tm, stride=heads), :] = vb[:, hd * HEAD_DIM:(hd + 1) * HEAD_DIM]
    if prompt:
        kb_ref[...] = k.astype(BF16)
        vbb_ref[...] = vb.astype(BF16)
    for c in range(tm // rows_per_mix):
        r0 = c * rows_per_mix
        for g in range(A_GROUPS):
            c0 = g * gw
            gate = jnp.dot(mix_ref[g], v16[r0:r0 + rows_per_mix, c0:c0 + gw],
                           preferred_element_type=F32) + bias_ref[:, c0:c0 + gw]
            a_ref[r0:r0 + rows_per_mix, c0:c0 + gw] = (
                u[r0:r0 + rows_per_mix, c0:c0 + gw] * gate).astype(a_ref.dtype)


def _even_in(x, g, w, sn, mix, bias, *, tm, prompt, cast_weights=()):
    t, d = x.shape
    steps = t // tm
    cast_specs = [pl.BlockSpec((c.shape[0] // steps, c.shape[1]), lambda i: (i, 0))
                  for c in cast_weights]
    a_width = sn.shape[1]
    bw = (w.shape[1] - 2 * a_width) // 3
    act = BF16 if prompt else F32
    row = lambda width: pl.BlockSpec((tm, width), lambda i: (i, 0))
    heads = bw // HEAD_DIM
    head_rows = pl.BlockSpec((tm * heads, HEAD_DIM), lambda i: (i, 0))
    out_shape = [jax.ShapeDtypeStruct((t, a_width), act), jax.ShapeDtypeStruct((t, bw), act),
                 jax.ShapeDtypeStruct((t * heads, HEAD_DIM), F32),
                 jax.ShapeDtypeStruct((t * heads, HEAD_DIM), F32)]
    out_specs = [row(a_width), row(bw), head_rows, head_rows]
    if prompt:
        out_shape += [jax.ShapeDtypeStruct((t, bw), BF16)] * 2
        out_specs += [row(bw), row(bw)]
    else:
        out_shape += [jax.ShapeDtypeStruct((t, a_width), F32)]
        out_specs += [row(a_width)]
    n_out = len(out_shape)
    outs = pl.pallas_call(
        functools.partial(_even_in_kernel, prompt=prompt, a_width=a_width,
                          n_cast=len(cast_weights)),
        grid=(steps,),
        in_specs=[row(d), _const_spec(g.shape), _const_spec(w.shape), _const_spec(sn.shape),
                  _const_spec(mix.shape), _const_spec(bias.shape)] + cast_specs,
        out_specs=out_specs + cast_specs,
        out_shape=out_shape + [jax.ShapeDtypeStruct(c.shape, BF16) for c in cast_weights],
        compiler_params=_params(("parallel",)),
        name="even_in_prompt" if prompt else "even_in_sample",
    )(x, g, w, sn, mix, bias, *cast_weights)
    return outs[:n_out], outs[n_out:]


def _split_halves(q):
    lane = lax.broadcasted_iota(jnp.int32, q.shape, 1)
    zero = jnp.zeros_like(q)
    return jnp.where(lane < QK_HALF, q, zero), jnp.where(lane >= QK_HALF, q, zero)


def _softmax_step(s, vblk, m_ref, l_ref, acc_ref, rows):
    tiles = [s[:, t * LANES:(t + 1) * LANES] for t in range(s.shape[1] // LANES)]
    m_prev = m_ref[rows, :]
    m_next = jnp.maximum(
        m_prev, jnp.max(functools.reduce(jnp.maximum, tiles), axis=1, keepdims=True))
    p_tiles = [jnp.exp2(t - m_next) for t in tiles]
    alpha = jnp.exp2(m_prev - m_next)
    l_ref[rows, :] = alpha * l_ref[rows, :] + functools.reduce(jnp.add, p_tiles)
    p = jnp.concatenate([t.astype(BF16) for t in p_tiles], axis=1)
    acc_ref[rows, :] = alpha * acc_ref[rows, :] + jnp.dot(p, vblk, preferred_element_type=F32)
    m_ref[rows, :] = m_next


def _row_sum(l):
    return jnp.sum(l, axis=1, keepdims=True)


def _diff_lambda(lq1, lk1, lq2, lk2, lam_init):
    e1 = jnp.exp(jnp.sum(lq1[...] * lk1[...], axis=1, keepdims=True))
    e2 = jnp.exp(jnp.sum(lq2[...] * lk2[...], axis=1, keepdims=True))
    return e1 - e2 + lam_init


def _head_out(o1, o2, lam, sub, lam_init):
    o = o1 - lam * o2
    return _rms(o, sub) * (1.0 - lam_init)


def _causal_bias(tq):
    row = lax.broadcasted_iota(jnp.int32, (2 * tq, tq), 0)
    col = lax.broadcasted_iota(jnp.int32, (2 * tq, tq), 1)
    qpos = jnp.where(row >= tq, row - tq, row)
    return jnp.where(qpos >= col, 0.0, MASK_VALUE)


def _prompt_attn_step(i, q_ref, k_ref, v_ref, lam, sub_ref, o_ref, qs_ref, causal_ref, m_ref,
                      l_ref, acc_ref, lam_init):
    tq = q_ref.shape[0]
    q1, q2 = _split_halves(q_ref[...])
    qs_ref[0:tq, :] = q1
    qs_ref[tq:2 * tq, :] = q2
    m_ref[...] = jnp.full(m_ref.shape, MASK_VALUE, F32)
    l_ref[...] = jnp.zeros(l_ref.shape, F32)
    acc_ref[...] = jnp.zeros(acc_ref.shape, F32)
    all_rows = slice(0, 2 * tq)

    def scores(j):
        kv_rows = pl.ds(pl.multiple_of(j * tq, tq), tq)
        s = lax.dot_general(qs_ref[...], k_ref[kv_rows, :], (((1,), (1,)), ((), ())),
                            preferred_element_type=F32)
        return s, v_ref[kv_rows, :]

    def full_block(j):
        s, vblk = scores(j)
        _softmax_step(s, vblk, m_ref, l_ref, acc_ref, all_rows)

    def full_quad(jj, carry):
        for u in range(4):
            full_block(4 * jj + u)
        return carry

    lax.fori_loop(0, i // 4, full_quad, 0)

    @pl.when(i % 4 >= 2)
    def _():
        full_block(4 * (i // 4))
        full_block(4 * (i // 4) + 1)

    @pl.when(i % 2 == 1)
    def _():
        full_block(i - 1)
    s, vblk = scores(i)
    _softmax_step(s + causal_ref[...], vblk, m_ref, l_ref, acc_ref, all_rows)

    o1 = acc_ref[0:tq, :] / _row_sum(l_ref[0:tq, :])
    o2 = acc_ref[tq:2 * tq, :] / _row_sum(l_ref[tq:2 * tq, :])
    o_ref[...] = _head_out(o1, o2, lam, sub_ref[...], lam_init).astype(o_ref.dtype)


def _head_match(shape, heads, rows_per_head):
    row = lax.broadcasted_iota(jnp.int32, shape, 0)
    col = lax.broadcasted_iota(jnp.int32, shape, 1)
    return row, col, lax.rem(col, heads) == row // rows_per_head


def _sample_attn_init(q_ref, qs_ref, m_ref, l_ref, acc_ref, heads):
    pieces = []
    for h in range(heads):
        pieces += _split_halves(q_ref[:, h * HEAD_DIM:(h + 1) * HEAD_DIM])
    qs_ref[...] = jnp.concatenate(pieces, axis=0).astype(BF16)
    m_ref[...] = jnp.full(m_ref.shape, MASK_VALUE, F32)
    l_ref[...] = jnp.zeros(l_ref.shape, F32)
    acc_ref[...] = jnp.zeros(acc_ref.shape, F32)


def _sample_attn_step(is_last, q_ref, kn_ref, vn_ref, k_pages, v_pages, lam, sub_ref, o_ref,
                      qs_ref, bias_ref, m_ref, l_ref, acc_ref, heads, lam_init):
    pages_per_step = len(k_pages)
    t_new = q_ref.shape[0]
    rows_per_head = 2 * t_new
    all_rows = slice(0, heads * rows_per_head)

    def scores(kblk):
        return lax.dot_general(qs_ref[...], kblk.astype(BF16), (((1,), (1,)), ((), ())),
                               preferred_element_type=F32)

    kblk = jnp.concatenate([k_pages[p][...] for p in range(pages_per_step)], axis=0)
    vblk = jnp.concatenate([v_pages[p][...] for p in range(pages_per_step)], axis=0)
    _softmax_step(scores(kblk) + bias_ref[...], vblk.astype(BF16), m_ref, l_ref, acc_ref, all_rows)

    @pl.when(is_last)
    def _():
        pad = jnp.zeros((LANES - kn_ref.shape[0], HEAD_DIM), F32)
        s = scores(jnp.concatenate([kn_ref[...], pad], axis=0))
        row, col, ok = _head_match(s.shape, heads, rows_per_head)
        s = jnp.where(ok, s, MASK_VALUE)
        s = jnp.where(col // heads <= lax.rem(row, t_new), s, MASK_VALUE)
        vblk = jnp.concatenate([vn_ref[...], pad], axis=0)
        _softmax_step(s, vblk.astype(BF16), m_ref, l_ref, acc_ref, all_rows)
        for h in range(heads):
            r0 = h * rows_per_head
            o1 = acc_ref[r0:r0 + t_new, :] / _row_sum(l_ref[r0:r0 + t_new, :])
            o2 = (acc_ref[r0 + t_new:r0 + rows_per_head, :]
                  / _row_sum(l_ref[r0 + t_new:r0 + rows_per_head, :]))
            o_ref[:, h * HEAD_DIM:(h + 1) * HEAD_DIM] = _head_out(
                o1, o2, lam, sub_ref[...], lam_init)


def _attention_kernel(pt_ref, qp_ref, kp_ref, vp_ref, qs_in_ref, kn_ref, vn_ref, *rest,
                      pages_per_step, heads, sample_steps, lam_init):
    del pt_ref
    k_pages = rest[:pages_per_step]
    v_pages = rest[pages_per_step:2 * pages_per_step]
    (lq1, lk1, lq2, lk2, sub_ref, op_ref, os_ref,
     pq_s, pcausal_s, pm_s, pl_s, pacc_s,
     sq_s, sbias_s, sm_s, sl_s, sacc_s) = rest[2 * pages_per_step:]
    i = pl.program_id(2)
    step = (pl.program_id(0) * pl.num_programs(1) + pl.program_id(1)) * pl.num_programs(2) + i
    j = lax.rem(step, sample_steps)

    @pl.when(step == 0)
    def _():
        pcausal_s[...] = _causal_bias(qp_ref.shape[0])
        _, _, ok = _head_match(sbias_s.shape, heads, 2 * qs_in_ref.shape[0])
        sbias_s[...] = jnp.where(ok, 0.0, MASK_VALUE)

    @pl.when(j == 0)
    def _():
        _sample_attn_init(qs_in_ref, sq_s, sm_s, sl_s, sacc_s, heads)

    lam = _diff_lambda(lq1, lk1, lq2, lk2, lam_init)
    _prompt_attn_step(i, qp_ref, kp_ref, vp_ref, lam, sub_ref, op_ref, pq_s, pcausal_s, pm_s,
                      pl_s, pacc_s, lam_init)
    _sample_attn_step(j == sample_steps - 1, qs_in_ref, kn_ref, vn_ref, k_pages, v_pages, lam,
                      sub_ref, os_ref, sq_s, sbias_s, sm_s, sl_s, sacc_s, heads, lam_init)


def _attention(page_table, q_p, kb_p, vb_p, q_s, k_new, v_new, cache_k, cache_v, layer,
               lq1, lk1, lq2, lk2, sub, *, batch, seq, tq, t_new, lam_init):
    tp, bw = q_p.shape
    ts = q_s.shape[0]
    heads = bw // HEAD_DIM
    nq = seq // tq
    dec_batch, n_pages = page_table.shape
    page_rows = cache_k.shape[2]
    total_steps = batch * heads * nq
    assert total_steps % dec_batch == 0 and n_pages % (total_steps // dec_batch) == 0
    sample_steps = total_steps // dec_batch
    pages_per_step = n_pages // sample_steps
    pt_flat = page_table.reshape(-1)

    def sample_pos(b, h, i):
        step = (b * heads + h) * nq + i
        return step // sample_steps, step % sample_steps

    def page_spec(p):
        def index(b, h, i, pt):
            s, j = sample_pos(b, h, i)
            return (layer, pt[s * n_pages + j * pages_per_step + p], 0, 0)
        return pl.BlockSpec((None, None, page_rows, HEAD_DIM), index)

    seq_block = lambda shape: pl.BlockSpec(shape, lambda b, h, i, pt: (sample_pos(b, h, i)[0], 0))
    vec = lambda a: pl.BlockSpec(a.shape, lambda b, h, i, pt: (0, 0))
    qp_spec = pl.BlockSpec((tq, HEAD_DIM), lambda b, h, i, pt: (b * nq + i, h))
    kv_spec = pl.BlockSpec((seq, HEAD_DIM), lambda b, h, i, pt: (b, h))
    rows_s = heads * 2 * t_new
    grid_spec = pltpu.PrefetchScalarGridSpec(
        num_scalar_prefetch=1,
        grid=(batch, heads, nq),
        in_specs=([qp_spec, kv_spec, kv_spec,
                   seq_block((t_new, bw)), seq_block((t_new * heads, HEAD_DIM)),
                   seq_block((t_new * heads, HEAD_DIM))]
                  + [page_spec(p) for p in range(pages_per_step)] * 2
                  + [vec(lq1), vec(lk1), vec(lq2), vec(lk2), vec(sub)]),
        out_specs=[qp_spec, seq_block((t_new, bw))],
        scratch_shapes=[pltpu.VMEM((2 * tq, HEAD_DIM), BF16),
                        pltpu.VMEM((2 * tq, tq), F32),
                        pltpu.VMEM((2 * tq, LANES), F32),
                        pltpu.VMEM((2 * tq, LANES), F32),
                        pltpu.VMEM((2 * tq, HEAD_DIM), F32),
                        pltpu.VMEM((rows_s, HEAD_DIM), BF16),
                        pltpu.VMEM((rows_s, pages_per_step * page_rows), F32),
                        pltpu.VMEM((rows_s, LANES), F32),
                        pltpu.VMEM((rows_s, LANES), F32),
                        pltpu.VMEM((rows_s, HEAD_DIM), F32)])
    return pl.pallas_call(
        functools.partial(_attention_kernel, pages_per_step=pages_per_step, heads=heads,
                          sample_steps=sample_steps, lam_init=lam_init),
        grid_spec=grid_spec,
        out_shape=[jax.ShapeDtypeStruct((tp, bw), BF16), jax.ShapeDtypeStruct((ts, bw), F32)],
        compiler_params=_params(("arbitrary", "arbitrary", "arbitrary")),
        name="attention",
    )(pt_flat, q_p, kb_p, vb_p, q_s, k_new, v_new,
      *([cache_k] * pages_per_step), *([cache_v] * pages_per_step), lq1, lk1, lq2, lk2, sub)


def _swiglu_residual(x1, g_ref, w_in_ref, w_out_ref, ff_chunk):
    d_ff = w_out_ref.shape[0]
    h = _rms(x1, g_ref[...]).astype(BF16)
    acc = x1
    for c0 in range(0, d_ff, ff_chunk):
        gate = jnp.dot(h, w_in_ref[:, c0:c0 + ff_chunk], preferred_element_type=F32)
        up = jnp.dot(h, w_in_ref[:, d_ff + c0:d_ff + c0 + ff_chunk], preferred_element_type=F32)
        act = (gate * jax.nn.sigmoid(gate) * up).astype(BF16)
        acc = acc + jnp.dot(act, w_out_ref[c0:c0 + ff_chunk, :], preferred_element_type=F32)
    return acc


def _mix_ffn_kernel(xp_ref, ap_ref, bp_ref, xs_ref, as_ref, bs_ref, wo_ref, g_ref, w_in_ref,
                    w_out_ref, op_ref, os_ref, slabs, *, ff_chunk, row_tokens):
    def mixed(x_ref, a_ref, b_ref):
        ab = jnp.concatenate([a_ref[...].astype(BF16), b_ref[...].astype(BF16)], axis=1)
        x1 = x_ref[...] + jnp.dot(ab, wo_ref[...], preferred_element_type=F32)
        return _swiglu_residual(x1, g_ref, w_in_ref, w_out_ref, ff_chunk)

    @pl.when(pl.program_id(0) < pl.num_programs(0) - 1)
    def _():
        tm, d = xp_ref.shape
        y = mixed(xp_ref, ap_ref, bp_ref)
        for j in range(d // LANES):
            slabs[j] = y[:, j * LANES:(j + 1) * LANES]
        for t in range(row_tokens):
            for j in range(d // LANES):
                op_ref[:, t * d + j * LANES:t * d + (j + 1) * LANES] = (
                    slabs[j, pl.ds(t, tm // row_tokens, stride=row_tokens), :])

    @pl.when(pl.program_id(0) == pl.num_programs(0) - 1)
    def _():
        os_ref[...] = mixed(xs_ref, as_ref, bs_ref)


def _ffn_final_kernel(xp_ref, xs_ref, g_ref, w_in_ref, w_out_ref, gf_ref, op_ref, os_ref, slabs,
                      *, ff_chunk, row_tokens):
    def final(x):
        return _rms(_swiglu_residual(x, g_ref, w_in_ref, w_out_ref, ff_chunk), gf_ref[...])

    @pl.when(pl.program_id(0) < pl.num_programs(0) - 1)
    def _():
        r = xp_ref.shape[0]
        d = xp_ref.shape[1] // row_tokens
        for t in range(row_tokens):
            for j in range(d // LANES):
                slabs[j, pl.ds(t, r, stride=row_tokens), :] = (
                    xp_ref[:, t * d + j * LANES:t * d + (j + 1) * LANES])
        op_ref[...] = final(jnp.concatenate([slabs[j] for j in range(d // LANES)], axis=1))

    @pl.when(pl.program_id(0) == pl.num_programs(0) - 1)
    def _():
        os_ref[...] = final(xs_ref[...])


def _prompt_tiles(n_tiles):
    return lambda i: (jnp.minimum(i, n_tiles - 1), 0)


def _whole(arr):
    return pl.BlockSpec(arr.shape, lambda i: (0,) * arr.ndim)


def _mix_ffn(xp, ap, bp, xs, a_s, b_s, wo, g, w_in, w_out, *, layer, tm, ff_chunk, row_tokens):
    t, d = xp.shape
    n_tiles = t // tm
    row = lambda arr: pl.BlockSpec((tm, arr.shape[1]), _prompt_tiles(n_tiles))
    return pl.pallas_call(
        functools.partial(_mix_ffn_kernel, ff_chunk=ff_chunk, row_tokens=row_tokens),
        grid=(n_tiles + 1,),
        in_specs=[row(xp), row(ap), row(bp), _whole(xs), _whole(a_s), _whole(b_s),
                  _const_spec(wo.shape), _const_spec(g.shape),
                  _layer_spec(w_in.shape, layer), _layer_spec(w_out.shape, layer)],
        out_specs=[pl.BlockSpec((tm // row_tokens, row_tokens * d), _prompt_tiles(n_tiles)),
                   _whole(xs)],
        out_shape=[jax.ShapeDtypeStruct((t // row_tokens, row_tokens * d), F32),
                   jax.ShapeDtypeStruct(xs.shape, F32)],
        scratch_shapes=[pltpu.VMEM((d // LANES, tm, LANES), F32)],
        compiler_params=_params(("arbitrary",)),
        name="mix_ffn",
    )(xp, ap, bp, xs, a_s, b_s, wo, g, w_in, w_out)


def _ffn_final(xp, xs, g, w_in, w_out, gf, *, layer, tm, ff_chunk, row_tokens):
    rows, width = xp.shape
    d = width // row_tokens
    t = rows * row_tokens
    n_tiles = t // tm
    return pl.pallas_call(
        functools.partial(_ffn_final_kernel, ff_chunk=ff_chunk, row_tokens=row_tokens),
        grid=(n_tiles + 1,),
        in_specs=[pl.BlockSpec((tm // row_tokens, width), _prompt_tiles(n_tiles)), _whole(xs),
                  _const_spec(g.shape), _layer_spec(w_in.shape, layer),
                  _layer_spec(w_out.shape, layer), _const_spec(gf.shape)],
        out_specs=[pl.BlockSpec((tm, d), _prompt_tiles(n_tiles)), _whole(xs)],
        out_shape=[jax.ShapeDtypeStruct((t, d), F32), jax.ShapeDtypeStruct(xs.shape, F32)],
        scratch_shapes=[pltpu.VMEM((d // LANES, tm, LANES), F32)],
        compiler_params=_params(("arbitrary",)),
        name="ffn_final",
    )(xp, xs, g, w_in, w_out, gf)


def _expand_groups(compact, r1, c1):
    n_slabs, rows, k = compact.shape
    width = (k // c1) * SLAB_GROUPS * c1
    src = lax.broadcasted_iota(jnp.int32, (k, width), 0)
    dst = lax.broadcasted_iota(jnp.int32, (k, width), 1)
    select = (src == (dst // (SLAB_GROUPS * c1)) * c1 + dst % c1).astype(F32)
    full = jnp.dot(compact.reshape(n_slabs * rows, k), select,
                   precision=lax.Precision.HIGHEST).reshape(n_slabs, rows, width)
    row_g = (lax.broadcasted_iota(jnp.int32, full.shape, 1) // r1) % SLAB_GROUPS
    col_h = (lax.broadcasted_iota(jnp.int32, full.shape, 2) // c1) % SLAB_GROUPS
    return jnp.where(row_g == col_h, full, 0.0).astype(BF16)


def _s5_rows_kernel(x_ref, g_ref, s0re_ref, s0im_ref, coef_ref, wb_ref, wy_ref, wc_ref, d_ref,
                    wglu_ref, bglu_ref, o_ref, sre_ref, sim_ref,
                    re_s, im_s, gel_s, car_re, car_im, *, row_tokens, n_seq):
    r, width = x_ref.shape
    d = width // row_tokens
    n_slabs = d // LANES
    slab_state = re_s.shape[1] // n_slabs

    xs = [x_ref[:, t * d:(t + 1) * d] for t in range(row_tokens)]
    hs = [_rms(x, g_ref[...]) for x in xs]
    hb = [h.astype(BF16) for h in hs]

    def slab_lhs(j):
        return jnp.concatenate([h[:, j * LANES:(j + 1) * LANES] for h in hb], axis=1)

    def state_cols(j):
        return slice(j * slab_state, (j + 1) * slab_state)

    def window(j):
        w = jnp.dot(slab_lhs(j), wb_ref[j], preferred_element_type=F32)
        re_s[:, state_cols(j)] = w[:, :slab_state]
        im_s[:, state_cols(j)] = w[:, slab_state:]

    n_tiles = r // SUBLANES

    def scan_rows(j, lo, hi):
        st = state_cols(j)
        cr, ci = car_re[:, st], car_im[:, st]
        first = lax.broadcasted_iota(jnp.int32, (SUBLANES, slab_state), 0) == 0
        for i in range(lo, hi):
            rows = slice(i * SUBLANES, (i + 1) * SUBLANES)
            wr, wi = re_s[rows, st], im_s[rows, st]
            for k, shift in enumerate(SCAN_SHIFTS):
                ar, ai = coef_ref[2 * k, :, st], coef_ref[2 * k + 1, :, st]
                sr = pltpu.roll(wr, shift, 0)
                si = pltpu.roll(wi, shift, 0)
                wr, wi = wr + ar * sr - ai * si, wi + ar * si + ai * sr
            pr = coef_ref[2 * len(SCAN_SHIFTS), :, st]
            pi = coef_ref[2 * len(SCAN_SHIFTS) + 1, :, st]
            wr, wi = wr + pr * cr - pi * ci, wi + pr * ci + pi * cr
            re_s[rows, st] = jnp.where(first, cr, pltpu.roll(wr, 1, 0))
            im_s[rows, st] = jnp.where(first, ci, pltpu.roll(wi, 1, 0))
            cr, ci = wr[SUBLANES - 1:SUBLANES, :], wi[SUBLANES - 1:SUBLANES, :]
        car_re[:, st] = cr
        car_im[:, st] = ci
        if hi == n_tiles:
            sre_ref[:, st] = cr
            sim_ref[:, st] = ci

    def slab_outputs(j):
        cols = slice(j * LANES, (j + 1) * LANES)
        st = state_cols(j)
        s_cat = jnp.concatenate([re_s[:, st], im_s[:, st]], axis=1).astype(BF16)
        y = (jnp.dot(slab_lhs(j), wy_ref[j], preferred_element_type=F32)
             + jnp.dot(s_cat, wc_ref[j], preferred_element_type=F32))
        for t in range(row_tokens):
            yt = y[:, t * LANES:(t + 1) * LANES] + d_ref[:, cols] * hs[t][:, cols]
            gel_s[t * r:(t + 1) * r, cols] = jax.nn.gelu(yt).astype(BF16)

    if n_seq == 1:
        @pl.when(pl.program_id(1) == 0)
        def _():
            car_re[...] = s0re_ref[...]
            car_im[...] = s0im_ref[...]

        window(0)
        for j in range(n_slabs):
            if j + 1 < n_slabs:
                window(j + 1)
            if j >= 1:
                slab_outputs(j - 1)
            scan_rows(j, 0, n_tiles)
        slab_outputs(n_slabs - 1)
    else:
        for j in range(n_slabs):
            window(j)
        a_re = coef_ref[0][1:2, :]
        a_im = coef_ref[1][1:2, :]
        cr, ci = s0re_ref[...], s0im_ref[...]
        for n in range(r // n_seq):
            rows = slice(n * n_seq, (n + 1) * n_seq)
            wr, wi = re_s[rows, :], im_s[rows, :]
            re_s[rows, :] = cr
            im_s[rows, :] = ci
            cr, ci = a_re * cr - a_im * ci + wr, a_re * ci + a_im * cr + wi
        sre_ref[...] = cr
        sim_ref[...] = ci
        for j in range(n_slabs):
            slab_outputs(j)
    z = jnp.dot(gel_s[...], wglu_ref[...], preferred_element_type=F32) + bglu_ref[...]
    for t in range(row_tokens):
        zt = z[t * r:(t + 1) * r, :]
        o_ref[:, t * d:(t + 1) * d] = xs[t] + zt[:, :d] * jax.nn.sigmoid(zt[:, d:])


def _s5_rows_mix(x, g, s0re, s0im, coef, wb, wy, wc, dskip, wglu, bglu, *, batch, rows_per_seq,
                 r, row_tokens, step_major, name):
    rows, width = x.shape
    d = width // row_tokens
    n_state = s0re.shape[-1]
    if step_major:
        grid = (1, 1)
        r = rows
        state_spec = pl.BlockSpec((batch, n_state), lambda b, i: (0, 0))
        x_spec = pl.BlockSpec((rows, width), lambda b, i: (0, 0))
    else:
        nt = rows_per_seq // r
        grid = (batch, nt)
        state_spec = pl.BlockSpec((None, 1, n_state), lambda b, i: (b, 0, 0))
        x_spec = pl.BlockSpec((r, width), lambda b, i: (b * nt + i, 0))
    consts = (g, coef, wb, wy, wc, dskip, wglu, bglu)
    x1, sre, sim = pl.pallas_call(
        functools.partial(_s5_rows_kernel, row_tokens=row_tokens,
                          n_seq=batch if step_major else 1),
        grid=grid,
        in_specs=[x_spec, _const_spec(g.shape), state_spec, state_spec]
                 + [_const_spec(c.shape) for c in consts[1:]],
        out_specs=[x_spec, state_spec, state_spec],
        out_shape=[jax.ShapeDtypeStruct((rows, width), F32),
                   jax.ShapeDtypeStruct(s0re.shape, F32),
                   jax.ShapeDtypeStruct(s0im.shape, F32)],
        scratch_shapes=[pltpu.VMEM((r, n_state), F32), pltpu.VMEM((r, n_state), F32),
                        pltpu.VMEM((r * row_tokens, d), BF16),
                        pltpu.VMEM((1, n_state), F32), pltpu.VMEM((1, n_state), F32)],
        compiler_params=_params(("parallel", "arbitrary")),
        name=name,
    )(x, g, s0re, s0im, *consts[1:])
    return x1, sre, sim


def _s5_row_tables(a_re, a_im, log_dt, b_re, b_im, c_re, c_im, row_tokens):
    groups, n_p = a_re.shape
    p_tok = row_tokens
    n_slabs = groups // SLAB_GROUPS
    dt = jnp.exp(log_dt)[:, None]
    ks = jnp.arange(p_tok * SUBLANES + 1, dtype=F32)[:, None, None]
    mag = jnp.exp((a_re * dt)[None] * ks)
    ang = (a_im * dt)[None] * ks
    pw_re, pw_im = mag * jnp.cos(ang), mag * jnp.sin(ang)
    den = a_re * a_re + a_im * a_im
    q_re = ((pw_re[1] - 1.0) * a_re + pw_im[1] * a_im) / den
    q_im = (pw_im[1] * a_re - (pw_re[1] - 1.0) * a_im) / den
    bb_re = q_re[:, :, None] * b_re - q_im[:, :, None] * b_im
    bb_im = q_re[:, :, None] * b_im + q_im[:, :, None] * b_re

    rows = jnp.arange(SUBLANES)
    coef = []
    for shift in SCAN_SHIFTS:
        keep = (rows >= shift)[:, None].astype(F32)
        coef += [pw_re[p_tok * shift].reshape(1, -1) * keep,
                 pw_im[p_tok * shift].reshape(1, -1) * keep]
    pick = lambda pw, idx: jnp.stack([pw[k] for k in idx])
    carry = [p_tok * (r + 1) for r in range(SUBLANES)]
    coef += [pick(pw_re, carry).reshape(SUBLANES, -1), pick(pw_im, carry).reshape(SUBLANES, -1)]
    coef = jnp.stack(coef)

    back = [p_tok - 1 - s for s in range(p_tok)]
    er, ei = pick(pw_re, back)[:, :, :, None], pick(pw_im, back)[:, :, :, None]
    lay_b = lambda m: m.reshape(p_tok, n_slabs, SLAB_GROUPS, n_p, SSM_GROUP).transpose(
        1, 0, 2, 4, 3).reshape(n_slabs, -1, n_p)
    wb = _expand_groups(jnp.concatenate([lay_b(er * bb_re - ei * bb_im),
                                         lay_b(er * bb_im + ei * bb_re)], axis=2), SSM_GROUP, n_p)

    kr, ki = pw_re[:p_tok + 1][:, :, None, :], pw_im[:p_tok + 1][:, :, None, :]
    ck_re, ck_im = c_re * kr - c_im * ki, c_re * ki + c_im * kr

    lay_c = lambda m: m.reshape(p_tok, n_slabs, SLAB_GROUPS, SSM_GROUP, n_p).transpose(
        1, 2, 4, 0, 3).reshape(n_slabs, SLAB_GROUPS * n_p, -1)
    wc = _expand_groups(jnp.concatenate([lay_c(ck_re[1:]), -lay_c(ck_im[1:])], axis=1),
                        n_p, SSM_GROUP)

    bt_re, bt_im = bb_re.transpose(0, 2, 1)[None, :, :, None, :], bb_im.transpose(0, 2, 1)[
        None, :, :, None, :]
    lag = jnp.sum(ck_re[:p_tok, :, None] * bt_re - ck_im[:p_tok, :, None] * bt_im, axis=-1)
    lag = lag.reshape(p_tok, n_slabs, SLAB_GROUPS * SSM_GROUP, SSM_GROUP)
    zero = jnp.zeros_like(lag[0])
    toep = jnp.concatenate(
        [jnp.concatenate([lag[t - s] if t >= s else zero for t in range(p_tok)], axis=2)
         for s in range(p_tok)], axis=1)
    wy = _expand_groups(toep, SSM_GROUP, SSM_GROUP)
    return coef, wb, wy, wc


def kernel(x_prompt, x_sample, cache_k, cache_v, page_table, state_ssm_re, state_ssm_im,
           norm_mix, norm_ffn, norm_final, w_in_even, w_out_even, sgu_norm, sgu_w, sgu_b,
           lambda_q1, lambda_k1, lambda_q2, lambda_k2, attn_subln,
           ssm_a_re, ssm_a_im, ssm_log_dt, ssm_b_re, ssm_b_im, ssm_c_re, ssm_c_im, ssm_d,
           w_glu, b_glu, w_ffn_in, w_ffn_out):
    batch, seq, d = x_prompt.shape
    dec_batch, t_new, _ = x_sample.shape
    depth = norm_mix.shape[0]
    assert depth == 2 and seq % CHUNK == 0 and t_new == SUBLANES
    tp = batch * seq
    ts = dec_batch * t_new
    xp = x_prompt.reshape(tp, d)
    xs = x_sample.reshape(ts, d)
    tm_prompt = TOKEN_TILE
    ff_chunk = FF_CHUNK
    row = lambda v: v.reshape(1, -1)

    lam_init = 0.8 - 0.6 * math.exp(-0.3 * 0)
    w_in0 = w_in_even[0].astype(BF16)
    a_width = sgu_norm.shape[1]
    gw = a_width // A_GROUPS
    tril = jnp.tril(jnp.ones((CHUNK, CHUNK), bool))
    mix_p = jnp.where(tril, sgu_w[0], 0).astype(BF16)
    bias_p = jnp.repeat(sgu_b[0].T, gw, axis=1)
    small = jnp.where(tril[:t_new, :t_new], sgu_w[0][:, :t_new, :t_new], 0)
    row_i = lax.broadcasted_iota(jnp.int32, (ts, ts), 0)
    col_i = lax.broadcasted_iota(jnp.int32, (ts, ts), 1)
    rep = (row_i[:, :t_new] % t_new == col_i[:, :t_new]).astype(F32)
    tiled = jnp.einsum('at,gts,bs->gab', rep, small, rep, precision=lax.Precision.HIGHEST)
    mix_s = jnp.where(row_i // t_new == col_i // t_new, tiled, 0.0).astype(BF16)
    bias_s = jnp.tile(bias_p[:t_new], (dec_batch, 1))
    lam_vecs = [row(lambda_q1[0]), row(lambda_k1[0]), row(lambda_q2[0]), row(lambda_k2[0]),
                row(attn_subln[0])]

    later_weights = (w_ffn_in.reshape(-1, w_ffn_in.shape[-1]),
                     w_ffn_out.reshape(-1, w_ffn_out.shape[-1]), w_glu[0], w_out_even[0])
    (a_p, q_p, k_p, vb_p, kb_p, vbb_p), (w_ffn_in_b, w_ffn_out_b, w_glu0, w_out0) = _even_in(
        xp, row(norm_mix[0]), w_in0, row(sgu_norm[0]), mix_p, bias_p, tm=tm_prompt, prompt=True,
        cast_weights=later_weights)
    w_ffn_in_b = w_ffn_in_b.reshape(w_ffn_in.shape)
    w_ffn_out_b = w_ffn_out_b.reshape(w_ffn_out.shape)
    (a_s, q_s, k_s, vb_s, vn_s), _ = _even_in(
        xs, row(norm_mix[0]), w_in0, row(sgu_norm[0]), mix_s, bias_s, tm=ts, prompt=False)
    n_layers, n_phys, page_size, heads, _ = cache_k.shape
    page_view = (n_layers, n_phys, page_size * heads, HEAD_DIM)
    b_p, b_s = _attention(page_table, q_p, kb_p, vbb_p, q_s, k_s, vb_s,
                          cache_k.reshape(page_view), cache_v.reshape(page_view), 0, *lam_vecs,
                          batch=batch, seq=seq, tq=ATTN_Q_TILE, t_new=t_new, lam_init=lam_init)
    xp, xs = _mix_ffn(xp, a_p, b_p, xs, a_s, b_s, w_out0, row(norm_ffn[0]), w_ffn_in_b,
                      w_ffn_out_b, layer=0, tm=tm_prompt, ff_chunk=ff_chunk,
                      row_tokens=ROW_TOKENS)

    groups, n_p = ssm_a_re.shape[1:]
    n_state = groups * n_p
    ssm = (ssm_a_re[0], ssm_a_im[0], ssm_log_dt[0], ssm_b_re[0], ssm_b_im[0], ssm_c_re[0],
           ssm_c_im[0])
    s5_args = (*_s5_row_tables(*ssm, ROW_TOKENS), row(ssm_d[0]), w_glu0, row(b_glu[0]))
    zeros = jnp.zeros((batch, 1, n_state), F32)
    xp, p_re, p_im = _s5_rows_mix(xp, row(norm_mix[1]), zeros, zeros, *s5_args, batch=batch,
                                  rows_per_seq=seq // ROW_TOKENS, r=S5_BLOCK_ROWS,
                                  row_tokens=ROW_TOKENS, step_major=False, name="s5_prompt")
    steps = t_new // ROW_TOKENS
    pack = lambda v: v.reshape(dec_batch, steps, -1).transpose(1, 0, 2).reshape(
        steps * dec_batch, -1)
    unpack = lambda v: v.reshape(steps, dec_batch, -1).transpose(1, 0, 2).reshape(ts, d)
    xs, s_re, s_im = _s5_rows_mix(pack(xs), row(norm_mix[1]),
                                  state_ssm_re[0].reshape(dec_batch, n_state),
                                  state_ssm_im[0].reshape(dec_batch, n_state), *s5_args,
                                  batch=dec_batch, rows_per_seq=steps, r=None,
                                  row_tokens=ROW_TOKENS, step_major=True, name="s5_sample")
    xs = unpack(xs)
    yp, ys = _ffn_final(xp, xs, row(norm_ffn[1]), w_ffn_in_b, w_ffn_out_b, row(norm_final),
                        layer=1, tm=tm_prompt, ff_chunk=ff_chunk, row_tokens=ROW_TOKENS)

    st = lambda s, n: s.reshape(1, n, groups, n_p)
    return (yp.reshape(batch, seq, d), ys.reshape(dec_batch, t_new, d),
            k_p.reshape(1, batch, seq, heads, HEAD_DIM), vb_p.reshape(1, batch, seq, heads, HEAD_DIM),
            k_s.reshape(1, dec_batch, t_new, heads, HEAD_DIM),
            vb_s.reshape(1, dec_batch, t_new, heads, HEAD_DIM),
            vn_s.reshape(1, dec_batch, t_new, a_width),
            st(p_re, batch), st(p_im, batch), st(s_re, dec_batch), st(s_im, dec_batch))
```

```python
import functools
import math

import jax
import jax.numpy as jnp
from jax import lax
from jax.experimental import pallas as pl
from jax.experimental.pallas import tpu as pltpu

F32 = jnp.float32
BF16 = jnp.bfloat16

EPS = 1e-6
LANES = 128
SUBLANES = 8
V7X_VMEM_BYTES = 64 * 1024 * 1024
VMEM_LIMIT = V7X_VMEM_BYTES * 7 // 8
MASK_VALUE = -0.7 * float(jnp.finfo(jnp.float32).max)

CHUNK = 128
A_GROUPS = 4
HEAD_DIM = 128
QK_HALF = 64
LOG2E = math.log2(math.e)
SSM_GROUP = 16
SLAB_GROUPS = LANES // SSM_GROUP
SCAN_SHIFTS = (1, 2, 4)
ROW_TOKENS = 4
TOKEN_TILE = 512
S5_BLOCK_ROWS = TOKEN_TILE // ROW_TOKENS
ATTN_Q_TILE = 512
FF_CHUNK = 256

def _const_spec(shape):
    zeros = (0,) * len(shape)
    return pl.BlockSpec(shape, lambda *_: zeros, pipeline_mode=pl.Buffered(1))


def _layer_spec(shape, layer):
    index = (layer,) + (0,) * (len(shape) - 1)
    return pl.BlockSpec((None,) + tuple(shape[1:]), lambda *_: index, pipeline_mode=pl.Buffered(1))


def _rms(x, g):
    return x * lax.rsqrt(jnp.mean(x * x, axis=-1, keepdims=True) + EPS) * g


def _params(semantics):
    return pltpu.CompilerParams(dimension_semantics=semantics, vmem_limit_bytes=VMEM_LIMIT)


def _even_in_kernel(x_ref, g_ref, w_ref, sn_ref, mix_ref, bias_ref, *rest, prompt, a_width,
                    n_cast):
    cast_in, outs, cast_out = rest[:n_cast], rest[n_cast:len(rest) - n_cast], rest[len(rest) - n_cast:]
    for src, dst in zip(cast_in, cast_out):
        dst[...] = src[...].astype(BF16)
    if prompt:
        a_ref, q_ref, k_ref, vb_ref, kb_ref, vbb_ref = outs
    else:
        a_ref, q_ref, k_ref, vb_ref, vn_ref = outs
    tm = x_ref.shape[0]
    rows_per_mix = mix_ref.shape[1]
    gw = a_width // A_GROUPS
    h = _rms(x_ref[...], g_ref[...]).astype(BF16)

    def proj(c0, width):
        return jnp.dot(h, w_ref[:, c0:c0 + width], preferred_element_type=F32)

    u = jax.nn.gelu(proj(0, a_width))
    v = _rms(jax.nn.gelu(proj(a_width, a_width)), sn_ref[...])
    if not prompt:
        vn_ref[...] = v
    v16 = v.astype(BF16)
    o = 2 * a_width
    bw = q_ref.shape[1]
    q_ref[...] = (proj(o, bw) * (QK_HALF ** -0.5 * LOG2E)).astype(q_ref.dtype)
    k = proj(o + bw, bw)
    vb = proj(o + 2 * bw, bw)
    heads = bw // HEAD_DIM
    for hd in range(heads):
        k_ref[pl.ds(hd, tm, stride=heads), :] = k[:, hd * HEAD_DIM:(hd + 1) * HEAD_DIM]
        vb_ref[pl.ds(hd, tm, stride=heads), :] = vb[:, hd * HEAD_DIM:(hd + 1) * HEAD_DIM]
    if prompt:
        kb_ref[...] = k.astype(BF16)
        vbb_ref[...] = vb.astype(BF16)
    for c in range(tm // rows_per_mix):
        r0 = c * rows_per_mix
        for g in range(A_GROUPS):
            c0 = g * gw
            gate = jnp.dot(mix_ref[g], v16[r0:r0 + rows_per_mix, c0:c0 + gw],
                           preferred_element_type=F32) + bias_ref[:, c0:c0 + gw]
            a_ref[r0:r0 + rows_per_mix, c0:c0 + gw] = (
                u[r0:r0 + rows_per_mix, c0:c0 + gw] * gate).astype(a_ref.dtype)


def _even_in(x, g, w, sn, mix, bias, *, tm, prompt, cast_weights=()):
    t, d = x.shape
    steps = t // tm
    cast_specs = [pl.BlockSpec((c.shape[0] // steps, c.shape[1]), lambda i: (i, 0))
                  for c in cast_weights]
    a_width = sn.shape[1]
    bw = (w.shape[1] - 2 * a_width) // 3
    act = BF16 if prompt else F32
    row = lambda width: pl.BlockSpec((tm, width), lambda i: (i, 0))
    heads = bw // HEAD_DIM
    head_rows = pl.BlockSpec((tm * heads, HEAD_DIM), lambda i: (i, 0))
    out_shape = [jax.ShapeDtypeStruct((t, a_width), act), jax.ShapeDtypeStruct((t, bw), act),
                 jax.ShapeDtypeStruct((t * heads, HEAD_DIM), F32),
                 jax.ShapeDtypeStruct((t * heads, HEAD_DIM), F32)]
    out_specs = [row(a_width), row(bw), head_rows, head_rows]
    if prompt:
        out_shape += [jax.ShapeDtypeStruct((t, bw), BF16)] * 2
        out_specs += [row(bw), row(bw)]
    else:
        out_shape += [jax.ShapeDtypeStruct((t, a_width), F32)]
        out_specs += [row(a_width)]
    n_out = len(out_shape)
    outs = pl.pallas_call(
        functools.partial(_even_in_kernel, prompt=prompt, a_width=a_width,
                          n_cast=len(cast_weights)),
        grid=(steps,),
        in_specs=[row(d), _const_spec(g.shape), _const_spec(w.shape), _const_spec(sn.shape),
                  _const_spec(mix.shape), _const_spec(bias.shape)] + cast_specs,
        out_specs=out_specs + cast_specs,
        out_shape=out_shape + [jax.ShapeDtypeStruct(c.shape, BF16) for c in cast_weights],
        compiler_params=_params(("parallel",)),
        name="even_in_prompt" if prompt else "even_in_sample",
    )(x, g, w, sn, mix, bias, *cast_weights)
    return outs[:n_out], outs[n_out:]


def _split_halves(q):
    lane = lax.broadcasted_iota(jnp.int32, q.shape, 1)
    zero = jnp.zeros_like(q)
    return jnp.where(lane < QK_HALF, q, zero), jnp.where(lane >= QK_HALF, q, zero)


def _softmax_step(s, vblk, m_ref, l_ref, acc_ref, rows):
    tiles = [s[:, t * LANES:(t + 1) * LANES] for t in range(s.shape[1] // LANES)]
    m_prev = m_ref[rows, :]
    m_next = jnp.maximum(
        m_prev, jnp.max(functools.reduce(jnp.maximum, tiles), axis=1, keepdims=True))
    p_tiles = [jnp.exp2(t - m_next) for t in tiles]
    alpha = jnp.exp2(m_prev - m_next)
    l_ref[rows, :] = alpha * l_ref[rows, :] + functools.reduce(jnp.add, p_tiles)
    p = jnp.concatenate([t.astype(BF16) for t in p_tiles], axis=1)
    acc_ref[rows, :] = alpha * acc_ref[rows, :] + jnp.dot(p, vblk, preferred_element_type=F32)
    m_ref[rows, :] = m_next


def _row_sum(l):
    return jnp.sum(l, axis=1, keepdims=True)


def _diff_lambda(lq1, lk1, lq2, lk2, lam_init):
    e1 = jnp.exp(jnp.sum(lq1[...] * lk1[...], axis=1, keepdims=True))
    e2 = jnp.exp(jnp.sum(lq2[...] * lk2[...], axis=1, keepdims=True))
    return e1 - e2 + lam_init


def _head_out(o1, o2, lam, sub, lam_init):
    o = o1 - lam * o2
    return _rms(o, sub) * (1.0 - lam_init)


def _prompt_attn_step(i, q_ref, k_ref, v_ref, lam, sub_ref, o_ref, qs_ref, m_ref, l_ref, acc_ref,
                      lam_init):
    tq = q_ref.shape[0]
    q1, q2 = _split_halves(q_ref[...])
    qs_ref[0:tq, :] = q1
    qs_ref[tq:2 * tq, :] = q2
    m_ref[...] = jnp.full(m_ref.shape, MASK_VALUE, F32)
    l_ref[...] = jnp.zeros(l_ref.shape, F32)
    acc_ref[...] = jnp.zeros(acc_ref.shape, F32)
    all_rows = slice(0, 2 * tq)

    def scores(j):
        kv_rows = pl.ds(pl.multiple_of(j * tq, tq), tq)
        s = lax.dot_general(qs_ref[...], k_ref[kv_rows, :], (((1,), (1,)), ((), ())),
                            preferred_element_type=F32)
        return s, v_ref[kv_rows, :]

    def full_block(j):
        s, vblk = scores(j)
        _softmax_step(s, vblk, m_ref, l_ref, acc_ref, all_rows)

    def full_quad(jj, carry):
        for u in range(4):
            full_block(4 * jj + u)
        return carry

    lax.fori_loop(0, i // 4, full_quad, 0)

    @pl.when(i % 4 >= 2)
    def _():
        full_block(4 * (i // 4))
        full_block(4 * (i // 4) + 1)

    @pl.when(i % 2 == 1)
    def _():
        full_block(i - 1)
    s, vblk = scores(i)
    row = lax.broadcasted_iota(jnp.int32, s.shape, 0)
    col = lax.broadcasted_iota(jnp.int32, s.shape, 1)
    qpos = jnp.where(row >= tq, row - tq, row)
    s = jnp.where(qpos >= col, s, MASK_VALUE)
    _softmax_step(s, vblk, m_ref, l_ref, acc_ref, all_rows)

    o1 = acc_ref[0:tq, :] / _row_sum(l_ref[0:tq, :])
    o2 = acc_ref[tq:2 * tq, :] / _row_sum(l_ref[tq:2 * tq, :])
    o_ref[...] = _head_out(o1, o2, lam, sub_ref[...], lam_init).astype(o_ref.dtype)


def _head_match(shape, heads, rows_per_head):
    row = lax.broadcasted_iota(jnp.int32, shape, 0)
    col = lax.broadcasted_iota(jnp.int32, shape, 1)
    return row, col, lax.rem(col, heads) == row // rows_per_head


def _sample_attn_init(q_ref, qs_ref, bias_ref, m_ref, l_ref, acc_ref, heads):
    pieces = []
    for h in range(heads):
        pieces += _split_halves(q_ref[:, h * HEAD_DIM:(h + 1) * HEAD_DIM])
    qs_ref[...] = jnp.concatenate(pieces, axis=0).astype(BF16)
    _, _, ok = _head_match(bias_ref.shape, heads, 2 * q_ref.shape[0])
    bias_ref[...] = jnp.where(ok, 0.0, MASK_VALUE)
    m_ref[...] = jnp.full(m_ref.shape, MASK_VALUE, F32)
    l_ref[...] = jnp.zeros(l_ref.shape, F32)
    acc_ref[...] = jnp.zeros(acc_ref.shape, F32)


def _sample_attn_step(is_last, q_ref, kn_ref, vn_ref, k_pages, v_pages, lam, sub_ref, o_ref,
                      qs_ref, bias_ref, m_ref, l_ref, acc_ref, heads, lam_init):
    pages_per_step = len(k_pages)
    t_new = q_ref.shape[0]
    rows_per_head = 2 * t_new
    all_rows = slice(0, heads * rows_per_head)

    def scores(kblk):
        return lax.dot_general(qs_ref[...], kblk.astype(BF16), (((1,), (1,)), ((), ())),
                               preferred_element_type=F32)

    kblk = jnp.concatenate([k_pages[p][...] for p in range(pages_per_step)], axis=0)
    vblk = jnp.concatenate([v_pages[p][...] for p in range(pages_per_step)], axis=0)
    _softmax_step(scores(kblk) + bias_ref[...], vblk.astype(BF16), m_ref, l_ref, acc_ref, all_rows)

    @pl.when(is_last)
    def _():
        pad = jnp.zeros((LANES - kn_ref.shape[0], HEAD_DIM), F32)
        s = scores(jnp.concatenate([kn_ref[...], pad], axis=0))
        row, col, ok = _head_match(s.shape, heads, rows_per_head)
        s = jnp.where(ok, s, MASK_VALUE)
        s = jnp.where(col // heads <= lax.rem(row, t_new), s, MASK_VALUE)
        vblk = jnp.concatenate([vn_ref[...], pad], axis=0)
        _softmax_step(s, vblk.astype(BF16), m_ref, l_ref, acc_ref, all_rows)
        for h in range(heads):
            r0 = h * rows_per_head
            o1 = acc_ref[r0:r0 + t_new, :] / _row_sum(l_ref[r0:r0 + t_new, :])
            o2 = (acc_ref[r0 + t_new:r0 + rows_per_head, :]
                  / _row_sum(l_ref[r0 + t_new:r0 + rows_per_head, :]))
            o_ref[:, h * HEAD_DIM:(h + 1) * HEAD_DIM] = _head_out(
                o1, o2, lam, sub_ref[...], lam_init)


def _attention_kernel(pt_ref, qp_ref, kp_ref, vp_ref, qs_in_ref, kn_ref, vn_ref, *rest,
                      pages_per_step, heads, sample_steps, lam_init):
    del pt_ref
    k_pages = rest[:pages_per_step]
    v_pages = rest[pages_per_step:2 * pages_per_step]
    (lq1, lk1, lq2, lk2, sub_ref, op_ref, os_ref,
     pq_s, pm_s, pl_s, pacc_s, sq_s, sbias_s, sm_s, sl_s, sacc_s) = rest[2 * pages_per_step:]
    i = pl.program_id(2)
    step = (pl.program_id(0) * pl.num_programs(1) + pl.program_id(1)) * pl.num_programs(2) + i
    j = lax.rem(step, sample_steps)

    @pl.when(j == 0)
    def _():
        _sample_attn_init(qs_in_ref, sq_s, sbias_s, sm_s, sl_s, sacc_s, heads)

    lam = _diff_lambda(lq1, lk1, lq2, lk2, lam_init)
    _prompt_attn_step(i, qp_ref, kp_ref, vp_ref, lam, sub_ref, op_ref, pq_s, pm_s, pl_s, pacc_s,
                      lam_init)
    _sample_attn_step(j == sample_steps - 1, qs_in_ref, kn_ref, vn_ref, k_pages, v_pages, lam,
                      sub_ref, os_ref, sq_s, sbias_s, sm_s, sl_s, sacc_s, heads, lam_init)


def _attention(page_table, q_p, kb_p, vb_p, q_s, k_new, v_new, cache_k, cache_v, layer,
               lq1, lk1, lq2, lk2, sub, *, batch, seq, tq, t_new, lam_init):
    tp, bw = q_p.shape
    ts = q_s.shape[0]
    heads = bw // HEAD_DIM
    nq = seq // tq
    dec_batch, n_pages = page_table.shape
    page_rows = cache_k.shape[2]
    total_steps = batch * heads * nq
    assert total_steps % dec_batch == 0 and n_pages % (total_steps // dec_batch) == 0
    sample_steps = total_steps // dec_batch
    pages_per_step = n_pages // sample_steps
    pt_flat = page_table.reshape(-1)

    def sample_pos(b, h, i):
        step = (b * heads + h) * nq + i
        return step // sample_steps, step % sample_steps

    def page_spec(p):
        def index(b, h, i, pt):
            s, j = sample_pos(b, h, i)
            return (layer, pt[s * n_pages + j * pages_per_step + p], 0, 0)
        return pl.BlockSpec((None, None, page_rows, HEAD_DIM), index)

    seq_block = lambda shape: pl.BlockSpec(shape, lambda b, h, i, pt: (sample_pos(b, h, i)[0], 0))
    vec = lambda a: pl.BlockSpec(a.shape, lambda b, h, i, pt: (0, 0))
    qp_spec = pl.BlockSpec((tq, HEAD_DIM), lambda b, h, i, pt: (b * nq + i, h))
    kv_spec = pl.BlockSpec((seq, HEAD_DIM), lambda b, h, i, pt: (b, h))
    rows_s = heads * 2 * t_new
    grid_spec = pltpu.PrefetchScalarGridSpec(
        num_scalar_prefetch=1,
        grid=(batch, heads, nq),
        in_specs=([qp_spec, kv_spec, kv_spec,
                   seq_block((t_new, bw)), seq_block((t_new * heads, HEAD_DIM)),
                   seq_block((t_new * heads, HEAD_DIM))]
                  + [page_spec(p) for p in range(pages_per_step)] * 2
                  + [vec(lq1), vec(lk1), vec(lq2), vec(lk2), vec(sub)]),
        out_specs=[qp_spec, seq_block((t_new, bw))],
        scratch_shapes=[pltpu.VMEM((2 * tq, HEAD_DIM), BF16),
                        pltpu.VMEM((2 * tq, LANES), F32),
                        pltpu.VMEM((2 * tq, LANES), F32),
                        pltpu.VMEM((2 * tq, HEAD_DIM), F32),
                        pltpu.VMEM((rows_s, HEAD_DIM), BF16),
                        pltpu.VMEM((rows_s, pages_per_step * page_rows), F32),
                        pltpu.VMEM((rows_s, LANES), F32),
                        pltpu.VMEM((rows_s, LANES), F32),
                        pltpu.VMEM((rows_s, HEAD_DIM), F32)])
    return pl.pallas_call(
        functools.partial(_attention_kernel, pages_per_step=pages_per_step, heads=heads,
                          sample_steps=sample_steps, lam_init=lam_init),
        grid_spec=grid_spec,
        out_shape=[jax.ShapeDtypeStruct((tp, bw), BF16), jax.ShapeDtypeStruct((ts, bw), F32)],
        compiler_params=_params(("arbitrary", "arbitrary", "arbitrary")),
        name="attention",
    )(pt_flat, q_p, kb_p, vb_p, q_s, k_new, v_new,
      *([cache_k] * pages_per_step), *([cache_v] * pages_per_step), lq1, lk1, lq2, lk2, sub)


def _swiglu_residual(x1, g_ref, w_in_ref, w_out_ref, ff_chunk):
    d_ff = w_out_ref.shape[0]
    h = _rms(x1, g_ref[...]).astype(BF16)
    acc = x1
    for c0 in range(0, d_ff, ff_chunk):
        gate = jnp.dot(h, w_in_ref[:, c0:c0 + ff_chunk], preferred_element_type=F32)
        up = jnp.dot(h, w_in_ref[:, d_ff + c0:d_ff + c0 + ff_chunk], preferred_element_type=F32)
        act = (gate * jax.nn.sigmoid(gate) * up).astype(BF16)
        acc = acc + jnp.dot(act, w_out_ref[c0:c0 + ff_chunk, :], preferred_element_type=F32)
    return acc


def _mix_ffn_kernel(xp_ref, ap_ref, bp_ref, xs_ref, as_ref, bs_ref, wo_ref, g_ref, w_in_ref,
                    w_out_ref, op_ref, os_ref, slabs, *, ff_chunk, row_tokens):
    def mixed(x_ref, a_ref, b_ref):
        ab = jnp.concatenate([a_ref[...].astype(BF16), b_ref[...].astype(BF16)], axis=1)
        x1 = x_ref[...] + jnp.dot(ab, wo_ref[...], preferred_element_type=F32)
        return _swiglu_residual(x1, g_ref, w_in_ref, w_out_ref, ff_chunk)

    @pl.when(pl.program_id(0) < pl.num_programs(0) - 1)
    def _():
        tm, d = xp_ref.shape
        y = mixed(xp_ref, ap_ref, bp_ref)
        for j in range(d // LANES):
            slabs[j] = y[:, j * LANES:(j + 1) * LANES]
        for t in range(row_tokens):
            for j in range(d // LANES):
                op_ref[:, t * d + j * LANES:t * d + (j + 1) * LANES] = (
                    slabs[j, pl.ds(t, tm // row_tokens, stride=row_tokens), :])

    @pl.when(pl.program_id(0) == pl.num_programs(0) - 1)
    def _():
        os_ref[...] = mixed(xs_ref, as_ref, bs_ref)


def _ffn_final_kernel(xp_ref, xs_ref, g_ref, w_in_ref, w_out_ref, gf_ref, op_ref, os_ref, slabs,
                      *, ff_chunk, row_tokens):
    def final(x):
        return _rms(_swiglu_residual(x, g_ref, w_in_ref, w_out_ref, ff_chunk), gf_ref[...])

    @pl.when(pl.program_id(0) < pl.num_programs(0) - 1)
    def _():
        r = xp_ref.shape[0]
        d = xp_ref.shape[1] // row_tokens
        for t in range(row_tokens):
            for j in range(d // LANES):
                slabs[j, pl.ds(t, r, stride=row_tokens), :] = (
                    xp_ref[:, t * d + j * LANES:t * d + (j + 1) * LANES])
        op_ref[...] = final(jnp.concatenate([slabs[j] for j in range(d // LANES)], axis=1))

    @pl.when(pl.program_id(0) == pl.num_programs(0) - 1)
    def _():
        os_ref[...] = final(xs_ref[...])


def _prompt_tiles(n_tiles):
    return lambda i: (jnp.minimum(i, n_tiles - 1), 0)


def _whole(arr):
    return pl.BlockSpec(arr.shape, lambda i: (0,) * arr.ndim)


def _mix_ffn(xp, ap, bp, xs, a_s, b_s, wo, g, w_in, w_out, *, layer, tm, ff_chunk, row_tokens):
    t, d = xp.shape
    n_tiles = t // tm
    row = lambda arr: pl.BlockSpec((tm, arr.shape[1]), _prompt_tiles(n_tiles))
    return pl.pallas_call(
        functools.partial(_mix_ffn_kernel, ff_chunk=ff_chunk, row_tokens=row_tokens),
        grid=(n_tiles + 1,),
        in_specs=[row(xp), row(ap), row(bp), _whole(xs), _whole(a_s), _whole(b_s),
                  _const_spec(wo.shape), _const_spec(g.shape),
                  _layer_spec(w_in.shape, layer), _layer_spec(w_out.shape, layer)],
        out_specs=[pl.BlockSpec((tm // row_tokens, row_tokens * d), _prompt_tiles(n_tiles)),
                   _whole(xs)],
        out_shape=[jax.ShapeDtypeStruct((t // row_tokens, row_tokens * d), F32),
                   jax.ShapeDtypeStruct(xs.shape, F32)],
        scratch_shapes=[pltpu.VMEM((d // LANES, tm, LANES), F32)],
        compiler_params=_params(("arbitrary",)),
        name="mix_ffn",
    )(xp, ap, bp, xs, a_s, b_s, wo, g, w_in, w_out)


def _ffn_final(xp, xs, g, w_in, w_out, gf, *, layer, tm, ff_chunk, row_tokens):
    rows, width = xp.shape
    d = width // row_tokens
    t = rows * row_tokens
    n_tiles = t // tm
    return pl.pallas_call(
        functools.partial(_ffn_final_kernel, ff_chunk=ff_chunk, row_tokens=row_tokens),
        grid=(n_tiles + 1,),
        in_specs=[pl.BlockSpec((tm // row_tokens, width), _prompt_tiles(n_tiles)), _whole(xs),
                  _const_spec(g.shape), _layer_spec(w_in.shape, layer),
                  _layer_spec(w_out.shape, layer), _const_spec(gf.shape)],
        out_specs=[pl.BlockSpec((tm, d), _prompt_tiles(n_tiles)), _whole(xs)],
        out_shape=[jax.ShapeDtypeStruct((t, d), F32), jax.ShapeDtypeStruct(xs.shape, F32)],
        scratch_shapes=[pltpu.VMEM((d // LANES, tm, LANES), F32)],
        compiler_params=_params(("arbitrary",)),
        name="ffn_final",
    )(xp, xs, g, w_in, w_out, gf)


def _expand_groups(compact, r1, c1):
    n_slabs, rows, k = compact.shape
    width = (k // c1) * SLAB_GROUPS * c1
    src = lax.broadcasted_iota(jnp.int32, (k, width), 0)
    dst = lax.broadcasted_iota(jnp.int32, (k, width), 1)
    select = (src == (dst // (SLAB_GROUPS * c1)) * c1 + dst % c1).astype(F32)
    full = jnp.dot(compact.reshape(n_slabs * rows, k), select,
                   precision=lax.Precision.HIGHEST).reshape(n_slabs, rows, width)
    row_g = (lax.broadcasted_iota(jnp.int32, full.shape, 1) // r1) % SLAB_GROUPS
    col_h = (lax.broadcasted_iota(jnp.int32, full.shape, 2) // c1) % SLAB_GROUPS
    return jnp.where(row_g == col_h, full, 0.0).astype(BF16)


def _s5_rows_kernel(x_ref, g_ref, s0re_ref, s0im_ref, coef_ref, wb_ref, wy_ref, wc_ref, d_ref,
                    wglu_ref, bglu_ref, o_ref, sre_ref, sim_ref,
                    re_s, im_s, gel_s, car_re, car_im, *, row_tokens, n_seq):
    r, width = x_ref.shape
    d = width // row_tokens
    n_slabs = d // LANES
    slab_state = re_s.shape[1] // n_slabs

    xs = [x_ref[:, t * d:(t + 1) * d] for t in range(row_tokens)]
    hs = [_rms(x, g_ref[...]) for x in xs]
    hb = [h.astype(BF16) for h in hs]

    def slab_lhs(j):
        return jnp.concatenate([h[:, j * LANES:(j + 1) * LANES] for h in hb], axis=1)

    def state_cols(j):
        return slice(j * slab_state, (j + 1) * slab_state)

    def window(j):
        w = jnp.dot(slab_lhs(j), wb_ref[j], preferred_element_type=F32)
        re_s[:, state_cols(j)] = w[:, :slab_state]
        im_s[:, state_cols(j)] = w[:, slab_state:]

    n_tiles = r // SUBLANES

    def scan_rows(j, lo, hi):
        st = state_cols(j)
        cr, ci = car_re[:, st], car_im[:, st]
        first = lax.broadcasted_iota(jnp.int32, (SUBLANES, slab_state), 0) == 0
        for i in range(lo, hi):
            rows = slice(i * SUBLANES, (i + 1) * SUBLANES)
            wr, wi = re_s[rows, st], im_s[rows, st]
            for k, shift in enumerate(SCAN_SHIFTS):
                ar, ai = coef_ref[2 * k, :, st], coef_ref[2 * k + 1, :, st]
                sr = pltpu.roll(wr, shift, 0)
                si = pltpu.roll(wi, shift, 0)
                wr, wi = wr + ar * sr - ai * si, wi + ar * si + ai * sr
            pr = coef_ref[2 * len(SCAN_SHIFTS), :, st]
            pi = coef_ref[2 * len(SCAN_SHIFTS) + 1, :, st]
            wr, wi = wr + pr * cr - pi * ci, wi + pr * ci + pi * cr
            re_s[rows, st] = jnp.where(first, cr, pltpu.roll(wr, 1, 0))
            im_s[rows, st] = jnp.where(first, ci, pltpu.roll(wi, 1, 0))
            cr, ci = wr[SUBLANES - 1:SUBLANES, :], wi[SUBLANES - 1:SUBLANES, :]
        car_re[:, st] = cr
        car_im[:, st] = ci
        if hi == n_tiles:
            sre_ref[:, st] = cr
            sim_ref[:, st] = ci

    def slab_outputs(j):
        cols = slice(j * LANES, (j + 1) * LANES)
        st = state_cols(j)
        s_cat = jnp.concatenate([re_s[:, st], im_s[:, st]], axis=1).astype(BF16)
        y = (jnp.dot(slab_lhs(j), wy_ref[j], preferred_element_type=F32)
             + jnp.dot(s_cat, wc_ref[j], preferred_element_type=F32))
        for t in range(row_tokens):
            yt = y[:, t * LANES:(t + 1) * LANES] + d_ref[:, cols] * hs[t][:, cols]
            gel_s[t * r:(t + 1) * r, cols] = jax.nn.gelu(yt).astype(BF16)

    if n_seq == 1:
        @pl.when(pl.program_id(1) == 0)
        def _():
            car_re[...] = s0re_ref[...]
            car_im[...] = s0im_ref[...]

        window(0)
        for j in range(n_slabs):
            if j + 1 < n_slabs:
                window(j + 1)
            if j >= 1:
                slab_outputs(j - 1)
            scan_rows(j, 0, n_tiles)
        slab_outputs(n_slabs - 1)
    else:
        for j in range(n_slabs):
            window(j)
        a_re = coef_ref[0][1:2, :]
        a_im = coef_ref[1][1:2, :]
        cr, ci = s0re_ref[...], s0im_ref[...]
        for n in range(r // n_seq):
            rows = slice(n * n_seq, (n + 1) * n_seq)
            wr, wi = re_s[rows, :], im_s[rows, :]
            re_s[rows, :] = cr
            im_s[rows, :] = ci
            cr, ci = a_re * cr - a_im * ci + wr, a_re * ci + a_im * cr + wi
        sre_ref[...] = cr
        sim_ref[...] = ci
        for j in range(n_slabs):
            slab_outputs(j)
    z = jnp.dot(gel_s[...], wglu_ref[...], preferred_element_type=F32) + bglu_ref[...]
    for t in range(row_tokens):
        zt = z[t * r:(t + 1) * r, :]
        o_ref[:, t * d:(t + 1) * d] = xs[t] + zt[:, :d] * jax.nn.sigmoid(zt[:, d:])


def _s5_rows_mix(x, g, s0re, s0im, coef, wb, wy, wc, dskip, wglu, bglu, *, batch, rows_per_seq,
                 r, row_tokens, step_major, name):
    rows, width = x.shape
    d = width // row_tokens
    n_state = s0re.shape[-1]
    if step_major:
        grid = (1, 1)
        r = rows
        state_spec = pl.BlockSpec((batch, n_state), lambda b, i: (0, 0))
        x_spec = pl.BlockSpec((rows, width), lambda b, i: (0, 0))
    else:
        nt = rows_per_seq // r
        grid = (batch, nt)
        state_spec = pl.BlockSpec((None, 1, n_state), lambda b, i: (b, 0, 0))
        x_spec = pl.BlockSpec((r, width), lambda b, i: (b * nt + i, 0))
    consts = (g, coef, wb, wy, wc, dskip, wglu, bglu)
    x1, sre, sim = pl.pallas_call(
        functools.partial(_s5_rows_kernel, row_tokens=row_tokens,
                          n_seq=batch if step_major else 1),
        grid=grid,
        in_specs=[x_spec, _const_spec(g.shape), state_spec, state_spec]
                 + [_const_spec(c.shape) for c in consts[1:]],
        out_specs=[x_spec, state_spec, state_spec],
        out_shape=[jax.ShapeDtypeStruct((rows, width), F32),
                   jax.ShapeDtypeStruct(s0re.shape, F32),
                   jax.ShapeDtypeStruct(s0im.shape, F32)],
        scratch_shapes=[pltpu.VMEM((r, n_state), F32), pltpu.VMEM((r, n_state), F32),
                        pltpu.VMEM((r * row_tokens, d), BF16),
                        pltpu.VMEM((1, n_state), F32), pltpu.VMEM((1, n_state), F32)],
        compiler_params=_params(("parallel", "arbitrary")),
        name=name,
    )(x, g, s0re, s0im, *consts[1:])
    return x1, sre, sim


def _s5_row_tables(a_re, a_im, log_dt, b_re, b_im, c_re, c_im, row_tokens):
    groups, n_p = a_re.shape
    p_tok = row_tokens
    n_slabs = groups // SLAB_GROUPS
    dt = jnp.exp(log_dt)[:, None]
    ks = jnp.arange(p_tok * SUBLANES + 1, dtype=F32)[:, None, None]
    mag = jnp.exp((a_re * dt)[None] * ks)
    ang = (a_im * dt)[None] * ks
    pw_re, pw_im = mag * jnp.cos(ang), mag * jnp.sin(ang)
    den = a_re * a_re + a_im * a_im
    q_re = ((pw_re[1] - 1.0) * a_re + pw_im[1] * a_im) / den
    q_im = (pw_im[1] * a_re - (pw_re[1] - 1.0) * a_im) / den
    bb_re = q_re[:, :, None] * b_re - q_im[:, :, None] * b_im
    bb_im = q_re[:, :, None] * b_im + q_im[:, :, None] * b_re

    rows = jnp.arange(SUBLANES)
    coef = []
    for shift in SCAN_SHIFTS:
        keep = (rows >= shift)[:, None].astype(F32)
        coef += [pw_re[p_tok * shift].reshape(1, -1) * keep,
                 pw_im[p_tok * shift].reshape(1, -1) * keep]
    pick = lambda pw, idx: jnp.stack([pw[k] for k in idx])
    carry = [p_tok * (r + 1) for r in range(SUBLANES)]
    coef += [pick(pw_re, carry).reshape(SUBLANES, -1), pick(pw_im, carry).reshape(SUBLANES, -1)]
    coef = jnp.stack(coef)

    back = [p_tok - 1 - s for s in range(p_tok)]
    er, ei = pick(pw_re, back)[:, :, :, None], pick(pw_im, back)[:, :, :, None]
    lay_b = lambda m: m.reshape(p_tok, n_slabs, SLAB_GROUPS, n_p, SSM_GROUP).transpose(
        1, 0, 2, 4, 3).reshape(n_slabs, -1, n_p)
    wb = _expand_groups(jnp.concatenate([lay_b(er * bb_re - ei * bb_im),
                                         lay_b(er * bb_im + ei * bb_re)], axis=2), SSM_GROUP, n_p)

    kr, ki = pw_re[:p_tok + 1][:, :, None, :], pw_im[:p_tok + 1][:, :, None, :]
    ck_re, ck_im = c_re * kr - c_im * ki, c_re * ki + c_im * kr

    lay_c = lambda m: m.reshape(p_tok, n_slabs, SLAB_GROUPS, SSM_GROUP, n_p).transpose(
        1, 2, 4, 0, 3).reshape(n_slabs, SLAB_GROUPS * n_p, -1)
    wc = _expand_groups(jnp.concatenate([lay_c(ck_re[1:]), -lay_c(ck_im[1:])], axis=1),
                        n_p, SSM_GROUP)

    bt_re, bt_im = bb_re.transpose(0, 2, 1)[None, :, :, None, :], bb_im.transpose(0, 2, 1)[
        None, :, :, None, :]
    lag = jnp.sum(ck_re[:p_tok, :, None] * bt_re - ck_im[:p_tok, :, None] * bt_im, axis=-1)
    lag = lag.reshape(p_tok, n_slabs, SLAB_GROUPS * SSM_GROUP, SSM_GROUP)
    zero = jnp.zeros_like(lag[0])
    toep = jnp.concatenate(
        [jnp.concatenate([lag[t - s] if t >= s else zero for t in range(p_tok)], axis=2)
         for s in range(p_tok)], axis=1)
    wy = _expand_groups(toep, SSM_GROUP, SSM_GROUP)
    return coef, wb, wy, wc


def kernel(x_prompt, x_sample, cache_k, cache_v, page_table, state_ssm_re, state_ssm_im,
           norm_mix, norm_ffn, norm_final, w_in_even, w_out_even, sgu_norm, sgu_w, sgu_b,
           lambda_q1, lambda_k1, lambda_q2, lambda_k2, attn_subln,
           ssm_a_re, ssm_a_im, ssm_log_dt, ssm_b_re, ssm_b_im, ssm_c_re, ssm_c_im, ssm_d,
           w_glu, b_glu, w_ffn_in, w_ffn_out):
    batch, seq, d = x_prompt.shape
    dec_batch, t_new, _ = x_sample.shape
    depth = norm_mix.shape[0]
    assert depth == 2 and seq % CHUNK == 0 and t_new == SUBLANES
    tp = batch * seq
    ts = dec_batch * t_new
    xp = x_prompt.reshape(tp, d)
    xs = x_sample.reshape(ts, d)
    tm_prompt = TOKEN_TILE
    ff_chunk = FF_CHUNK
    row = lambda v: v.reshape(1, -1)

    lam_init = 0.8 - 0.6 * math.exp(-0.3 * 0)
    w_in0 = w_in_even[0].astype(BF16)
    a_width = sgu_norm.shape[1]
    gw = a_width // A_GROUPS
    tril = jnp.tril(jnp.ones((CHUNK, CHUNK), bool))
    mix_p = jnp.where(tril, sgu_w[0], 0).astype(BF16)
    bias_p = jnp.repeat(sgu_b[0].T, gw, axis=1)
    small = jnp.where(tril[:t_new, :t_new], sgu_w[0][:, :t_new, :t_new], 0)
    row_i = lax.broadcasted_iota(jnp.int32, (ts, ts), 0)
    col_i = lax.broadcasted_iota(jnp.int32, (ts, ts), 1)
    rep = (row_i[:, :t_new] % t_new == col_i[:, :t_new]).astype(F32)
    tiled = jnp.einsum('at,gts,bs->gab', rep, small, rep, precision=lax.Precision.HIGHEST)
    mix_s = jnp.where(row_i // t_new == col_i // t_new, tiled, 0.0).astype(BF16)
    bias_s = jnp.tile(bias_p[:t_new], (dec_batch, 1))
    lam_vecs = [row(lambda_q1[0]), row(lambda_k1[0]), row(lambda_q2[0]), row(lambda_k2[0]),
                row(attn_subln[0])]

    later_weights = (w_ffn_in.reshape(-1, w_ffn_in.shape[-1]),
                     w_ffn_out.reshape(-1, w_ffn_out.shape[-1]), w_glu[0], w_out_even[0])
    (a_p, q_p, k_p, vb_p, kb_p, vbb_p), (w_ffn_in_b, w_ffn_out_b, w_glu0, w_out0) = _even_in(
        xp, row(norm_mix[0]), w_in0, row(sgu_norm[0]), mix_p, bias_p, tm=tm_prompt, prompt=True,
        cast_weights=later_weights)
    w_ffn_in_b = w_ffn_in_b.reshape(w_ffn_in.shape)
    w_ffn_out_b = w_ffn_out_b.reshape(w_ffn_out.shape)
    (a_s, q_s, k_s, vb_s, vn_s), _ = _even_in(
        xs, row(norm_mix[0]), w_in0, row(sgu_norm[0]), mix_s, bias_s, tm=ts, prompt=False)
    n_layers, n_phys, page_size, heads, _ = cache_k.shape
    page_view = (n_layers, n_phys, page_size * heads, HEAD_DIM)
    b_p, b_s = _attention(page_table, q_p, kb_p, vbb_p, q_s, k_s, vb_s,
                          cache_k.reshape(page_view), cache_v.reshape(page_view), 0, *lam_vecs,
                          batch=batch, seq=seq, tq=ATTN_Q_TILE, t_new=t_new, lam_init=lam_init)
    xp, xs = _mix_ffn(xp, a_p, b_p, xs, a_s, b_s, w_out0, row(norm_ffn[0]), w_ffn_in_b,
                      w_ffn_out_b, layer=0, tm=tm_prompt, ff_chunk=ff_chunk,
                      row_tokens=ROW_TOKENS)

    groups, n_p = ssm_a_re.shape[1:]
    n_state = groups * n_p
    ssm = (ssm_a_re[0], ssm_a_im[0], ssm_log_dt[0], ssm_b_re[0], ssm_b_im[0], ssm_c_re[0],
           ssm_c_im[0])
    s5_args = (*_s5_row_tables(*ssm, ROW_TOKENS), row(ssm_d[0]), w_glu0, row(b_glu[0]))
    zeros = jnp.zeros((batch, 1, n_state), F32)
    xp, p_re, p_im = _s5_rows_mix(xp, row(norm_mix[1]), zeros, zeros, *s5_args, batch=batch,
                                  rows_per_seq=seq // ROW_TOKENS, r=S5_BLOCK_ROWS,
                                  row_tokens=ROW_TOKENS, step_major=False, name="s5_prompt")
    steps = t_new // ROW_TOKENS
    pack = lambda v: v.reshape(dec_batch, steps, -1).transpose(1, 0, 2).reshape(
        steps * dec_batch, -1)
    unpack = lambda v: v.reshape(steps, dec_batch, -1).transpose(1, 0, 2).reshape(ts, d)
    xs, s_re, s_im = _s5_rows_mix(pack(xs), row(norm_mix[1]),
                                  state_ssm_re[0].reshape(dec_batch, n_state),
                                  state_ssm_im[0].reshape(dec_batch, n_state), *s5_args,
                                  batch=dec_batch, rows_per_seq=steps, r=None,
                                  row_tokens=ROW_TOKENS, step_major=True, name="s5_sample")
    xs = unpack(xs)
    yp, ys = _ffn_final(xp, xs, row(norm_ffn[1]), w_ffn_in_b, w_ffn_out_b, row(norm_final),
                        layer=1, tm=tm_prompt, ff_chunk=ff_chunk, row_tokens=ROW_TOKENS)

    st = lambda s, n: s.reshape(1, n, groups, n_p)
    return (yp.reshape(batch, seq, d), ys.reshape(dec_batch, t_new, d),
            k_p.reshape(1, batch, seq, heads, HEAD_DIM), vb_p.reshape(1, batch, seq, heads, HEAD_DIM),
            k_s.reshape(1, dec_batch, t_new, heads, HEAD_DIM),
            vb_s.reshape(1, dec_batch, t_new, heads, HEAD_DIM),
            vn_s.reshape(1, dec_batch, t_new, a_width),
            st(p_re, batch), st(p_im, batch), st(s_re, dec_batch), st(s_im, dec_batch))
```
